```python
import math
import jax
import jax.numpy as jnp
from jax import lax
import numpy as np


D_MODEL = 4096
BATCH = 2
SEQ = 4096
DEPTH = 1
DEC_BATCH = 32
DEC_SEQ = 16
PAST_LEN = 1024

CHUNK = 64
SSD_D_INNER = 4096
SSD_HEAD_DIM = 64
SSD_HEADS = 64
SSD_GROUPS = 8
SSD_HEADS_PER_GROUP = 8
SSD_STATE = 128
SSD_CONV_W = 4
SSD_CONV_CH = SSD_D_INNER + 2 * SSD_GROUPS * SSD_STATE
SSD_BLOCK = CHUNK
N_HEADS = 32
KV_HEADS = 8
HEAD_DIM = 128
Q_PER_KV = N_HEADS // KV_HEADS
IDX_HEADS = 16
IDX_DIM = 64
TOPK_MAX = 256
ATTN_QBLOCK = 128
MEM_TOKENS = 256
MEM_HEADS = 4
MEM_HEAD_DIM = D_MODEL // MEM_HEADS
D_FF = 11008
ALPHA = (2.0 * DEPTH) ** 0.25
BETA = (8.0 * DEPTH) ** -0.25
LN_EPS = 1e-5
RMS_EPS = 1e-5

IN_SIZES = (SSD_D_INNER, SSD_CONV_CH, SSD_HEADS,
            N_HEADS * HEAD_DIM, KV_HEADS * HEAD_DIM, KV_HEADS * HEAD_DIM,
            IDX_HEADS * IDX_DIM, IDX_HEADS, IDX_DIM,
            D_MODEL, D_MODEL)
D_IN_PROJ = (SSD_D_INNER + SSD_CONV_CH + SSD_HEADS + N_HEADS * HEAD_DIM + 2 * KV_HEADS * HEAD_DIM
             + IDX_HEADS * IDX_DIM + IDX_HEADS + IDX_DIM + 2 * D_MODEL)

kernel_name = 'chunk_streaming_ssd_dsa_hybrid_step'


def layer_norm(x, g, b):
    xf = x.astype(jnp.float32)
    mu = jnp.mean(xf, axis=-1, keepdims=True)
    var = jnp.mean(jnp.square(xf - mu), axis=-1, keepdims=True)
    return ((xf - mu) * lax.rsqrt(var + LN_EPS) * g + b).astype(x.dtype)


def swiglu(x, w1, w3, w2):
    return (jax.nn.silu(x @ w1) * (x @ w3)) @ w2


def split_cols(u, sizes):
    parts = []
    start = 0
    for n in sizes:
        parts.append(u[..., start:start + n])
        start += n
    return parts


def causal_depthwise_conv(xbc, conv_prev, conv_w, conv_b):
    L = xbc.shape[1]
    ext = jnp.concatenate([conv_prev.astype(xbc.dtype), xbc], axis=1)
    out = conv_b + conv_w[0] * ext[:, 0:L]
    for j in range(1, SSD_CONV_W):
        out = out + conv_w[j] * ext[:, j:j + L]
    new_prev = ext[:, ext.shape[1] - (SSD_CONV_W - 1):]
    return out, new_prev


def ssd_scan(xs, dt, a, bm, cm, h0):
    f32 = jnp.float32
    b, L = xs.shape[0], xs.shape[1]
    G, R, P, N = SSD_GROUPS, SSD_HEADS_PER_GROUP, SSD_HEAD_DIM, SSD_STATE
    blk = min(SSD_BLOCK, L)
    nc = L // blk
    xdt = (xs.astype(f32) * dt[..., None]).reshape(b, nc, blk, G, R, P)
    bm = bm.astype(f32).reshape(b, nc, blk, G, N)
    cm = cm.astype(f32).reshape(b, nc, blk, G, N)
    a_cum = jnp.cumsum((dt * a).reshape(b, nc, blk, G, R), axis=2)
    causal = jnp.tril(jnp.ones((blk, blk), dtype=bool))
    seg = a_cum[:, :, :, None] - a_cum[:, :, None, :]
    decay = jnp.exp(jnp.where(causal[:, :, None, None], seg, -jnp.inf))
    cb = jnp.einsum('bclgn,bcsgn->bclsg', cm, bm)
    y_diag = jnp.einsum('bclsgr,bcsgrp->bclgrp', cb[..., None] * decay, xdt)
    decay_to_end = jnp.exp(a_cum[:, :, -1:] - a_cum)
    block_states = jnp.einsum('bclgn,bclgrp->bcgrpn', bm, xdt * decay_to_end[..., None])
    block_decay = jnp.exp(a_cum[:, :, -1])

    def step(h, inp):
        dec, st = inp
        return dec[..., None, None] * h + st, h

    h_last, h_prev = lax.scan(step, h0.astype(f32),
                              (jnp.moveaxis(block_decay, 1, 0), jnp.moveaxis(block_states, 1, 0)))
    h_prev = jnp.moveaxis(h_prev, 0, 1)
    y_off = jnp.einsum('bclgn,bcgrpn->bclgrp', cm, h_prev) * jnp.exp(a_cum)[..., None]
    y = (y_diag + y_off).reshape(b, L, G, R, P)
    return y, h_last


def ssd_mixer(z, xbc, dt_raw, conv_prev, h0, conv_w, conv_b, dt_bias, a_log, d_skip, norm_g):
    f32 = jnp.float32
    b, L, _ = z.shape
    G, R, P, N = SSD_GROUPS, SSD_HEADS_PER_GROUP, SSD_HEAD_DIM, SSD_STATE
    conv, new_conv = causal_depthwise_conv(xbc, conv_prev, conv_w, conv_b)
    conv = jax.nn.silu(conv)
    xs, bm, cm = split_cols(conv, (SSD_D_INNER, G * N, G * N))
    xs = xs.reshape(b, L, G, R, P)
    bm = bm.reshape(b, L, G, N)
    cm = cm.reshape(b, L, G, N)
    dt = jax.nn.softplus((dt_raw + dt_bias).astype(f32)).reshape(b, L, G, R)
    a = -jnp.exp(a_log.astype(f32)).reshape(G, R)
    y, h_last = ssd_scan(xs, dt, a, bm, cm, h0.reshape(b, G, R, P, N))
    y = y + d_skip.astype(f32).reshape(G, R)[:, :, None] * xs.astype(f32)
    yg = (y.reshape(b, L, SSD_D_INNER) * jax.nn.silu(z.astype(f32))).reshape(b, L, G, SSD_D_INNER // G)
    yg = yg * lax.rsqrt(jnp.mean(jnp.square(yg), axis=-1, keepdims=True) + RMS_EPS)
    out = (yg.reshape(b, L, SSD_D_INNER) * norm_g).astype(z.dtype)
    return out, new_conv, h_last.reshape(b, SSD_HEADS, P, N).astype(h0.dtype)


def dsa_query_block(q_b, qi_b, wi_b, pos_b, k_all, v_all, ki_all, n_sel):
    f32 = jnp.float32
    L = k_all.shape[1]
    limit = (pos_b // CHUNK + 1) * CHUNK
    admissible = jnp.arange(L)[None, :] < limit[:, None]
    idx_logits = jnp.einsum('bqhd,bsd->bqhs', qi_b, ki_all).astype(f32) * (IDX_DIM ** -0.5)
    score = jnp.einsum('bqh,bqhs->bqs', wi_b.astype(f32) * (IDX_HEADS ** -0.5), jax.nn.relu(idx_logits))
    score = jnp.where(admissible[None], score, -jnp.inf)
    _, sel = lax.top_k(score, n_sel)
    valid = sel < limit[None, :, None]
    gather = jax.vmap(lambda rows, idx: rows[idx])
    k_sel = gather(k_all, sel)
    v_sel = gather(v_all, sel)
    logits = jnp.einsum('bqhgd,bqshd->bqhgs', q_b, k_sel).astype(f32) * (HEAD_DIM ** -0.5)
    logits = jnp.where(valid[:, :, None, None, :], logits, -jnp.inf)
    probs = jax.nn.softmax(logits, axis=-1).astype(v_sel.dtype)
    out = jnp.einsum('bqhgs,bqshd->bqhgd', probs, v_sel)
    return out.reshape(out.shape[0], out.shape[1], N_HEADS * HEAD_DIM)


def dsa_attention(q, qi, wi, pos0, k_all, v_all, ki_all):
    b, Lq = q.shape[0], q.shape[1]
    L = k_all.shape[1]
    n_sel = min(TOPK_MAX, L // 4)
    qb = min(ATTN_QBLOCK, Lq)
    nb = Lq // qb

    def blocks(t):
        return jnp.moveaxis(t.reshape((b, nb, qb) + t.shape[2:]), 1, 0)

    q5 = q.reshape(b, Lq, KV_HEADS, Q_PER_KV, HEAD_DIM)
    qi4 = qi.reshape(b, Lq, IDX_HEADS, IDX_DIM)
    pos = (pos0 + jnp.arange(Lq)).reshape(nb, qb)
    out = lax.map(lambda blk: dsa_query_block(blk[0], blk[1], blk[2], blk[3], k_all, v_all, ki_all, n_sel),
                  (blocks(q5), blocks(qi4), blocks(wi), pos))
    return jnp.moveaxis(out, 0, 1).reshape(b, Lq, N_HEADS * HEAD_DIM)


def memory_attention(x, mem_k, mem_v, w_mq, w_mo):
    b, L, _ = x.shape
    q = (x @ w_mq).reshape(b, L, MEM_HEADS, MEM_HEAD_DIM)
    logits = jnp.einsum('bqhd,bmhd->bhqm', q, mem_k).astype(jnp.float32) * (MEM_HEAD_DIM ** -0.5)
    probs = jax.nn.softmax(logits, axis=-1).astype(mem_v.dtype)
    o = jnp.einsum('bhqm,bmhd->bqhd', probs, mem_v).reshape(b, L, D_MODEL)
    return o @ w_mo


def trunk_layer(x, mem_k, mem_v, conv_prev, h0, k_past, v_past, ki_past, p):
    b, Lq, _ = x.shape
    pos0 = k_past.shape[1]
    x = layer_norm(ALPHA * x + 0.5 * swiglu(x, p['ffn1_w1'], p['ffn1_w3'], p['ffn1_w2']), p['ln1_g'], p['ln1_b'])
    u = x @ p['w_in']
    z, xbc, dt_raw, q, k, v, qi, wi, ki, g_ssd, g_att = split_cols(u, IN_SIZES)
    y_ssd, new_conv, new_h = ssd_mixer(z, xbc, dt_raw, conv_prev, h0, p['conv_w'], p['conv_b'],
                                       p['dt_bias'], p['a_log'], p['d_skip'], p['ssd_norm_g'])
    k = k.reshape(b, Lq, KV_HEADS, HEAD_DIM)
    v = v.reshape(b, Lq, KV_HEADS, HEAD_DIM)
    y_att = dsa_attention(q, qi, wi, pos0,
                          jnp.concatenate([k_past.astype(k.dtype), k], axis=1),
                          jnp.concatenate([v_past.astype(v.dtype), v], axis=1),
                          jnp.concatenate([ki_past.astype(ki.dtype), ki], axis=1))
    merged = jax.nn.sigmoid(g_ssd) * (y_ssd @ p['w_ssd_br']) + jax.nn.sigmoid(g_att) * (y_att @ p['w_att_br'])
    x = layer_norm(ALPHA * x + merged @ p['w_out'], p['ln2_g'], p['ln2_b'])
    x = layer_norm(ALPHA * x + memory_attention(x, mem_k, mem_v, p['w_mq'], p['w_mo']), p['ln3_g'], p['ln3_b'])
    x = layer_norm(ALPHA * x + 0.5 * swiglu(x, p['ffn2_w1'], p['ffn2_w3'], p['ffn2_w2']), p['ln4_g'], p['ln4_b'])
    return x, new_conv, new_h, k, v, ki


def setup_inputs(seed: int = 0) -> dict:
    key = jax.random.key(seed)
    ks = iter(jax.random.split(key, 48))
    f32 = jnp.float32

    def nrm(shape, scale):
        return jax.random.normal(next(ks), shape, f32) * scale

    def gain():
        return 1.0 + nrm((DEPTH, D_MODEL), 0.02)

    def bias():
        return nrm((DEPTH, D_MODEL), 0.02)

    dt0 = jnp.exp(jax.random.uniform(next(ks), (DEPTH, SSD_HEADS), f32,
                                     minval=math.log(1e-3), maxval=math.log(1e-1)))
    return {
        'x_prompt': nrm((BATCH, SEQ, D_MODEL), 1.0),
        'x_sample': nrm((DEC_BATCH, DEC_SEQ, D_MODEL), 1.0),
        'mem_prompt': nrm((BATCH, MEM_TOKENS, D_MODEL), 1.0),
        'cache_k': nrm((DEPTH, DEC_BATCH, PAST_LEN, KV_HEADS, HEAD_DIM), 1.0),
        'cache_v': nrm((DEPTH, DEC_BATCH, PAST_LEN, KV_HEADS, HEAD_DIM), 1.0),
        'cache_idx_k': nrm((DEPTH, DEC_BATCH, PAST_LEN, IDX_DIM), 1.0),
        'cache_mem_k': nrm((DEPTH, DEC_BATCH, MEM_TOKENS, MEM_HEADS, MEM_HEAD_DIM), 1.0),
        'cache_mem_v': nrm((DEPTH, DEC_BATCH, MEM_TOKENS, MEM_HEADS, MEM_HEAD_DIM), 1.0),
        'state_ssm': nrm((DEPTH, DEC_BATCH, SSD_HEADS, SSD_HEAD_DIM, SSD_STATE), 0.3),
        'state_conv': nrm((DEPTH, DEC_BATCH, SSD_CONV_W - 1, SSD_CONV_CH), 1.0),
        'ln1_g': gain(),
        'ln1_b': bias(),
        'ffn1_w1': nrm((DEPTH, D_MODEL, D_FF), D_MODEL ** -0.5),
        'ffn1_w3': nrm((DEPTH, D_MODEL, D_FF), D_MODEL ** -0.5),
        'ffn1_w2': nrm((DEPTH, D_FF, D_MODEL), BETA * D_FF ** -0.5),
        'w_in': nrm((DEPTH, D_MODEL, D_IN_PROJ), D_MODEL ** -0.5),
        'conv_w': nrm((DEPTH, SSD_CONV_W, SSD_CONV_CH), SSD_CONV_W ** -0.5),
        'conv_b': nrm((DEPTH, SSD_CONV_CH), 0.02),
        'dt_bias': dt0 + jnp.log(-jnp.expm1(-dt0)),
        'a_log': jnp.log(jax.random.uniform(next(ks), (DEPTH, SSD_HEADS), f32, minval=1.0, maxval=16.0)),
        'd_skip': 1.0 + nrm((DEPTH, SSD_HEADS), 0.02),
        'ssd_norm_g': 1.0 + nrm((DEPTH, SSD_D_INNER), 0.02),
        'w_ssd_br': nrm((DEPTH, SSD_D_INNER, D_MODEL), SSD_D_INNER ** -0.5),
        'w_att_br': nrm((DEPTH, N_HEADS * HEAD_DIM, D_MODEL), (N_HEADS * HEAD_DIM) ** -0.5),
        'w_out': nrm((DEPTH, D_MODEL, D_MODEL), BETA * D_MODEL ** -0.5),
        'ln2_g': gain(),
        'ln2_b': bias(),
        'w_mq': nrm((DEPTH, D_MODEL, D_MODEL), D_MODEL ** -0.5),
        'w_mk': nrm((DEPTH, D_MODEL, D_MODEL), D_MODEL ** -0.5),
        'w_mv': nrm((DEPTH, D_MODEL, D_MODEL), D_MODEL ** -0.5),
        'w_mo': nrm((DEPTH, D_MODEL, D_MODEL), BETA * D_MODEL ** -0.5),
        'ln3_g': gain(),
        'ln3_b': bias(),
        'ffn2_w1': nrm((DEPTH, D_MODEL, D_FF), D_MODEL ** -0.5),
        'ffn2_w3': nrm((DEPTH, D_MODEL, D_FF), D_MODEL ** -0.5),
        'ffn2_w2': nrm((DEPTH, D_FF, D_MODEL), BETA * D_FF ** -0.5),
        'ln4_g': gain(),
        'ln4_b': bias(),
    }


def reference(x_prompt, x_sample, mem_prompt, cache_k, cache_v, cache_idx_k, cache_mem_k, cache_mem_v,
              state_ssm, state_conv, ln1_g, ln1_b, ffn1_w1, ffn1_w3, ffn1_w2, w_in, conv_w, conv_b,
              dt_bias, a_log, d_skip, ssd_norm_g, w_ssd_br, w_att_br, w_out, ln2_g, ln2_b,
              w_mq, w_mk, w_mv, w_mo, ln3_g, ln3_b, ffn2_w1, ffn2_w3, ffn2_w2, ln4_g, ln4_b):
    b_p = x_prompt.shape[0]
    dtp = x_prompt.dtype
    y_prompt = x_prompt
    y_sample = x_sample
    p_ssm, p_conv, p_k, p_v, p_ki, p_mk, p_mv = [], [], [], [], [], [], []
    s_ssm, s_conv, s_k, s_v, s_ki = [], [], [], [], []
    for l in range(DEPTH):
        p = dict(ln1_g=ln1_g[l], ln1_b=ln1_b[l], ffn1_w1=ffn1_w1[l], ffn1_w3=ffn1_w3[l], ffn1_w2=ffn1_w2[l],
                 w_in=w_in[l], conv_w=conv_w[l], conv_b=conv_b[l], dt_bias=dt_bias[l], a_log=a_log[l],
                 d_skip=d_skip[l], ssd_norm_g=ssd_norm_g[l], w_ssd_br=w_ssd_br[l], w_att_br=w_att_br[l],
                 w_out=w_out[l], ln2_g=ln2_g[l], ln2_b=ln2_b[l], w_mq=w_mq[l], w_mo=w_mo[l],
                 ln3_g=ln3_g[l], ln3_b=ln3_b[l], ffn2_w1=ffn2_w1[l], ffn2_w3=ffn2_w3[l], ffn2_w2=ffn2_w2[l],
                 ln4_g=ln4_g[l], ln4_b=ln4_b[l])
        mk_p = (mem_prompt @ w_mk[l]).reshape(b_p, MEM_TOKENS, MEM_HEADS, MEM_HEAD_DIM)
        mv_p = (mem_prompt @ w_mv[l]).reshape(b_p, MEM_TOKENS, MEM_HEADS, MEM_HEAD_DIM)
        y_prompt, pc, ph, pk, pv, pki = trunk_layer(
            y_prompt, mk_p, mv_p,
            jnp.zeros((b_p, SSD_CONV_W - 1, SSD_CONV_CH), dtp),
            jnp.zeros((b_p, SSD_HEADS, SSD_HEAD_DIM, SSD_STATE), dtp),
            jnp.zeros((b_p, 0, KV_HEADS, HEAD_DIM), dtp),
            jnp.zeros((b_p, 0, KV_HEADS, HEAD_DIM), dtp),
            jnp.zeros((b_p, 0, IDX_DIM), dtp), p)
        y_sample, sc, sh, sk, sv, ski = trunk_layer(
            y_sample, cache_mem_k[l], cache_mem_v[l], state_conv[l], state_ssm[l],
            cache_k[l], cache_v[l], cache_idx_k[l], p)
        p_ssm.append(ph)
        p_conv.append(pc)
        p_k.append(pk)
        p_v.append(pv)
        p_ki.append(pki)
        p_mk.append(mk_p)
        p_mv.append(mv_p)
        s_ssm.append(sh)
        s_conv.append(sc)
        s_k.append(sk)
        s_v.append(sv)
        s_ki.append(ski)
    return (y_prompt, y_sample,
            jnp.stack(p_ssm), jnp.stack(p_conv), jnp.stack(p_k), jnp.stack(p_v), jnp.stack(p_ki),
            jnp.stack(p_mk), jnp.stack(p_mv),
            jnp.stack(s_ssm), jnp.stack(s_conv), jnp.stack(s_k), jnp.stack(s_v), jnp.stack(s_ki))
```

```python
import functools

import jax
import jax.numpy as jnp
from jax import lax
from jax.experimental import pallas as pl
from jax.experimental.pallas import tpu as pltpu

F32 = jnp.float32
BF16 = jnp.bfloat16
I32 = jnp.int32

DEPTH = 1
CHUNK = 64
SSD_HEAD_DIM = 64
SSD_HEADS_PER_GROUP = 8
SSD_STATE = 128
SSD_CONV_W = 4
HEAD_DIM = 128
Q_PER_KV = 4
IDX_HEADS = 16
IDX_DIM = 64
TOPK_MAX = 256
MEM_HEADS = 4
ALPHA = (2.0 * DEPTH) ** 0.25
LN_EPS = 1e-5
RMS_EPS = 1e-5

LANES = 128
VMEM_LIMIT = 56 * 1024 * 1024

NEG = -1e30
INT_MIN = -2147483648

DSA_QB = 128
DSA_KB = 512
LN_ROWS = 256

SSD_HEADS = 64
US_DT = 0
US_WI = US_DT + SSD_HEADS
US_KI = US_WI + IDX_HEADS
US_W = 256


def _cparams(sem):
    return pltpu.CompilerParams(dimension_semantics=sem, vmem_limit_bytes=VMEM_LIMIT)


def _dot(a, b):
    return jnp.dot(a, b, preferred_element_type=F32)


def _dot_nt(a, b):
    return lax.dot_general(a, b, (((1,), (1,)), ((), ())), preferred_element_type=F32)


def _dot_tn(a, b, precision=None):
    return lax.dot_general(a, b, (((0,), (0,)), ((), ())), preferred_element_type=F32,
                           precision=precision)


def _dot_hi(a, b):
    return jnp.dot(a, b, preferred_element_type=F32, precision=lax.Precision.HIGHEST)


def _sigmoid(x):
    return 1.0 / (1.0 + jnp.exp(-x))


def _silu(x):
    return x * _sigmoid(x)


def _softplus(x):
    return jnp.maximum(x, 0.0) + jnp.log1p(jnp.exp(-jnp.abs(x)))


def _mm_kernel(x_ref, w_ref, o_ref):
    o_ref[...] = _dot(x_ref[...], w_ref[...]).astype(o_ref.dtype)


def _matmul(x, w, *, tm, tn, out_dtype):
    M, K = x.shape
    N = w.shape[1]
    return pl.pallas_call(
        _mm_kernel,
        grid=(M // tm, N // tn),
        in_specs=[pl.BlockSpec((tm, K), lambda i, j: (i, 0)),
                  pl.BlockSpec((K, tn), lambda i, j: (0, j))],
        out_specs=pl.BlockSpec((tm, tn), lambda i, j: (i, j)),
        out_shape=jax.ShapeDtypeStruct((M, N), out_dtype),
        compiler_params=_cparams(("parallel", "arbitrary")),
    )(x, w)


def _ffn_up_kernel(x_ref, w1_ref, w3_ref, o_ref):
    x = x_ref[...]
    a = _dot(x, w1_ref[...])
    b = _dot(x, w3_ref[...])
    o_ref[...] = (_silu(a) * b).astype(o_ref.dtype)


def _ffn_up(x, w1, w3, *, tm, tn):
    M, K = x.shape
    N = w1.shape[1]
    return pl.pallas_call(
        _ffn_up_kernel,
        grid=(M // tm, N // tn),
        in_specs=[pl.BlockSpec((tm, K), lambda i, j: (i, 0)),
                  pl.BlockSpec((K, tn), lambda i, j: (0, j)),
                  pl.BlockSpec((K, tn), lambda i, j: (0, j))],
        out_specs=pl.BlockSpec((tm, tn), lambda i, j: (i, j)),
        out_shape=jax.ShapeDtypeStruct((M, N), BF16),
        compiler_params=_cparams(("parallel", "arbitrary")),
    )(x, w1, w3)


def _gate_kernel(ys_ref, ya_ref, ws_ref, wa_ref, gs_ref, ga_ref, o_ref):
    s = _dot(ys_ref[...], ws_ref[...])
    a = _dot(ya_ref[...], wa_ref[...])
    o_ref[...] = (_sigmoid(gs_ref[...]) * s + _sigmoid(ga_ref[...]) * a).astype(o_ref.dtype)


def _gated_merge(y_ssd, y_att, w_ssd, w_att, u_main, gs_col, ga_col, *, tm, tn):
    M, K = y_ssd.shape
    N = w_ssd.shape[1]
    gs_blk, ga_blk = gs_col // tn, ga_col // tn
    return pl.pallas_call(
        _gate_kernel,
        grid=(M // tm, N // tn),
        in_specs=[pl.BlockSpec((tm, K), lambda i, j: (i, 0)),
                  pl.BlockSpec((tm, K), lambda i, j: (i, 0)),
                  pl.BlockSpec((K, tn), lambda i, j: (0, j)),
                  pl.BlockSpec((K, tn), lambda i, j: (0, j)),
                  pl.BlockSpec((tm, tn), lambda i, j: (i, gs_blk + j)),
                  pl.BlockSpec((tm, tn), lambda i, j: (i, ga_blk + j))],
        out_specs=pl.BlockSpec((tm, tn), lambda i, j: (i, j)),
        out_shape=jax.ShapeDtypeStruct((M, N), BF16),
        compiler_params=_cparams(("parallel", "arbitrary")),
    )(y_ssd, y_att, w_ssd, w_att, u_main, u_main)


def _ln_kernel(x_ref, d_ref, g_ref, b_ref, o_ref, ob_ref, *, scale):
    y = ALPHA * x_ref[...] + scale * d_ref[...]
    mu = jnp.mean(y, axis=-1, keepdims=True)
    yc = y - mu
    var = jnp.mean(yc * yc, axis=-1, keepdims=True)
    o = yc * lax.rsqrt(var + LN_EPS) * g_ref[...] + b_ref[...]
    o_ref[...] = o
    ob_ref[...] = o.astype(BF16)


def _res_layer_norm(x, delta, g, b, *, scale):
    M, D = x.shape
    tr = max(r for r in range(8, LN_ROWS + 1, 8) if M % r == 0)
    row = pl.BlockSpec((tr, D), lambda i: (i, 0))
    vec = pl.BlockSpec((1, D), lambda i: (0, 0))
    return pl.pallas_call(
        functools.partial(_ln_kernel, scale=scale),
        grid=(M // tr,),
        in_specs=[row, row, vec, vec],
        out_specs=[row, row],
        out_shape=[jax.ShapeDtypeStruct((M, D), F32), jax.ShapeDtypeStruct((M, D), BF16)],
        compiler_params=_cparams(("parallel",)),
    )(x, delta, g.reshape(1, D), b.reshape(1, D))


def _memattn_kernel(q_ref, k_ref, v_ref, o_ref):
    dh = q_ref.shape[1] // MEM_HEADS
    scale = dh ** -0.5
    for h in range(MEM_HEADS):
        sl = slice(h * dh, (h + 1) * dh)
        q = q_ref[:, sl]
        k = k_ref[:, sl].astype(BF16)
        v = v_ref[:, sl].astype(BF16)
        s = _dot_nt(q, k) * scale
        m = jnp.max(s, axis=-1, keepdims=True)
        p = jnp.exp(s - m)
        p = p / jnp.sum(p, axis=-1, keepdims=True)
        o_ref[:, sl] = _dot(p.astype(BF16), v).astype(o_ref.dtype)


def _memory_attention(q, mem_k, mem_v, *, n_batch, rows_per_batch, row0, tq):
    D = q.shape[1]
    mt = mem_k.shape[0] // n_batch
    nq = rows_per_batch // tq
    blk0 = row0 // tq
    qspec = pl.BlockSpec((tq, D), lambda b, j: (blk0 + b * nq + j, 0))
    mspec = pl.BlockSpec((mt, D), lambda b, j: (b, 0))
    return pl.pallas_call(
        _memattn_kernel,
        grid=(n_batch, nq),
        in_specs=[qspec, mspec, mspec],
        out_specs=pl.BlockSpec((tq, D), lambda b, j: (b * nq + j, 0)),
        out_shape=jax.ShapeDtypeStruct((n_batch * rows_per_batch, D), BF16),
        compiler_params=_cparams(("parallel", "arbitrary")),
    )(q, mem_k, mem_v)


def _ssd_kernel(z_ref, xs_ref, bc_ref, us_ref, dtT_ref, cprev_ref, h0_ref,
                convw_ref, convb_ref, dtb_ref, alog_ref, dtbT_ref, alogT_ref, dskip_ref, ng_ref,
                expP_ref, expT_ref,
                y_ref, cnew_ref, h_ref, extx_s, extbc_s, *, T, n_heads):
    P, R, N = SSD_HEAD_DIM, SSD_HEADS_PER_GROUP, SSD_STATE
    G = n_heads // R
    DI = n_heads * P
    GN = G * N
    c = pl.program_id(1)
    W1 = SSD_CONV_W - 1
    base = 8 - W1

    @pl.when(c == 0)
    def _():
        extx_s[base:8, :] = cprev_ref[0, :, :DI]
        extbc_s[base:8, :] = cprev_ref[0, :, DI:]
        h_ref[0] = h0_ref[0]

    extx_s[8:8 + T, :] = xs_ref[...]
    extbc_s[8:8 + T, :] = bc_ref[...]

    def conv(ext, lo, hi):
        out = convb_ref[:, lo:hi] + convw_ref[0:1, lo:hi] * ext[base:base + T, :]
        for j in range(1, SSD_CONV_W):
            out = out + convw_ref[j:j + 1, lo:hi] * ext[base + j:base + j + T, :]
        return _silu(out)

    xs = conv(extx_s, 0, DI)
    bc = conv(extbc_s, DI, DI + 2 * GN)
    newx = extx_s[8 + T - W1:8 + T, :]
    newbc = extbc_s[8 + T - W1:8 + T, :]
    cnew_ref[0, :, :DI] = newx
    cnew_ref[0, :, DI:] = newbc
    extx_s[base:8, :] = newx
    extbc_s[base:8, :] = newbc

    dt = _softplus(us_ref[:, US_DT:US_DT + n_heads] + dtb_ref[...])
    d_a = dt * (-jnp.exp(alog_ref[...]))
    row = lax.broadcasted_iota(I32, (T, T), 0)
    col = lax.broadcasted_iota(I32, (T, T), 1)
    tri = (row >= col).astype(F32)
    a_cum = _dot_hi(tri, d_a)
    a_exp = _dot_hi(a_cum, expP_ref[...])
    dt_exp = _dot_hi(dt, expP_ref[...])
    a_expt = a_exp if T == P else _dot_hi(a_cum, expT_ref[...])
    d_a_t = _softplus(dtT_ref[0] + dtbT_ref[...]) * (-jnp.exp(alogT_ref[...]))
    r2 = lax.broadcasted_iota(I32, (2 * T, 2 * T), 0)
    c2 = lax.broadcasted_iota(I32, (2 * T, 2 * T), 1)
    tri2 = ((r2 // T == c2 // T) & (r2 <= c2)).astype(F32)
    a_cum_t = _dot_hi(d_a_t, tri2)

    l_idx = lax.broadcasted_iota(I32, (T, 2 * T), 0)
    j_idx = lax.broadcasted_iota(I32, (T, 2 * T), 1)
    causal2 = l_idx >= (j_idx % T)
    rr = lax.broadcasted_iota(I32, (2 * T, 2 * P), 0)
    cc = lax.broadcasted_iota(I32, (2 * T, 2 * P), 1)
    blockdiag = (rr // T) == (cc // P)
    sel_last = (lax.broadcasted_iota(I32, (T, LANES), 0) == T - 1).astype(F32)

    xdt = xs * dt_exp
    for g in range(G):
        gs = slice(g * R * P, (g + 1) * R * P)
        b_g = bc[:, g * N:(g + 1) * N].astype(BF16)
        c_g = bc[:, GN + g * N:GN + (g + 1) * N].astype(BF16)
        cb2 = _dot_nt(c_g, jnp.concatenate([b_g, b_g], axis=0))
        h_g = h_ref[0, gs, :]
        y_off = _dot_nt(c_g, h_g.astype(BF16)) * jnp.exp(a_exp[:, gs])
        pairs = []
        for pr in range(R // 2):
            i = g * (R // 2) + pr
            seg = a_expt[:, i * 2 * T:(i + 1) * 2 * T] - a_cum_t[i:i + 1, :]
            decay = jnp.exp(jnp.where(causal2, seg, -jnp.inf))
            m_pair = (cb2 * decay).astype(BF16)
            x_pair = xdt[:, i * 2 * P:(i + 1) * 2 * P]
            rhs = jnp.where(blockdiag, jnp.concatenate([x_pair, x_pair], axis=0), 0.0).astype(BF16)
            pairs.append(_dot(m_pair, rhs))
        y_g = jnp.concatenate(pairs, axis=1) + y_off + dskip_ref[:, gs] * xs[:, gs]
        a_g = a_exp[:, gs]
        dte = jnp.exp(a_g[T - 1:T, :] - a_g)
        st = _dot_tn((xdt[:, gs] * dte).astype(BF16), b_g)
        dec = jnp.exp(_dot_tn(a_g, sel_last, precision=lax.Precision.HIGHEST))
        h_ref[0, gs, :] = dec * h_g + st
        yg = y_g * _silu(z_ref[:, gs])
        ms = jnp.mean(yg * yg, axis=-1, keepdims=True)
        y_ref[:, gs] = (yg * lax.rsqrt(ms + RMS_EPS) * ng_ref[:, gs]).astype(y_ref.dtype)


def _ssd_mixer(u_main, u_small, dt_t, conv_prev, h0, prm, cols, *, n_batch, seq, row0, T):
    n_heads = prm['dtb'].shape[1]
    DI = n_heads * SSD_HEAD_DIM
    BCW = conv_prev.shape[2] - DI
    nc = seq // T
    blk0 = row0 // T

    def rows(width, colblk):
        return pl.BlockSpec((T, width), lambda b, c: (blk0 + b * nc + c, colblk))

    def full(a):
        return pl.BlockSpec(a.shape, lambda b, c: (0,) * a.ndim)

    params = [prm['conv_w'], prm['conv_b'], prm['dtb'], prm['alog'], prm['dtbT'], prm['alogT'],
              prm['dskip'], prm['ng'], prm['expP'], prm['expT']]
    in_specs = [rows(DI, cols['z'] // DI), rows(DI, cols['xs'] // DI), rows(BCW, cols['bc'] // BCW),
                rows(US_W, 0),
                pl.BlockSpec((1,) + dt_t.shape[1:], lambda b, c: (b * nc + c, 0, 0)),
                pl.BlockSpec((1,) + conv_prev.shape[1:], lambda b, c: (b, 0, 0)),
                pl.BlockSpec((1,) + h0.shape[1:], lambda b, c: (b, 0, 0))] + [full(a) for a in params]
    out_specs = [pl.BlockSpec((T, DI), lambda b, c: (b * nc + c, 0)),
                 pl.BlockSpec((1,) + conv_prev.shape[1:], lambda b, c: (b, 0, 0)),
                 pl.BlockSpec((1,) + h0.shape[1:], lambda b, c: (b, 0, 0))]
    return pl.pallas_call(
        functools.partial(_ssd_kernel, T=T, n_heads=n_heads),
        grid=(n_batch, nc),
        in_specs=in_specs,
        out_specs=out_specs,
        out_shape=[jax.ShapeDtypeStruct((n_batch * seq, DI), BF16),
                   jax.ShapeDtypeStruct(conv_prev.shape, F32),
                   jax.ShapeDtypeStruct(h0.shape, F32)],
        scratch_shapes=[pltpu.VMEM((T + 8, DI), F32), pltpu.VMEM((T + 8, BCW), F32)],
        compiler_params=_cparams(("parallel", "arbitrary")),
    )(u_main, u_main, u_main, u_small, dt_t, conv_prev, h0, *params)


def _sortable_key(x):
    b = lax.bitcast_convert_type(x, I32)
    return b ^ ((b >> 31) & 0x7FFFFFFF)


def _indexer_scores(qi_heads, wi, ki_blk):
    acc = None
    for h in range(IDX_HEADS):
        l = _dot_nt(qi_heads[h], ki_blk) * (IDX_DIM ** -0.5)
        t = jnp.maximum(l, 0.0) * wi[:, h:h + 1]
        acc = t if acc is None else acc + t
    return acc


def _kth_largest(count_ge, n_rows, k):
    c0 = count_ge(jnp.zeros((n_rows, 1), I32))
    t0 = jnp.where(c0 >= k, 0, INT_MIN).astype(I32)

    def body(i, t):
        cand = t | (jnp.int32(1) << (30 - i))
        return jnp.where(count_ge(cand) >= k, cand, t)

    return lax.fori_loop(0, 31, body, t0)


def _limits(pos0, n_rows):
    pos = pos0 + lax.broadcasted_iota(I32, (n_rows, 1), 0)
    return (pos // CHUNK + 1) * CHUNK


def _dsa_prompt_kernel(q_ref, qi_ref, us_ref, ki_ref, k_ref, v_ref, o_ref, key_s, bias_s, *, n_sel, kv_heads):
    QB, KB = DSA_QB, DSA_KB
    j = pl.program_id(1)
    nkb = ((j + 1) * QB + KB - 1) // KB
    wi = us_ref[:, US_WI:US_WI + IDX_HEADS] * (IDX_HEADS ** -0.5)
    qi_heads = [qi_ref[:, h * IDX_DIM:(h + 1) * IDX_DIM].astype(BF16) for h in range(IDX_HEADS)]
    lim = _limits(j * QB, QB)

    def score_blk(kb, carry):
        off = pl.multiple_of(kb * KB, KB)
        sc = _indexer_scores(qi_heads, wi, ki_ref[pl.ds(off, KB), :])
        s_idx = off + lax.broadcasted_iota(I32, (QB, KB), 1)
        key_s[kb] = jnp.where(s_idx < lim, _sortable_key(sc), INT_MIN)
        return carry

    lax.fori_loop(0, nkb, score_blk, 0)

    def count_ge(cand):
        def body(kb, acc):
            m = (key_s[kb] >= cand).astype(I32)
            for c in range(KB // LANES):
                acc = acc + m[:, c * LANES:(c + 1) * LANES]
            return acc
        acc = lax.fori_loop(0, nkb, body, jnp.zeros((QB, LANES), I32))
        return jnp.sum(acc, axis=1, keepdims=True)

    thr = _kth_largest(count_ge, QB, n_sel)

    def bias_blk(kb, carry):
        k = key_s[kb]
        bias_s[kb] = jnp.where((k >= thr) & (k > INT_MIN), 0.0, NEG)
        return carry

    lax.fori_loop(0, nkb, bias_blk, 0)

    scale = HEAD_DIM ** -0.5
    for g in range(kv_heads):
        hs = slice(g * HEAD_DIM, (g + 1) * HEAD_DIM)
        qs = jnp.concatenate(
            [q_ref[:, (g * Q_PER_KV + i) * HEAD_DIM:(g * Q_PER_KV + i + 1) * HEAD_DIM] for i in range(Q_PER_KV)],
            axis=0).astype(BF16)

        def body(kb, carry):
            m, l, acc = carry
            off = pl.multiple_of(kb * KB, KB)
            s = _dot_nt(qs, k_ref[pl.ds(off, KB), hs]) * scale
            b = bias_s[kb]
            s = s + jnp.concatenate([b] * Q_PER_KV, axis=0)
            m_new = jnp.maximum(m, jnp.max(s, axis=-1, keepdims=True))
            alpha = jnp.exp(m - m_new)
            p = jnp.exp(s - m_new)
            l = alpha * l + jnp.sum(p, axis=-1, keepdims=True)
            acc = alpha * acc + _dot(p.astype(BF16), v_ref[pl.ds(off, KB), hs])
            return m_new, l, acc

        m0 = jnp.full((Q_PER_KV * QB, 1), NEG, F32)
        l0 = jnp.zeros((Q_PER_KV * QB, 1), F32)
        a0 = jnp.zeros((Q_PER_KV * QB, HEAD_DIM), F32)
        _, l, acc = lax.fori_loop(0, nkb, body, (m0, l0, a0))
        out = acc / l
        for i in range(Q_PER_KV):
            o_ref[:, (g * Q_PER_KV + i) * HEAD_DIM:(g * Q_PER_KV + i + 1) * HEAD_DIM] = (
                out[i * QB:(i + 1) * QB].astype(o_ref.dtype))


def _dsa_prompt(u_main, u_small, ki_b, k_b, v_b, cols, *, n_batch, seq, kv_heads):
    QB, KB = DSA_QB, DSA_KB
    nq = seq // QB
    nkb_max = seq // KB
    DQ = kv_heads * Q_PER_KV * HEAD_DIM
    DK = kv_heads * HEAD_DIM
    DQI = IDX_HEADS * IDX_DIM
    n_sel = min(TOPK_MAX, seq // 4)
    return pl.pallas_call(
        functools.partial(_dsa_prompt_kernel, n_sel=n_sel, kv_heads=kv_heads),
        grid=(n_batch, nq),
        in_specs=[pl.BlockSpec((QB, DQ), lambda b, j: (b * nq + j, cols['q'] // DQ)),
                  pl.BlockSpec((QB, DQI), lambda b, j: (b * nq + j, cols['qi'] // DQI)),
                  pl.BlockSpec((QB, US_W), lambda b, j: (b * nq + j, 0)),
                  pl.BlockSpec((seq, IDX_DIM), lambda b, j: (b, 0)),
                  pl.BlockSpec((seq, DK), lambda b, j: (b, 0)),
                  pl.BlockSpec((seq, DK), lambda b, j: (b, 0))],
        out_specs=pl.BlockSpec((QB, DQ), lambda b, j: (b * nq + j, 0)),
        out_shape=jax.ShapeDtypeStruct((n_batch * seq, DQ), BF16),
        scratch_shapes=[pltpu.VMEM((nkb_max, QB, KB), I32), pltpu.VMEM((nkb_max, QB, KB), F32)],
        compiler_params=_cparams(("parallel", "arbitrary")),
    )(u_main, u_main, u_small, ki_b, k_b, v_b)


def _dsa_sample_kernel(q_ref, qi_ref, us_ref, kn_ref, vn_ref, ck_ref, cv_ref, cki_ref, o_ref, *,
                       n_sel, kv_heads, past):
    Q = q_ref.shape[0]
    wi = us_ref[:, US_WI:US_WI + IDX_HEADS] * (IDX_HEADS ** -0.5)
    qi_heads = [qi_ref[:, h * IDX_DIM:(h + 1) * IDX_DIM].astype(BF16) for h in range(IDX_HEADS)]
    lim = _limits(past, Q)
    ki_new = us_ref[:, US_KI:US_KI + IDX_DIM].astype(BF16)
    sc_p = _indexer_scores(qi_heads, wi, cki_ref[0].astype(BF16))
    sc_n = _indexer_scores(qi_heads, wi, ki_new)
    idx_p = lax.broadcasted_iota(I32, (Q, past), 1)
    idx_n = past + lax.broadcasted_iota(I32, (Q, Q), 1)
    key_p = jnp.where(idx_p < lim, _sortable_key(sc_p), INT_MIN)
    key_n = jnp.where(idx_n < lim, _sortable_key(sc_n), INT_MIN)

    def count_ge(cand):
        return (jnp.sum((key_p >= cand).astype(I32), axis=1, keepdims=True)
                + jnp.sum((key_n >= cand).astype(I32), axis=1, keepdims=True))

    thr = _kth_largest(count_ge, Q, n_sel)
    bias_p = jnp.where((key_p >= thr) & (key_p > INT_MIN), 0.0, NEG)
    bias_n = jnp.where((key_n >= thr) & (key_n > INT_MIN), 0.0, NEG)
    bias_p = jnp.concatenate([bias_p] * Q_PER_KV, axis=0)
    bias_n = jnp.concatenate([bias_n] * Q_PER_KV, axis=0)

    scale = HEAD_DIM ** -0.5
    for g in range(kv_heads):
        hs = slice(g * HEAD_DIM, (g + 1) * HEAD_DIM)
        qs = jnp.concatenate(
            [q_ref[:, (g * Q_PER_KV + i) * HEAD_DIM:(g * Q_PER_KV + i + 1) * HEAD_DIM] for i in range(Q_PER_KV)],
            axis=0).astype(BF16)
        s_p = _dot_nt(qs, ck_ref[0, :, hs].astype(BF16)) * scale + bias_p
        s_n = _dot_nt(qs, kn_ref[:, hs].astype(BF16)) * scale + bias_n
        m = jnp.maximum(jnp.max(s_p, axis=-1, keepdims=True), jnp.max(s_n, axis=-1, keepdims=True))
        p_p = jnp.exp(s_p - m)
        p_n = jnp.exp(s_n - m)
        l = jnp.sum(p_p, axis=-1, keepdims=True) + jnp.sum(p_n, axis=-1, keepdims=True)
        acc = (_dot(p_p.astype(BF16), cv_ref[0, :, hs].astype(BF16))
               + _dot(p_n.astype(BF16), vn_ref[:, hs].astype(BF16)))
        out = acc / l
        for i in range(Q_PER_KV):
            o_ref[:, (g * Q_PER_KV + i) * HEAD_DIM:(g * Q_PER_KV + i + 1) * HEAD_DIM] = (
                out[i * Q:(i + 1) * Q].astype(o_ref.dtype))


def _dsa_sample(u_main, u_small, cache_k, cache_v, cache_ki, cols, *, n_batch, seq, row0, kv_heads):
    past = cache_k.shape[1]
    DQ = kv_heads * Q_PER_KV * HEAD_DIM
    DK = kv_heads * HEAD_DIM
    DQI = IDX_HEADS * IDX_DIM
    n_sel = min(TOPK_MAX, (past + seq) // 4)
    blk0 = row0 // seq
    return pl.pallas_call(
        functools.partial(_dsa_sample_kernel, n_sel=n_sel, kv_heads=kv_heads, past=past),
        grid=(n_batch,),
        in_specs=[pl.BlockSpec((seq, DQ), lambda b: (blk0 + b, cols['q'] // DQ)),
                  pl.BlockSpec((seq, DQI), lambda b: (blk0 + b, cols['qi'] // DQI)),
                  pl.BlockSpec((seq, US_W), lambda b: (blk0 + b, 0)),
                  pl.BlockSpec((seq, DK), lambda b: (blk0 + b, cols['k'] // DK)),
                  pl.BlockSpec((seq, DK), lambda b: (blk0 + b, cols['v'] // DK)),
                  pl.BlockSpec((1, past, DK), lambda b: (b, 0, 0)),
                  pl.BlockSpec((1, past, DK), lambda b: (b, 0, 0)),
                  pl.BlockSpec((1, past, IDX_DIM), lambda b: (b, 0, 0))],
        out_specs=pl.BlockSpec((seq, DQ), lambda b: (b, 0)),
        out_shape=jax.ShapeDtypeStruct((n_batch * seq, DQ), BF16),
        compiler_params=_cparams(("parallel",)),
    )(u_main, u_main, u_small, u_main, u_main, cache_k, cache_v, cache_ki)


def _pair_layout(v, T):
    return jnp.repeat(v.reshape(-1, 2), T, axis=1)


def _dt_transposed(dt_raw, T):
    n, H = dt_raw.shape
    return dt_raw.reshape(n // T, T, H // 2, 2).transpose(0, 2, 3, 1).reshape(n // T, H // 2, 2 * T)


def _ssd_params(conv_w, conv_b, dt_bias, a_log, d_skip, norm_g, T):
    H = dt_bias.shape[0]
    P = SSD_HEAD_DIM
    eye = jnp.eye(H, dtype=F32)
    return dict(conv_w=conv_w, conv_b=conv_b.reshape(1, -1),
                dtb=dt_bias.reshape(1, H), alog=a_log.reshape(1, H),
                dtbT=_pair_layout(dt_bias, T), alogT=_pair_layout(a_log, T),
                dskip=jnp.repeat(d_skip, P).reshape(1, H * P), ng=norm_g.reshape(1, -1),
                expP=jnp.repeat(eye, P, axis=1), expT=jnp.repeat(eye, T, axis=1))


def kernel(x_prompt, x_sample, mem_prompt, cache_k, cache_v, cache_idx_k, cache_mem_k, cache_mem_v,
           state_ssm, state_conv, ln1_g, ln1_b, ffn1_w1, ffn1_w3, ffn1_w2, w_in, conv_w, conv_b,
           dt_bias, a_log, d_skip, ssd_norm_g, w_ssd_br, w_att_br, w_out, ln2_g, ln2_b,
           w_mq, w_mk, w_mv, w_mo, ln3_g, ln3_b, ffn2_w1, ffn2_w3, ffn2_w2, ln4_g, ln4_b):
    assert x_prompt.ndim == 3 and ln1_g.shape[0] == DEPTH == 1
    BP, LP, D = x_prompt.shape
    BS, LS, _ = x_sample.shape
    MP, MS = BP * LP, BS * LS
    M = MP + MS
    H = dt_bias.shape[1]
    DI = H * SSD_HEAD_DIM
    BCW = conv_w.shape[2] - DI
    KVH = cache_k.shape[3]
    DQ = KVH * Q_PER_KV * HEAD_DIM
    DK = KVH * HEAD_DIM
    DQI = IDX_HEADS * IDX_DIM
    MT = mem_prompt.shape[1]
    PAST = cache_k.shape[2]

    sizes = (DI, DI + BCW, H, DQ, DK, DK, DQI, IDX_HEADS, IDX_DIM, D, D)
    names = ('z', 'xbc', 'dt', 'q', 'k', 'v', 'qi', 'wi', 'ki', 'g_ssd', 'g_att')
    ref_off, o = {}, 0
    for n, s in zip(names, sizes):
        ref_off[n] = (o, s)
        o += s

    def wcols(w, n, lo=0, hi=None):
        s, width = ref_off[n]
        hi = width if hi is None else hi
        return w[:, s + lo:s + hi]

    l = 0
    w0 = w_in[l]
    main_parts = [('z', wcols(w0, 'z')), ('xs', wcols(w0, 'xbc', 0, DI)), ('q', wcols(w0, 'q')),
                  ('g_ssd', wcols(w0, 'g_ssd')), ('g_att', wcols(w0, 'g_att')),
                  ('bc', wcols(w0, 'xbc', DI, DI + BCW)), ('k', wcols(w0, 'k')), ('v', wcols(w0, 'v')),
                  ('qi', wcols(w0, 'qi'))]
    cols, o = {}, 0
    for n, part in main_parts:
        assert o % part.shape[1] == 0
        cols[n] = o
        o += part.shape[1]
    w_main = jnp.concatenate([p.astype(BF16) for _, p in main_parts], axis=1)
    n_small = H + IDX_HEADS + IDX_DIM
    assert H == SSD_HEADS and n_small <= US_W
    w_small = jnp.concatenate([wcols(w0, 'dt'), wcols(w0, 'wi'), wcols(w0, 'ki'),
                               jnp.zeros((D, US_W - n_small), F32)], axis=1).astype(BF16)

    bf = lambda w: w[l].astype(BF16)
    tm_big = M // 8
    tm_half = M // 16

    x0 = jnp.concatenate([x_prompt.reshape(MP, D), x_sample.reshape(MS, D)], axis=0)

    def ffn_block(x, xb, w1, w3, w2, g, b):
        h = _ffn_up(xb, bf(w1), bf(w3), tm=tm_big, tn=256)
        d = _matmul(h, bf(w2), tm=tm_half, tn=256, out_dtype=F32)
        return _res_layer_norm(x, d, g[l], b[l], scale=0.5)

    x1, x1b = ffn_block(x0, x0.astype(BF16), ffn1_w1, ffn1_w3, ffn1_w2, ln1_g, ln1_b)

    u_main = _matmul(x1b, w_main, tm=tm_big, tn=512, out_dtype=F32)
    u_small = _matmul(x1b, w_small, tm=tm_big, tn=US_W, out_dtype=F32)

    dt_raw = u_small[:, US_DT:US_DT + H]
    zeros_conv = jnp.zeros((BP,) + state_conv.shape[2:], F32)
    zeros_h = jnp.zeros((BP, DI, SSD_STATE), F32)
    prm_p = _ssd_params(conv_w[l], conv_b[l], dt_bias[l], a_log[l], d_skip[l], ssd_norm_g[l], CHUNK)
    y_ssd_p, conv_p, h_p = _ssd_mixer(u_main, u_small, _dt_transposed(dt_raw[:MP], CHUNK), zeros_conv, zeros_h,
                                      prm_p, cols, n_batch=BP, seq=LP, row0=0, T=CHUNK)
    TS = min(CHUNK, LS)
    prm_s = _ssd_params(conv_w[l], conv_b[l], dt_bias[l], a_log[l], d_skip[l], ssd_norm_g[l], TS)
    y_ssd_s, conv_s, h_s = _ssd_mixer(u_main, u_small, _dt_transposed(dt_raw[MP:], TS), state_conv[l],
                                      state_ssm[l].reshape(BS, DI, SSD_STATE),
                                      prm_s, cols, n_batch=BS, seq=LS, row0=MP, T=TS)
    y_ssd = jnp.concatenate([y_ssd_p, y_ssd_s], axis=0)

    k_all = u_main[:, cols['k']:cols['k'] + DK]
    v_all = u_main[:, cols['v']:cols['v'] + DK]
    ki_all = u_small[:, US_KI:US_KI + IDX_DIM]
    y_att_p = _dsa_prompt(u_main, u_small, ki_all[:MP].astype(BF16), k_all[:MP].astype(BF16),
                          v_all[:MP].astype(BF16), cols, n_batch=BP, seq=LP, kv_heads=KVH)
    y_att_s = _dsa_sample(u_main, u_small, cache_k[l].reshape(BS, PAST, DK), cache_v[l].reshape(BS, PAST, DK),
                          cache_idx_k[l], cols, n_batch=BS, seq=LS, row0=MP, kv_heads=KVH)
    y_att = jnp.concatenate([y_att_p, y_att_s], axis=0)

    merged = _gated_merge(y_ssd, y_att, bf(w_ssd_br), bf(w_att_br), u_main, cols['g_ssd'], cols['g_att'],
                          tm=tm_half, tn=512)
    d2 = _matmul(merged, bf(w_out), tm=tm_big, tn=512, out_dtype=F32)
    x2, x2b = _res_layer_norm(x1, d2, ln2_g[l], ln2_b[l], scale=1.0)

    memb = mem_prompt.reshape(BP * MT, D).astype(BF16)
    mk_p = _matmul(memb, bf(w_mk), tm=BP * MT, tn=512, out_dtype=F32)
    mv_p = _matmul(memb, bf(w_mv), tm=BP * MT, tn=512, out_dtype=F32)
    qm = _matmul(x2b, bf(w_mq), tm=tm_big, tn=512, out_dtype=BF16)
    o_p = _memory_attention(qm, mk_p, mv_p, n_batch=BP, rows_per_batch=LP, row0=0, tq=min(512, LP))
    o_s = _memory_attention(qm, cache_mem_k[l].reshape(BS * MT, D), cache_mem_v[l].reshape(BS * MT, D),
                            n_batch=BS, rows_per_batch=LS, row0=MP, tq=LS)
    d3 = _matmul(jnp.concatenate([o_p, o_s], axis=0), bf(w_mo), tm=tm_big, tn=512, out_dtype=F32)
    x3, x3b = _res_layer_norm(x2, d3, ln3_g[l], ln3_b[l], scale=1.0)

    x4, _ = ffn_block(x3, x3b, ffn2_w1, ffn2_w3, ffn2_w2, ln4_g, ln4_b)

    mh = D // MEM_HEADS
    return (x4[:MP].reshape(BP, LP, D), x4[MP:].reshape(BS, LS, D),
            h_p.reshape(1, BP, H, SSD_HEAD_DIM, SSD_STATE), conv_p[None],
            k_all[:MP].reshape(1, BP, LP, KVH, HEAD_DIM), v_all[:MP].reshape(1, BP, LP, KVH, HEAD_DIM),
            ki_all[:MP].reshape(1, BP, LP, IDX_DIM),
            mk_p.reshape(1, BP, MT, MEM_HEADS, mh), mv_p.reshape(1, BP, MT, MEM_HEADS, mh),
            h_s.reshape(1, BS, H, SSD_HEAD_DIM, SSD_STATE), conv_s[None],
            k_all[MP:].reshape(1, BS, LS, KVH, HEAD_DIM), v_all[MP:].reshape(1, BS, LS, KVH, HEAD_DIM),
            ki_all[MP:].reshape(1, BS, LS, IDX_DIM))
```

```python
import functools
import math

import jax
import jax.numpy as jnp
from jax import lax
from jax.experimental import pallas as pl
from jax.experimental.pallas import tpu as pltpu

F32 = jnp.float32
BF16 = jnp.bfloat16
I32 = jnp.int32

DEPTH = 1
CHUNK = 64
SSD_HEADS = 64
SSD_HEAD_DIM = 64
SSD_HEADS_PER_GROUP = 8
SSD_STATE = 128
SSD_CONV_W = 4
HEAD_DIM = 128
Q_PER_KV = 4
IDX_HEADS = 16
IDX_DIM = 64
TOPK_MAX = 256
MEM_HEADS = 4
ALPHA = (2.0 * DEPTH) ** 0.25
LN_EPS = 1e-5
RMS_EPS = 1e-5

LANES = 128
VMEM_LIMIT = 56 * 1024 * 1024

NEG = -1e30
INT_MIN = -2147483648
LOG2E = math.log2(math.e)

DSA_QB = 128
DSA_KB = 512
DSA_GROUPS_PER_LOOP = 2
LN_ROWS = 256

US_DT = 0
US_WI = US_DT + SSD_HEADS
US_KI = US_WI + IDX_HEADS
US_W = 256


def _cparams(sem):
    return pltpu.CompilerParams(dimension_semantics=sem, vmem_limit_bytes=VMEM_LIMIT)


def _dot(a, b):
    return jnp.dot(a, b, preferred_element_type=F32)


def _dot_nt(a, b):
    return lax.dot_general(a, b, (((1,), (1,)), ((), ())), preferred_element_type=F32)


def _dot_tn(a, b, precision=None):
    return lax.dot_general(a, b, (((0,), (0,)), ((), ())), preferred_element_type=F32,
                           precision=precision)


def _dot_hi(a, b):
    return jnp.dot(a, b, preferred_element_type=F32, precision=lax.Precision.HIGHEST)


def _sigmoid(x):
    return 1.0 / (1.0 + jnp.exp(-x))


def _silu(x):
    return x * _sigmoid(x)


def _softplus(x):
    return jnp.maximum(x, 0.0) + jnp.log1p(jnp.exp(-jnp.abs(x)))


def _stacked_call(kernel, *, grid, in_specs, out_specs, out_shape, stacked, n_stacked, sem, scratch_shapes=()):
    if stacked is None:
        return pl.pallas_call(kernel, grid=grid, in_specs=in_specs, out_specs=out_specs, out_shape=out_shape,
                              scratch_shapes=scratch_shapes, compiler_params=_cparams(sem))
    n_in = len(in_specs)

    def body(*refs):
        kernel(*refs[:n_in], *refs[n_in + n_stacked:])

    call = pl.pallas_call(
        body, grid=grid,
        in_specs=list(in_specs) + [pl.BlockSpec(memory_space=pl.ANY)] * n_stacked,
        out_specs=out_specs, out_shape=out_shape, scratch_shapes=scratch_shapes,
        input_output_aliases={n_in + i: i for i in range(n_stacked)},
        compiler_params=_cparams(sem))
    return lambda *args: call(*args, *stacked)


def _mm_kernel(x_ref, w_ref, o_ref):
    o_ref[...] = _dot(x_ref[...], w_ref[...].astype(BF16)).astype(o_ref.dtype)


def _matmul(x, w, *, tm, tn, out_dtype, col0=0, n_cols=None, single_buffer_x=False):
    M, K = x.shape
    n_cols = w.shape[1] - col0 if n_cols is None else n_cols
    assert col0 % tn == 0 and n_cols % tn == 0 and M % tm == 0
    cb = col0 // tn
    xmode = dict(pipeline_mode=pl.Buffered(1)) if single_buffer_x else {}
    return pl.pallas_call(
        _mm_kernel,
        grid=(M // tm, n_cols // tn),
        in_specs=[pl.BlockSpec((tm, K), lambda i, j: (i, 0), **xmode),
                  pl.BlockSpec((K, tn), lambda i, j: (0, cb + j))],
        out_specs=pl.BlockSpec((tm, tn), lambda i, j: (i, j)),
        out_shape=jax.ShapeDtypeStruct((M, n_cols), out_dtype),
        compiler_params=_cparams(("parallel", "arbitrary")),
    )(x, w)


def _ffn_up_kernel(x_ref, w1_ref, w3_ref, o_ref):
    x = x_ref[...]
    a = _dot(x, w1_ref[...].astype(BF16))
    b = _dot(x, w3_ref[...].astype(BF16))
    o_ref[...] = (_silu(a) * b).astype(o_ref.dtype)


def _ffn_up(x, w1, w3, *, tm, tn):
    M, K = x.shape
    N = w1.shape[1]
    return pl.pallas_call(
        _ffn_up_kernel,
        grid=(M // tm, N // tn),
        in_specs=[pl.BlockSpec((tm, K), lambda i, j: (i, 0)),
                  pl.BlockSpec((K, tn), lambda i, j: (0, j)),
                  pl.BlockSpec((K, tn), lambda i, j: (0, j))],
        out_specs=pl.BlockSpec((tm, tn), lambda i, j: (i, j)),
        out_shape=jax.ShapeDtypeStruct((M, N), BF16),
        compiler_params=_cparams(("parallel", "arbitrary")),
    )(x, w1, w3)


def _gate_kernel(ys_ref, ya_ref, ws_ref, wa_ref, gs_ref, ga_ref, o_ref):
    s = _dot(ys_ref[...], ws_ref[...].astype(BF16))
    a = _dot(ya_ref[...], wa_ref[...].astype(BF16))
    o_ref[...] = (_sigmoid(gs_ref[...]) * s + _sigmoid(ga_ref[...]) * a).astype(o_ref.dtype)


def _gated_merge(y_ssd, y_att, w_ssd, w_att, u_gate, gs_col, ga_col, *, tm, tn):
    M, K = y_ssd.shape
    N = w_ssd.shape[1]
    gs_blk, ga_blk = gs_col // tn, ga_col // tn
    once = dict(pipeline_mode=pl.Buffered(1))
    return pl.pallas_call(
        _gate_kernel,
        grid=(M // tm, N // tn),
        in_specs=[pl.BlockSpec((tm, K), lambda i, j: (i, 0), **once),
                  pl.BlockSpec((tm, K), lambda i, j: (i, 0), **once),
                  pl.BlockSpec((K, tn), lambda i, j: (0, j)),
                  pl.BlockSpec((K, tn), lambda i, j: (0, j)),
                  pl.BlockSpec((tm, tn), lambda i, j: (i, gs_blk + j)),
                  pl.BlockSpec((tm, tn), lambda i, j: (i, ga_blk + j))],
        out_specs=pl.BlockSpec((tm, tn), lambda i, j: (i, j)),
        out_shape=jax.ShapeDtypeStruct((M, N), BF16),
        compiler_params=_cparams(("parallel", "arbitrary")),
    )(y_ssd, y_att, w_ssd, w_att, u_gate, u_gate)


def _ln_kernel(x_ref, d_ref, g_ref, b_ref, *o_refs, scale):
    y = ALPHA * x_ref[...] + scale * d_ref[...]
    mu = jnp.mean(y, axis=-1, keepdims=True)
    yc = y - mu
    var = jnp.mean(yc * yc, axis=-1, keepdims=True)
    o = yc * lax.rsqrt(var + LN_EPS) * g_ref[...] + b_ref[...]
    for o_ref in o_refs:
        o_ref[...] = o.astype(o_ref.dtype)


def _res_layer_norm(x, delta, g, b, *, scale, n_rows, x_row0=0, d_row0=0, out_rows=None, out_row0=0,
                    stacked=None, emit_bf16=True):
    D = x.shape[1]
    out_rows = n_rows if out_rows is None else out_rows
    if stacked is not None:
        out_rows = stacked[0].shape[0]
    tr = max(r for r in range(8, LN_ROWS + 1, 8)
             if all(v % r == 0 for v in (n_rows, x_row0, d_row0, out_row0)))
    xb, db, ob = x_row0 // tr, d_row0 // tr, out_row0 // tr
    vec = pl.BlockSpec((1, D), lambda i: (0, 0))
    dts = (F32, BF16) if emit_bf16 else (F32,)
    call = _stacked_call(
        functools.partial(_ln_kernel, scale=scale),
        grid=(n_rows // tr,),
        in_specs=[pl.BlockSpec((tr, D), lambda i: (xb + i, 0)), pl.BlockSpec((tr, D), lambda i: (db + i, 0)), vec, vec],
        out_specs=[pl.BlockSpec((tr, D), lambda i: (ob + i, 0)) for _ in dts],
        out_shape=[jax.ShapeDtypeStruct((out_rows, D), dt) for dt in dts],
        stacked=stacked, n_stacked=len(dts), sem=("parallel",))
    return call(x, delta, g.reshape(1, D), b.reshape(1, D))


def _memattn_kernel(q_ref, k_ref, v_ref, o_ref):
    dh = q_ref.shape[1] // MEM_HEADS
    scale = dh ** -0.5
    for h in range(MEM_HEADS):
        sl = slice(h * dh, (h + 1) * dh)
        q = q_ref[:, sl]
        k = k_ref[:, sl].astype(BF16)
        v = v_ref[:, sl].astype(BF16)
        s = _dot_nt(q, k) * scale
        m = jnp.max(s, axis=-1, keepdims=True)
        p = jnp.exp(s - m)
        p = p / jnp.sum(p, axis=-1, keepdims=True)
        o_ref[:, sl] = _dot(p.astype(BF16), v).astype(o_ref.dtype)


def _memory_attention(q, mem_k, mem_v, *, n_batch, rows_per_batch, row0, tq, stacked=None):
    M, D = q.shape
    mt = mem_k.shape[0] // n_batch
    nq = rows_per_batch // tq
    blk0 = row0 // tq
    qspec = pl.BlockSpec((tq, D), lambda b, j: (blk0 + b * nq + j, 0))
    mspec = pl.BlockSpec((mt, D), lambda b, j: (b, 0))
    call = _stacked_call(
        _memattn_kernel, grid=(n_batch, nq), in_specs=[qspec, mspec, mspec], out_specs=[qspec],
        out_shape=[jax.ShapeDtypeStruct((M, D), BF16)], stacked=stacked, n_stacked=1,
        sem=("parallel", "arbitrary"))
    return call(q, mem_k, mem_v)[0]


def _ssd_kernel(z_ref, xs_ref, bc_ref, us_ref, dtT_ref, cprev_ref, h0_ref,
                convw_ref, convb_ref, dtb_ref, alog_ref, dtbT_ref, alogT_ref, dskip_ref, ng_ref,
                expP_ref, expT_ref,
                y_ref, cnew_ref, h_ref, extx_s, extbc_s, *, T, n_heads):
    P, R, N = SSD_HEAD_DIM, SSD_HEADS_PER_GROUP, SSD_STATE
    G = n_heads // R
    DI = n_heads * P
    GN = G * N
    c = pl.program_id(1)
    W1 = SSD_CONV_W - 1
    base = 8 - W1

    @pl.when(c == 0)
    def _():
        extx_s[base:8, :] = cprev_ref[0, :, :DI]
        extbc_s[base:8, :] = cprev_ref[0, :, DI:]
        h_ref[0] = h0_ref[0]

    extx_s[8:8 + T, :] = xs_ref[...]
    extbc_s[8:8 + T, :] = bc_ref[...]

    def conv(ext, lo, hi):
        out = convb_ref[:, lo:hi] + convw_ref[0:1, lo:hi] * ext[base:base + T, :]
        for j in range(1, SSD_CONV_W):
            out = out + convw_ref[j:j + 1, lo:hi] * ext[base + j:base + j + T, :]
        return _silu(out)

    xs = conv(extx_s, 0, DI)
    bc = conv(extbc_s, DI, DI + 2 * GN)
    newx = extx_s[8 + T - W1:8 + T, :]
    newbc = extbc_s[8 + T - W1:8 + T, :]
    cnew_ref[0, :, :DI] = newx
    cnew_ref[0, :, DI:] = newbc
    extx_s[base:8, :] = newx
    extbc_s[base:8, :] = newbc

    dt = _softplus(us_ref[:, US_DT:US_DT + n_heads] + dtb_ref[...])
    d_a = dt * (-jnp.exp(alog_ref[...]))
    row = lax.broadcasted_iota(I32, (T, T), 0)
    col = lax.broadcasted_iota(I32, (T, T), 1)
    tri = (row >= col).astype(F32)
    a_cum = _dot_hi(tri, d_a)
    a_exp = _dot_hi(a_cum, expP_ref[...])
    dt_exp = _dot_hi(dt, expP_ref[...])
    a_expt = a_exp if T == P else _dot_hi(a_cum, expT_ref[...])
    d_a_t = _softplus(dtT_ref[0] + dtbT_ref[...]) * (-jnp.exp(alogT_ref[...]))
    r2 = lax.broadcasted_iota(I32, (2 * T, 2 * T), 0)
    c2 = lax.broadcasted_iota(I32, (2 * T, 2 * T), 1)
    tri2 = ((r2 // T == c2 // T) & (r2 <= c2)).astype(F32)
    a_cum_t = _dot_hi(d_a_t, tri2)

    l_idx = lax.broadcasted_iota(I32, (T, 2 * T), 0)
    j_idx = lax.broadcasted_iota(I32, (T, 2 * T), 1)
    causal2 = l_idx >= (j_idx % T)
    rr = lax.broadcasted_iota(I32, (2 * T, 2 * P), 0)
    cc = lax.broadcasted_iota(I32, (2 * T, 2 * P), 1)
    blockdiag = (rr // T) == (cc // P)
    sel_last = (lax.broadcasted_iota(I32, (T, LANES), 0) == T - 1).astype(F32)

    xdt = xs * dt_exp
    for g in range(G):
        gs = slice(g * R * P, (g + 1) * R * P)
        b_g = bc[:, g * N:(g + 1) * N].astype(BF16)
        c_g = bc[:, GN + g * N:GN + (g + 1) * N].astype(BF16)
        cb2 = _dot_nt(c_g, jnp.concatenate([b_g, b_g], axis=0))
        h_g = h_ref[0, gs, :]
        y_off = _dot_nt(c_g, h_g.astype(BF16)) * jnp.exp(a_exp[:, gs])
        pairs = []
        for pr in range(R // 2):
            i = g * (R // 2) + pr
            seg = a_expt[:, i * 2 * T:(i + 1) * 2 * T] - a_cum_t[i:i + 1, :]
            decay = jnp.exp(jnp.where(causal2, seg, -jnp.inf))
            m_pair = (cb2 * decay).astype(BF16)
            x_pair = xdt[:, i * 2 * P:(i + 1) * 2 * P]
            rhs = jnp.where(blockdiag, jnp.concatenate([x_pair, x_pair], axis=0), 0.0).astype(BF16)
            pairs.append(_dot(m_pair, rhs))
        y_g = jnp.concatenate(pairs, axis=1) + y_off + dskip_ref[:, gs] * xs[:, gs]
        a_g = a_exp[:, gs]
        dte = jnp.exp(a_g[T - 1:T, :] - a_g)
        st = _dot_tn((xdt[:, gs] * dte).astype(BF16), b_g)
        dec = jnp.exp(_dot_tn(a_g, sel_last, precision=lax.Precision.HIGHEST))
        h_ref[0, gs, :] = dec * h_g + st
        yg = y_g * _silu(z_ref[:, gs])
        ms = jnp.mean(yg * yg, axis=-1, keepdims=True)
        y_ref[:, gs] = (yg * lax.rsqrt(ms + RMS_EPS) * ng_ref[:, gs]).astype(y_ref.dtype)


def _ssd_mixer(u_ssd, u_small, dt_t, conv_prev, h0, prm, cols, *, n_batch, seq, row0, T, stacked=None):
    M = u_ssd.shape[0]
    n_heads = prm['dtb'].shape[1]
    DI = n_heads * SSD_HEAD_DIM
    BCW = conv_prev.shape[2] - DI
    nc = seq // T
    blk0 = row0 // T

    def rows(width, colblk):
        return pl.BlockSpec((T, width), lambda b, c: (blk0 + b * nc + c, colblk))

    def full(a):
        return pl.BlockSpec(a.shape, lambda b, c: (0,) * a.ndim)

    params = [prm['conv_w'], prm['conv_b'], prm['dtb'], prm['alog'], prm['dtbT'], prm['alogT'],
              prm['dskip'], prm['ng'], prm['expP'], prm['expT']]
    in_specs = [rows(DI, cols['z'] // DI), rows(DI, cols['xs'] // DI), rows(BCW, cols['bc'] // BCW),
                rows(US_W, 0),
                pl.BlockSpec((1,) + dt_t.shape[1:], lambda b, c: (b * nc + c, 0, 0)),
                pl.BlockSpec((1,) + conv_prev.shape[1:], lambda b, c: (b, 0, 0)),
                pl.BlockSpec((1,) + h0.shape[1:], lambda b, c: (b, 0, 0))] + [full(a) for a in params]
    out_specs = [rows(DI, 0),
                 pl.BlockSpec((1,) + conv_prev.shape[1:], lambda b, c: (b, 0, 0)),
                 pl.BlockSpec((1,) + h0.shape[1:], lambda b, c: (b, 0, 0))]
    call = _stacked_call(
        functools.partial(_ssd_kernel, T=T, n_heads=n_heads),
        grid=(n_batch, nc), in_specs=in_specs, out_specs=out_specs,
        out_shape=[jax.ShapeDtypeStruct((M, DI), BF16),
                   jax.ShapeDtypeStruct(conv_prev.shape, F32),
                   jax.ShapeDtypeStruct(h0.shape, F32)],
        stacked=stacked, n_stacked=1, sem=("parallel", "arbitrary"),
        scratch_shapes=[pltpu.VMEM((T + 8, DI), F32), pltpu.VMEM((T + 8, BCW), F32)])
    return call(u_ssd, u_ssd, u_ssd, u_small, dt_t, conv_prev, h0, *params)


def _sortable_key(x):
    b = lax.bitcast_convert_type(x, I32)
    return b ^ ((b >> 31) & 0x7FFFFFFF)


def _indexer_scores(qi_heads, wi, ki_blk):
    acc = None
    for h in range(IDX_HEADS):
        l = _dot_nt(qi_heads[h], ki_blk) * (IDX_DIM ** -0.5)
        t = jnp.maximum(l, 0.0) * wi[:, h:h + 1]
        acc = t if acc is None else acc + t
    return acc


def _kth_largest(count_ge, n_rows, k):
    c0 = count_ge(jnp.zeros((n_rows, 1), I32))
    t0 = jnp.where(c0 >= k, 0, INT_MIN).astype(I32)

    def body(i, t):
        cand = t | (jnp.int32(1) << (30 - i))
        return jnp.where(count_ge(cand) >= k, cand, t)

    return lax.fori_loop(0, 31, body, t0)


def _limits(pos0, n_rows):
    pos = pos0 + lax.broadcasted_iota(I32, (n_rows, 1), 0)
    return (pos // CHUNK + 1) * CHUNK


def _stack_q_heads(q_ref, g):
    parts = [q_ref[:, (g * Q_PER_KV + i) * HEAD_DIM:(g * Q_PER_KV + i + 1) * HEAD_DIM] for i in range(Q_PER_KV)]
    return (jnp.concatenate(parts, axis=0) * (HEAD_DIM ** -0.5 * LOG2E)).astype(BF16)


def _dsa_prompt_kernel(q_ref, qi_ref, us_ref, ki_ref, k_ref, v_ref, o_ref, key_s, bias_s, *, n_sel, kv_heads):
    QB, KB = DSA_QB, DSA_KB
    j = pl.program_id(1)
    nkb = ((j + 1) * QB + KB - 1) // KB
    wi = us_ref[:, US_WI:US_WI + IDX_HEADS] * (IDX_HEADS ** -0.5)
    qi_heads = [qi_ref[:, h * IDX_DIM:(h + 1) * IDX_DIM].astype(BF16) for h in range(IDX_HEADS)]
    lim = _limits(j * QB, QB)

    def score_blk(kb, carry):
        off = pl.multiple_of(kb * KB, KB)
        sc = _indexer_scores(qi_heads, wi, ki_ref[pl.ds(off, KB), :])
        s_idx = off + lax.broadcasted_iota(I32, (QB, KB), 1)
        key_s[kb] = jnp.where(s_idx < lim, _sortable_key(sc), INT_MIN)
        return carry

    lax.fori_loop(0, nkb, score_blk, 0)

    def count_ge(cand):
        def body(kb, acc):
            m = (key_s[kb] >= cand).astype(I32)
            for c in range(KB // LANES):
                acc = acc + m[:, c * LANES:(c + 1) * LANES]
            return acc
        acc = lax.fori_loop(0, nkb, body, jnp.zeros((QB, LANES), I32))
        return jnp.sum(acc, axis=1, keepdims=True)

    thr = _kth_largest(count_ge, QB, n_sel)

    def bias_blk(kb, carry):
        k = key_s[kb]
        bias_s[kb] = jnp.where((k >= thr) & (k > INT_MIN), 0.0, NEG)
        return carry

    lax.fori_loop(0, nkb, bias_blk, 0)

    rows = Q_PER_KV * QB
    GPL = DSA_GROUPS_PER_LOOP
    for g0 in range(0, kv_heads, GPL):
        groups = range(g0, g0 + GPL)
        qs = [_stack_q_heads(q_ref, g) for g in groups]

        def body(kb, carry):
            off = pl.multiple_of(kb * KB, KB)
            b = bias_s[kb]
            b4 = jnp.concatenate([b] * Q_PER_KV, axis=0)
            new = []
            for n, g in enumerate(groups):
                m, l, acc = carry[n]
                hs = slice(g * HEAD_DIM, (g + 1) * HEAD_DIM)
                t = _dot_nt(qs[n], k_ref[pl.ds(off, KB), hs]) + b4
                m_new = jnp.maximum(m, jnp.max(t, axis=-1, keepdims=True))
                alpha = jnp.exp2(m - m_new)
                p = jnp.exp2(t - m_new)
                l = alpha * l + jnp.sum(p, axis=-1, keepdims=True)
                acc = alpha * acc + _dot(p.astype(BF16), v_ref[pl.ds(off, KB), hs])
                new.append((m_new, l, acc))
            return tuple(new)

        init = tuple((jnp.full((rows, 1), NEG, F32), jnp.zeros((rows, 1), F32), jnp.zeros((rows, HEAD_DIM), F32))
                     for _ in groups)
        res = lax.fori_loop(0, nkb, body, init)
        for n, g in enumerate(groups):
            _, l, acc = res[n]
            out = acc / l
            for i in range(Q_PER_KV):
                o_ref[:, (g * Q_PER_KV + i) * HEAD_DIM:(g * Q_PER_KV + i + 1) * HEAD_DIM] = (
                    out[i * QB:(i + 1) * QB].astype(o_ref.dtype))


def _dsa_prompt(u_att, u_small, ki_b, k_b, v_b, cols, *, n_batch, seq, kv_heads):
    M = u_att.shape[0]
    QB, KB = DSA_QB, DSA_KB
    nq = seq // QB
    nkb_max = seq // KB
    DQ = kv_heads * Q_PER_KV * HEAD_DIM
    DK = kv_heads * HEAD_DIM
    DQI = IDX_HEADS * IDX_DIM
    n_sel = min(TOPK_MAX, seq // 4)
    return pl.pallas_call(
        functools.partial(_dsa_prompt_kernel, n_sel=n_sel, kv_heads=kv_heads),
        grid=(n_batch, nq),
        in_specs=[pl.BlockSpec((QB, DQ), lambda b, j: (b * nq + j, cols['q'] // DQ)),
                  pl.BlockSpec((QB, DQI), lambda b, j: (b * nq + j, cols['qi'] // DQI)),
                  pl.BlockSpec((QB, US_W), lambda b, j: (b * nq + j, 0)),
                  pl.BlockSpec((seq, IDX_DIM), lambda b, j: (b, 0)),
                  pl.BlockSpec((seq, DK), lambda b, j: (b, 0)),
                  pl.BlockSpec((seq, DK), lambda b, j: (b, 0))],
        out_specs=pl.BlockSpec((QB, DQ), lambda b, j: (b * nq + j, 0)),
        out_shape=jax.ShapeDtypeStruct((M, DQ), BF16),
        scratch_shapes=[pltpu.VMEM((nkb_max, QB, KB), I32), pltpu.VMEM((nkb_max, QB, KB), F32)],
        compiler_params=_cparams(("parallel", "arbitrary")),
    )(u_att, u_att, u_small, ki_b, k_b, v_b)


def _dsa_sample_kernel(q_ref, qi_ref, us_ref, kn_ref, vn_ref, ck_ref, cv_ref, cki_ref, o_ref, *,
                       n_sel, kv_heads, past):
    Q = q_ref.shape[0]
    wi = us_ref[:, US_WI:US_WI + IDX_HEADS] * (IDX_HEADS ** -0.5)
    qi_heads = [qi_ref[:, h * IDX_DIM:(h + 1) * IDX_DIM].astype(BF16) for h in range(IDX_HEADS)]
    lim = _limits(past, Q)
    ki_new = us_ref[:, US_KI:US_KI + IDX_DIM].astype(BF16)
    sc_p = _indexer_scores(qi_heads, wi, cki_ref[0].astype(BF16))
    sc_n = _indexer_scores(qi_heads, wi, ki_new)
    idx_p = lax.broadcasted_iota(I32, (Q, past), 1)
    idx_n = past + lax.broadcasted_iota(I32, (Q, Q), 1)
    key_p = jnp.where(idx_p < lim, _sortable_key(sc_p), INT_MIN)
    key_n = jnp.where(idx_n < lim, _sortable_key(sc_n), INT_MIN)

    def count_ge(cand):
        return (jnp.sum((key_p >= cand).astype(I32), axis=1, keepdims=True)
                + jnp.sum((key_n >= cand).astype(I32), axis=1, keepdims=True))

    thr = _kth_largest(count_ge, Q, n_sel)
    bias_p = jnp.where((key_p >= thr) & (key_p > INT_MIN), 0.0, NEG)
    bias_n = jnp.where((key_n >= thr) & (key_n > INT_MIN), 0.0, NEG)
    bias_p = jnp.concatenate([bias_p] * Q_PER_KV, axis=0)
    bias_n = jnp.concatenate([bias_n] * Q_PER_KV, axis=0)

    for g in range(kv_heads):
        hs = slice(g * HEAD_DIM, (g + 1) * HEAD_DIM)
        qs = _stack_q_heads(q_ref, g)
        ck = ck_ref[0, pl.ds(g, past, stride=kv_heads), :].astype(BF16)
        cv = cv_ref[0, pl.ds(g, past, stride=kv_heads), :].astype(BF16)
        t_p = _dot_nt(qs, ck) + bias_p
        t_n = _dot_nt(qs, kn_ref[:, hs].astype(BF16)) + bias_n
        m = jnp.maximum(jnp.max(t_p, axis=-1, keepdims=True), jnp.max(t_n, axis=-1, keepdims=True))
        p_p = jnp.exp2(t_p - m)
        p_n = jnp.exp2(t_n - m)
        l = jnp.sum(p_p, axis=-1, keepdims=True) + jnp.sum(p_n, axis=-1, keepdims=True)
        acc = _dot(p_p.astype(BF16), cv) + _dot(p_n.astype(BF16), vn_ref[:, hs].astype(BF16))
        out = acc / l
        for i in range(Q_PER_KV):
            o_ref[:, (g * Q_PER_KV + i) * HEAD_DIM:(g * Q_PER_KV + i + 1) * HEAD_DIM] = (
                out[i * Q:(i + 1) * Q].astype(o_ref.dtype))


def _dsa_sample(u_att, u_small, cache_k, cache_v, cache_ki, cols, *, n_batch, seq, row0, kv_heads, stacked):
    past = cache_ki.shape[1]
    DQ = kv_heads * Q_PER_KV * HEAD_DIM
    DK = kv_heads * HEAD_DIM
    DQI = IDX_HEADS * IDX_DIM
    n_sel = min(TOPK_MAX, (past + seq) // 4)
    blk0 = row0 // seq
    call = _stacked_call(
        functools.partial(_dsa_sample_kernel, n_sel=n_sel, kv_heads=kv_heads, past=past),
        grid=(n_batch,),
        in_specs=[pl.BlockSpec((seq, DQ), lambda b: (blk0 + b, cols['q'] // DQ)),
                  pl.BlockSpec((seq, DQI), lambda b: (blk0 + b, cols['qi'] // DQI)),
                  pl.BlockSpec((seq, US_W), lambda b: (blk0 + b, 0)),
                  pl.BlockSpec((seq, DK), lambda b: (blk0 + b, cols['k'] // DK)),
                  pl.BlockSpec((seq, DK), lambda b: (blk0 + b, cols['v'] // DK)),
                  pl.BlockSpec((1, past * kv_heads, HEAD_DIM), lambda b: (b, 0, 0)),
                  pl.BlockSpec((1, past * kv_heads, HEAD_DIM), lambda b: (b, 0, 0)),
                  pl.BlockSpec((1, past, IDX_DIM), lambda b: (b, 0, 0))],
        out_specs=[pl.BlockSpec((seq, DQ), lambda b: (blk0 + b, 0))],
        out_shape=[jax.ShapeDtypeStruct(stacked[0].shape, BF16)],
        stacked=stacked, n_stacked=1, sem=("parallel",))
    return call(u_att, u_att, u_small, u_att, u_att, cache_k, cache_v, cache_ki)[0]


def _pair_layout(v, T):
    return jnp.repeat(v.reshape(-1, 2), T, axis=1)


def _dt_transposed(dt_raw, T):
    n, H = dt_raw.shape
    return dt_raw.reshape(n // T, T, H // 2, 2).transpose(0, 2, 3, 1).reshape(n // T, H // 2, 2 * T)


def _ssd_params(conv_w, conv_b, dt_bias, a_log, d_skip, norm_g, T):
    H = dt_bias.shape[0]
    P = SSD_HEAD_DIM
    eye = jnp.eye(H, dtype=F32)
    return dict(conv_w=conv_w, conv_b=conv_b.reshape(1, -1),
                dtb=dt_bias.reshape(1, H), alog=a_log.reshape(1, H),
                dtbT=_pair_layout(dt_bias, T), alogT=_pair_layout(a_log, T),
                dskip=jnp.repeat(d_skip, P).reshape(1, H * P), ng=norm_g.reshape(1, -1),
                expP=jnp.repeat(eye, P, axis=1), expT=jnp.repeat(eye, T, axis=1))


def kernel(x_prompt, x_sample, mem_prompt, cache_k, cache_v, cache_idx_k, cache_mem_k, cache_mem_v,
           state_ssm, state_conv, ln1_g, ln1_b, ffn1_w1, ffn1_w3, ffn1_w2, w_in, conv_w, conv_b,
           dt_bias, a_log, d_skip, ssd_norm_g, w_ssd_br, w_att_br, w_out, ln2_g, ln2_b,
           w_mq, w_mk, w_mv, w_mo, ln3_g, ln3_b, ffn2_w1, ffn2_w3, ffn2_w2, ln4_g, ln4_b):
    assert x_prompt.ndim == 3 and ln1_g.shape[0] == DEPTH == 1
    l = 0
    BP, LP, D = x_prompt.shape
    BS, LS, _ = x_sample.shape
    MP, MS = BP * LP, BS * LS
    M = MP + MS
    H = dt_bias.shape[1]
    assert H == SSD_HEADS
    DI = H * SSD_HEAD_DIM
    BCW = conv_w.shape[2] - DI
    KVH = cache_k.shape[3]
    DQ = KVH * Q_PER_KV * HEAD_DIM
    DK = KVH * HEAD_DIM
    DQI = IDX_HEADS * IDX_DIM
    MT = mem_prompt.shape[1]
    PAST = cache_k.shape[2]
    tm_big = M // 8
    tm_half = M // 16

    w0 = w_in[l]
    c_dt = 2 * DI + BCW
    c_q = c_dt + H
    c_wi = c_q + DQ + 2 * DK + DQI
    c_ki = c_wi + IDX_HEADS
    c_g = c_ki + IDX_DIM
    assert c_g + 2 * D == w0.shape[1]
    cols_ssd = dict(z=0, xs=DI, bc=2 * DI)
    cols_att = dict(q=0, k=DQ, v=DQ + DK, qi=DQ + 2 * DK)
    w_att = w0[:, c_q:c_wi]
    w_gate = w0[:, c_g:]
    n_small = H + IDX_HEADS + IDX_DIM
    assert n_small <= US_W
    w_small = jnp.concatenate([w0[:, c_dt:c_q], w0[:, c_wi:c_g], jnp.zeros((D, US_W - n_small), F32)], axis=1)

    x_p2, x_s2 = x_prompt.reshape(MP, D), x_sample.reshape(MS, D)
    x0b = jnp.concatenate([x_p2.astype(BF16), x_s2.astype(BF16)], axis=0)

    def ffn_delta(xb, w1, w3, w2):
        h = _ffn_up(xb, w1[l], w3[l], tm=tm_big, tn=256)
        return _matmul(h, w2[l].astype(BF16), tm=tm_big, tn=256, out_dtype=F32, single_buffer_x=True)

    d1 = ffn_delta(x0b, ffn1_w1, ffn1_w3, ffn1_w2)
    ln1 = _res_layer_norm(x_p2, d1, ln1_g[l], ln1_b[l], scale=0.5, n_rows=MP, out_rows=M)
    x1, x1b = _res_layer_norm(x_s2, d1, ln1_g[l], ln1_b[l], scale=0.5, n_rows=MS, d_row0=MP, out_row0=MP,
                              stacked=ln1)

    u_ssd = _matmul(x1b, w0, tm=tm_big, tn=512, out_dtype=F32, n_cols=c_dt)
    u_att = _matmul(x1b, w_att, tm=tm_big, tn=512, out_dtype=F32)
    u_gate = _matmul(x1b, w_gate, tm=tm_big, tn=512, out_dtype=F32)
    u_small = _matmul(x1b, w_small, tm=tm_big, tn=US_W, out_dtype=F32)

    dt_raw = u_small[:, US_DT:US_DT + H]
    zeros_conv = jnp.zeros((BP,) + state_conv.shape[2:], F32)
    zeros_h = jnp.zeros((BP, DI, SSD_STATE), F32)
    prm_p = _ssd_params(conv_w[l], conv_b[l], dt_bias[l], a_log[l], d_skip[l], ssd_norm_g[l], CHUNK)
    y_ssd, conv_p, h_p = _ssd_mixer(u_ssd, u_small, _dt_transposed(dt_raw[:MP], CHUNK), zeros_conv, zeros_h,
                                    prm_p, cols_ssd, n_batch=BP, seq=LP, row0=0, T=CHUNK)
    TS = min(CHUNK, LS)
    prm_s = _ssd_params(conv_w[l], conv_b[l], dt_bias[l], a_log[l], d_skip[l], ssd_norm_g[l], TS)
    y_ssd, conv_s, h_s = _ssd_mixer(u_ssd, u_small, _dt_transposed(dt_raw[MP:], TS), state_conv[l],
                                    state_ssm[l].reshape(BS, DI, SSD_STATE),
                                    prm_s, cols_ssd, n_batch=BS, seq=LS, row0=MP, T=TS, stacked=(y_ssd,))

    k_all = u_att[:, cols_att['k']:cols_att['k'] + DK]
    v_all = u_att[:, cols_att['v']:cols_att['v'] + DK]
    ki_all = u_small[:, US_KI:US_KI + IDX_DIM]
    y_att = _dsa_prompt(u_att, u_small, ki_all[:MP].astype(BF16), k_all[:MP].astype(BF16),
                        v_all[:MP].astype(BF16), cols_att, n_batch=BP, seq=LP, kv_heads=KVH)
    y_att = _dsa_sample(u_att, u_small, cache_k[l].reshape(BS, PAST * KVH, HEAD_DIM),
                        cache_v[l].reshape(BS, PAST * KVH, HEAD_DIM), cache_idx_k[l], cols_att,
                        n_batch=BS, seq=LS, row0=MP, kv_heads=KVH, stacked=(y_att,))

    merged = _gated_merge(y_ssd, y_att, w_ssd_br[l], w_att_br[l], u_gate, 0, D, tm=tm_big, tn=256)
    d2 = _matmul(merged, w_out[l], tm=tm_big, tn=512, out_dtype=F32)
    x2, x2b = _res_layer_norm(x1, d2, ln2_g[l], ln2_b[l], scale=1.0, n_rows=M)

    memb = mem_prompt.reshape(BP * MT, D).astype(BF16)
    mk_p = _matmul(memb, w_mk[l], tm=BP * MT, tn=512, out_dtype=F32)
    mv_p = _matmul(memb, w_mv[l], tm=BP * MT, tn=512, out_dtype=F32)
    qm = _matmul(x2b, w_mq[l], tm=tm_big, tn=512, out_dtype=BF16)
    o_m = _memory_attention(qm, mk_p, mv_p, n_batch=BP, rows_per_batch=LP, row0=0, tq=min(512, LP))
    o_m = _memory_attention(qm, cache_mem_k[l].reshape(BS * MT, D), cache_mem_v[l].reshape(BS * MT, D),
                            n_batch=BS, rows_per_batch=LS, row0=MP, tq=LS, stacked=(o_m,))
    d3 = _matmul(o_m, w_mo[l], tm=tm_big, tn=512, out_dtype=F32)
    x3, x3b = _res_layer_norm(x2, d3, ln3_g[l], ln3_b[l], scale=1.0, n_rows=M)

    d4 = ffn_delta(x3b, ffn2_w1, ffn2_w3, ffn2_w2)
    y_p, = _res_layer_norm(x3, d4, ln4_g[l], ln4_b[l], scale=0.5, n_rows=MP, emit_bf16=False)
    y_s, = _res_layer_norm(x3, d4, ln4_g[l], ln4_b[l], scale=0.5, n_rows=MS, x_row0=MP, d_row0=MP, emit_bf16=False)

    mh = D // MEM_HEADS
    return (y_p.reshape(BP, LP, D), y_s.reshape(BS, LS, D),
            h_p.reshape(1, BP, H, SSD_HEAD_DIM, SSD_STATE), conv_p[None],
            k_all[:MP].reshape(1, BP, LP, KVH, HEAD_DIM), v_all[:MP].reshape(1, BP, LP, KVH, HEAD_DIM),
            ki_all[:MP].reshape(1, BP, LP, IDX_DIM),
            mk_p.reshape(1, BP, MT, MEM_HEADS, mh), mv_p.reshape(1, BP, MT, MEM_HEADS, mh),
            h_s.reshape(1, BS, H, SSD_HEAD_DIM, SSD_STATE), conv_s[None],
            k_all[MP:].reshape(1, BS, LS, KVH, HEAD_DIM), v_all[MP:].reshape(1, BS, LS, KVH, HEAD_DIM),
            ki_all[MP:].reshape(1, BS, LS, IDX_DIM))
```

```python
import functools
import math

import jax
import jax.numpy as jnp
from jax import lax
from jax.experimental import pallas as pl
from jax.experimental.pallas import tpu as pltpu

F32 = jnp.float32
BF16 = jnp.bfloat16
I32 = jnp.int32

DEPTH = 1
CHUNK = 64
SSD_HEADS = 64
SSD_HEAD_DIM = 64
SSD_HEADS_PER_GROUP = 8
SSD_STATE = 128
SSD_CONV_W = 4
HEAD_DIM = 128
Q_PER_KV = 4
IDX_HEADS = 16
IDX_DIM = 64
TOPK_MAX = 256
MEM_HEADS = 4
ALPHA = (2.0 * DEPTH) ** 0.25
LN_EPS = 1e-5
RMS_EPS = 1e-5

LANES = 128
VMEM_LIMIT = 56 * 1024 * 1024

NEG = -1e30
INT_MIN = -2147483648
LOG2E = math.log2(math.e)

DSA_QB = 128
DSA_KB = 512
DSA_GROUPS_PER_LOOP = 8
IDX_STACK = 4
LN_ROWS = 256

US_DT = 0
US_WI = US_DT + SSD_HEADS
US_KI = US_WI + IDX_HEADS
US_W = 256


def _cparams(sem):
    return pltpu.CompilerParams(dimension_semantics=sem, vmem_limit_bytes=VMEM_LIMIT)


def _dot(a, b):
    return jnp.dot(a, b, preferred_element_type=F32)


def _dot_nt(a, b):
    return lax.dot_general(a, b, (((1,), (1,)), ((), ())), preferred_element_type=F32)


def _dot_tn(a, b, precision=None):
    return lax.dot_general(a, b, (((0,), (0,)), ((), ())), preferred_element_type=F32,
                           precision=precision)


def _dot_hi(a, b):
    return jnp.dot(a, b, preferred_element_type=F32, precision=lax.Precision.HIGHEST)


def _sigmoid(x):
    return 1.0 / (1.0 + jnp.exp(-x))


def _silu(x):
    return x * _sigmoid(x)


def _softplus(x):
    return jnp.maximum(x, 0.0) + jnp.log1p(jnp.exp(-jnp.abs(x)))


def _stacked_call(kernel, *, grid, in_specs, out_specs, out_shape, stacked, n_stacked, sem, scratch_shapes=()):
    if stacked is None:
        return pl.pallas_call(kernel, grid=grid, in_specs=in_specs, out_specs=out_specs, out_shape=out_shape,
                              scratch_shapes=scratch_shapes, compiler_params=_cparams(sem))
    n_in = len(in_specs)

    def body(*refs):
        kernel(*refs[:n_in], *refs[n_in + n_stacked:])

    call = pl.pallas_call(
        body, grid=grid,
        in_specs=list(in_specs) + [pl.BlockSpec(memory_space=pl.ANY)] * n_stacked,
        out_specs=out_specs, out_shape=out_shape, scratch_shapes=scratch_shapes,
        input_output_aliases={n_in + i: i for i in range(n_stacked)},
        compiler_params=_cparams(sem))
    return lambda *args: call(*args, *stacked)


def _mm_kernel(x_ref, w_ref, o_ref):
    o_ref[...] = _dot(x_ref[...], w_ref[...].astype(BF16)).astype(o_ref.dtype)


def _matmul(x, w, *, tm, tn, out_dtype, col0=0, n_cols=None, single_buffer_x=False):
    M, K = x.shape
    n_cols = w.shape[1] - col0 if n_cols is None else n_cols
    assert col0 % tn == 0 and n_cols % tn == 0 and M % tm == 0
    cb = col0 // tn
    xmode = dict(pipeline_mode=pl.Buffered(1)) if single_buffer_x else {}
    return pl.pallas_call(
        _mm_kernel,
        grid=(M // tm, n_cols // tn),
        in_specs=[pl.BlockSpec((tm, K), lambda i, j: (i, 0), **xmode),
                  pl.BlockSpec((K, tn), lambda i, j: (0, cb + j))],
        out_specs=pl.BlockSpec((tm, tn), lambda i, j: (i, j)),
        out_shape=jax.ShapeDtypeStruct((M, n_cols), out_dtype),
        compiler_params=_cparams(("parallel", "arbitrary")),
    )(x, w)


def _mm_nt_kernel(x_ref, wt_ref, o_ref):
    o_ref[...] = _dot_nt(x_ref[...], wt_ref[...].astype(BF16)).astype(o_ref.dtype)


def _matmul_nt(x, wt, *, row0, n_rows, tm, tn, out_dtype):
    M, K = x.shape
    assert n_rows % tn == 0 and M % tm == 0 and row0 % 8 == 0
    return pl.pallas_call(
        _mm_nt_kernel,
        grid=(M // tm, n_rows // tn),
        in_specs=[pl.BlockSpec((tm, K), lambda i, j: (i, 0)),
                  pl.BlockSpec((pl.Element(tn), pl.Element(K)), lambda i, j: (pl.multiple_of(row0 + j * tn, 8), 0))],
        out_specs=pl.BlockSpec((tm, tn), lambda i, j: (i, j)),
        out_shape=jax.ShapeDtypeStruct((M, n_rows), out_dtype),
        compiler_params=_cparams(("parallel", "arbitrary")),
    )(x, wt)


def _cast_kernel(x_ref, o_ref):
    o_ref[...] = x_ref[...].astype(o_ref.dtype)


def _cast_rows(x, *, out_rows, out_row0, stacked=None):
    n, D = x.shape
    tr = max(r for r in range(16, LN_ROWS + 1, 16) if n % r == 0 and out_row0 % r == 0)
    ob = out_row0 // tr
    call = _stacked_call(
        _cast_kernel, grid=(n // tr,),
        in_specs=[pl.BlockSpec((tr, D), lambda i: (i, 0))],
        out_specs=[pl.BlockSpec((tr, D), lambda i: (ob + i, 0))],
        out_shape=[jax.ShapeDtypeStruct((out_rows, D), BF16)],
        stacked=stacked, n_stacked=1, sem=("parallel",))
    return call(x)


def _ffn_up_kernel(x_ref, w1_ref, w3_ref, o_ref):
    x = x_ref[...]
    a = _dot(x, w1_ref[...].astype(BF16))
    b = _dot(x, w3_ref[...].astype(BF16))
    o_ref[...] = (_silu(a) * b).astype(o_ref.dtype)


def _ffn_up(x, w1, w3, *, tm, tn):
    M, K = x.shape
    N = w1.shape[1]
    return pl.pallas_call(
        _ffn_up_kernel,
        grid=(M // tm, N // tn),
        in_specs=[pl.BlockSpec((tm, K), lambda i, j: (i, 0)),
                  pl.BlockSpec((K, tn), lambda i, j: (0, j)),
                  pl.BlockSpec((K, tn), lambda i, j: (0, j))],
        out_specs=pl.BlockSpec((tm, tn), lambda i, j: (i, j)),
        out_shape=jax.ShapeDtypeStruct((M, N), BF16),
        compiler_params=_cparams(("parallel", "arbitrary")),
    )(x, w1, w3)


def _gate_kernel(ys_ref, ya_ref, ws_ref, wa_ref, gs_ref, ga_ref, o_ref):
    s = _dot(ys_ref[...], ws_ref[...].astype(BF16))
    a = _dot(ya_ref[...], wa_ref[...].astype(BF16))
    o_ref[...] = (_sigmoid(gs_ref[...]) * s + _sigmoid(ga_ref[...]) * a).astype(o_ref.dtype)


def _gated_merge(y_ssd, y_att, w_ssd, w_att, u_gate, gs_col, ga_col, *, tm, tn):
    M, K = y_ssd.shape
    N = w_ssd.shape[1]
    gs_blk, ga_blk = gs_col // tn, ga_col // tn
    once = dict(pipeline_mode=pl.Buffered(1))
    return pl.pallas_call(
        _gate_kernel,
        grid=(M // tm, N // tn),
        in_specs=[pl.BlockSpec((tm, K), lambda i, j: (i, 0), **once),
                  pl.BlockSpec((tm, K), lambda i, j: (i, 0), **once),
                  pl.BlockSpec((K, tn), lambda i, j: (0, j)),
                  pl.BlockSpec((K, tn), lambda i, j: (0, j)),
                  pl.BlockSpec((tm, tn), lambda i, j: (i, gs_blk + j)),
                  pl.BlockSpec((tm, tn), lambda i, j: (i, ga_blk + j))],
        out_specs=pl.BlockSpec((tm, tn), lambda i, j: (i, j)),
        out_shape=jax.ShapeDtypeStruct((M, N), BF16),
        compiler_params=_cparams(("parallel", "arbitrary")),
    )(y_ssd, y_att, w_ssd, w_att, u_gate, u_gate)


def _ln_kernel(x_ref, d_ref, g_ref, b_ref, *o_refs, scale):
    y = ALPHA * x_ref[...] + scale * d_ref[...]
    mu = jnp.mean(y, axis=-1, keepdims=True)
    yc = y - mu
    var = jnp.mean(yc * yc, axis=-1, keepdims=True)
    o = yc * lax.rsqrt(var + LN_EPS) * g_ref[...] + b_ref[...]
    for o_ref in o_refs:
        o_ref[...] = o.astype(o_ref.dtype)


def _res_layer_norm(x, delta, g, b, *, scale, n_rows, x_row0=0, d_row0=0, out_rows=None, out_row0=0,
                    stacked=None, emit_bf16=True):
    D = x.shape[1]
    out_rows = n_rows if out_rows is None else out_rows
    if stacked is not None:
        out_rows = stacked[0].shape[0]
    tr = max(r for r in range(8, LN_ROWS + 1, 8)
             if all(v % r == 0 for v in (n_rows, x_row0, d_row0, out_row0)))
    xb, db, ob = x_row0 // tr, d_row0 // tr, out_row0 // tr
    vec = pl.BlockSpec((1, D), lambda i: (0, 0))
    dts = (F32, BF16) if emit_bf16 else (F32,)
    call = _stacked_call(
        functools.partial(_ln_kernel, scale=scale),
        grid=(n_rows // tr,),
        in_specs=[pl.BlockSpec((tr, D), lambda i: (xb + i, 0)), pl.BlockSpec((tr, D), lambda i: (db + i, 0)), vec, vec],
        out_specs=[pl.BlockSpec((tr, D), lambda i: (ob + i, 0)) for _ in dts],
        out_shape=[jax.ShapeDtypeStruct((out_rows, D), dt) for dt in dts],
        stacked=stacked, n_stacked=len(dts), sem=("parallel",))
    return call(x, delta, g.reshape(1, D), b.reshape(1, D))


def _memattn_kernel(q_ref, k_ref, v_ref, o_ref):
    dh = q_ref.shape[1] // MEM_HEADS
    scale = dh ** -0.5
    for h in range(MEM_HEADS):
        sl = slice(h * dh, (h + 1) * dh)
        q = q_ref[:, sl]
        k = k_ref[:, sl].astype(BF16)
        v = v_ref[:, sl].astype(BF16)
        s = _dot_nt(q, k) * scale
        m = jnp.max(s, axis=-1, keepdims=True)
        p = jnp.exp(s - m)
        p = p / jnp.sum(p, axis=-1, keepdims=True)
        o_ref[:, sl] = _dot(p.astype(BF16), v).astype(o_ref.dtype)


def _memory_attention(q, mem_k, mem_v, *, n_batch, rows_per_batch, row0, tq, stacked=None):
    M, D = q.shape
    mt = mem_k.shape[0] // n_batch
    nq = rows_per_batch // tq
    blk0 = row0 // tq
    qspec = pl.BlockSpec((tq, D), lambda b, j: (blk0 + b * nq + j, 0))
    mspec = pl.BlockSpec((mt, D), lambda b, j: (b, 0))
    call = _stacked_call(
        _memattn_kernel, grid=(n_batch, nq), in_specs=[qspec, mspec, mspec], out_specs=[qspec],
        out_shape=[jax.ShapeDtypeStruct((M, D), BF16)], stacked=stacked, n_stacked=1,
        sem=("parallel", "arbitrary"))
    return call(q, mem_k, mem_v)[0]


def _ssd_kernel(z_ref, xs_ref, bc_ref, us_ref, dtT_ref, cprev_ref, h0_ref,
                convw_ref, convb_ref, dtb_ref, alog_ref, dtbT_ref, alogT_ref, dskip_ref, ng_ref,
                expP_ref, expT_ref,
                y_ref, cnew_ref, h_ref, extx_s, extbc_s, *, T, n_heads):
    P, R, N = SSD_HEAD_DIM, SSD_HEADS_PER_GROUP, SSD_STATE
    G = n_heads // R
    DI = n_heads * P
    GN = G * N
    c = pl.program_id(1)
    W1 = SSD_CONV_W - 1
    base = 8 - W1

    @pl.when(c == 0)
    def _():
        extx_s[base:8, :] = cprev_ref[0, :, :DI]
        extbc_s[base:8, :] = cprev_ref[0, :, DI:]
        h_ref[0] = h0_ref[0]

    extx_s[8:8 + T, :] = xs_ref[...]
    extbc_s[8:8 + T, :] = bc_ref[...]

    def conv(ext, lo, hi):
        out = convb_ref[:, lo:hi] + convw_ref[0:1, lo:hi] * ext[base:base + T, :]
        for j in range(1, SSD_CONV_W):
            out = out + convw_ref[j:j + 1, lo:hi] * ext[base + j:base + j + T, :]
        return _silu(out)

    xs = conv(extx_s, 0, DI)
    bc = conv(extbc_s, DI, DI + 2 * GN)
    newx = extx_s[8 + T - W1:8 + T, :]
    newbc = extbc_s[8 + T - W1:8 + T, :]
    cnew_ref[0, :, :DI] = newx
    cnew_ref[0, :, DI:] = newbc
    extx_s[base:8, :] = newx
    extbc_s[base:8, :] = newbc

    dt = _softplus(us_ref[:, US_DT:US_DT + n_heads] + dtb_ref[...])
    d_a = dt * (-jnp.exp(alog_ref[...]))
    row = lax.broadcasted_iota(I32, (T, T), 0)
    col = lax.broadcasted_iota(I32, (T, T), 1)
    tri = (row >= col).astype(F32)
    a_cum = _dot_hi(tri, d_a)
    a_exp = _dot_hi(a_cum, expP_ref[...])
    dt_exp = _dot_hi(dt, expP_ref[...])
    a_expt = a_exp if T == P else _dot_hi(a_cum, expT_ref[...])
    d_a_t = _softplus(dtT_ref[0] + dtbT_ref[...]) * (-jnp.exp(alogT_ref[...]))
    r2 = lax.broadcasted_iota(I32, (2 * T, 2 * T), 0)
    c2 = lax.broadcasted_iota(I32, (2 * T, 2 * T), 1)
    tri2 = ((r2 // T == c2 // T) & (r2 <= c2)).astype(F32)
    a_cum_t = _dot_hi(d_a_t, tri2)

    l_idx = lax.broadcasted_iota(I32, (T, 2 * T), 0)
    j_idx = lax.broadcasted_iota(I32, (T, 2 * T), 1)
    causal2 = l_idx >= (j_idx % T)
    rr = lax.broadcasted_iota(I32, (2 * T, 2 * P), 0)
    cc = lax.broadcasted_iota(I32, (2 * T, 2 * P), 1)
    blockdiag = (rr // T) == (cc // P)
    sel_last = (lax.broadcasted_iota(I32, (T, LANES), 0) == T - 1).astype(F32)

    xdt = xs * dt_exp
    for g in range(G):
        gs = slice(g * R * P, (g + 1) * R * P)
        b_g = bc[:, g * N:(g + 1) * N].astype(BF16)
        c_g = bc[:, GN + g * N:GN + (g + 1) * N].astype(BF16)
        cb2 = _dot_nt(c_g, jnp.concatenate([b_g, b_g], axis=0))
        h_g = h_ref[0, gs, :]
        y_off = _dot_nt(c_g, h_g.astype(BF16)) * jnp.exp(a_exp[:, gs])
        pairs = []
        for pr in range(R // 2):
            i = g * (R // 2) + pr
            seg = a_expt[:, i * 2 * T:(i + 1) * 2 * T] - a_cum_t[i:i + 1, :]
            decay = jnp.exp(jnp.where(causal2, seg, -jnp.inf))
            m_pair = (cb2 * decay).astype(BF16)
            x_pair = xdt[:, i * 2 * P:(i + 1) * 2 * P]
            rhs = jnp.where(blockdiag, jnp.concatenate([x_pair, x_pair], axis=0), 0.0).astype(BF16)
            pairs.append(_dot(m_pair, rhs))
        y_g = jnp.concatenate(pairs, axis=1) + y_off + dskip_ref[:, gs] * xs[:, gs]
        a_g = a_exp[:, gs]
        dte = jnp.exp(a_g[T - 1:T, :] - a_g)
        st = _dot_tn((xdt[:, gs] * dte).astype(BF16), b_g)
        dec = jnp.exp(_dot_tn(a_g, sel_last, precision=lax.Precision.HIGHEST))
        h_ref[0, gs, :] = dec * h_g + st
        yg = y_g * _silu(z_ref[:, gs])
        ms = jnp.mean(yg * yg, axis=-1, keepdims=True)
        y_ref[:, gs] = (yg * lax.rsqrt(ms + RMS_EPS) * ng_ref[:, gs]).astype(y_ref.dtype)


def _ssd_mixer(u_ssd, u_small, dt_t, conv_prev, h0, prm, cols, *, n_batch, seq, row0, T, stacked=None):
    M = u_ssd.shape[0]
    n_heads = prm['dtb'].shape[1]
    DI = n_heads * SSD_HEAD_DIM
    BCW = conv_prev.shape[2] - DI
    nc = seq // T
    blk0 = row0 // T

    def rows(width, colblk):
        return pl.BlockSpec((T, width), lambda b, c: (blk0 + b * nc + c, colblk))

    def full(a):
        return pl.BlockSpec(a.shape, lambda b, c: (0,) * a.ndim)

    params = [prm['conv_w'], prm['conv_b'], prm['dtb'], prm['alog'], prm['dtbT'], prm['alogT'],
              prm['dskip'], prm['ng'], prm['expP'], prm['expT']]
    in_specs = [rows(DI, cols['z'] // DI), rows(DI, cols['xs'] // DI), rows(BCW, cols['bc'] // BCW),
                rows(US_W, 0),
                pl.BlockSpec((1,) + dt_t.shape[1:], lambda b, c: (b * nc + c, 0, 0)),
                pl.BlockSpec((1,) + conv_prev.shape[1:], lambda b, c: (b, 0, 0)),
                pl.BlockSpec((1,) + h0.shape[1:], lambda b, c: (b, 0, 0))] + [full(a) for a in params]
    out_specs = [rows(DI, 0),
                 pl.BlockSpec((1,) + conv_prev.shape[1:], lambda b, c: (b, 0, 0)),
                 pl.BlockSpec((1,) + h0.shape[1:], lambda b, c: (b, 0, 0))]
    call = _stacked_call(
        functools.partial(_ssd_kernel, T=T, n_heads=n_heads),
        grid=(n_batch, nc), in_specs=in_specs, out_specs=out_specs,
        out_shape=[jax.ShapeDtypeStruct((M, DI), BF16),
                   jax.ShapeDtypeStruct(conv_prev.shape, F32),
                   jax.ShapeDtypeStruct(h0.shape, F32)],
        stacked=stacked, n_stacked=1, sem=("parallel", "arbitrary"),
        scratch_shapes=[pltpu.VMEM((T + 8, DI), F32), pltpu.VMEM((T + 8, BCW), F32)])
    return call(u_ssd, u_ssd, u_ssd, u_small, dt_t, conv_prev, h0, *params)


def _sortable_key(x):
    b = lax.bitcast_convert_type(x, I32)
    return b ^ ((b >> 31) & 0x7FFFFFFF)


def _head_weights(us_ref):
    return us_ref[:, US_WI:US_WI + IDX_HEADS] * ((IDX_HEADS * IDX_DIM) ** -0.5)


def _indexer_scores(qi_heads, wi, ki_blk):
    Q = qi_heads[0].shape[0]
    acc = None
    for h0 in range(0, IDX_HEADS, IDX_STACK):
        l = _dot_nt(jnp.concatenate(qi_heads[h0:h0 + IDX_STACK], axis=0), ki_blk)
        for n in range(IDX_STACK):
            t = jnp.maximum(l[n * Q:(n + 1) * Q], 0.0) * wi[:, h0 + n:h0 + n + 1]
            acc = t if acc is None else acc + t
    return acc


def _kth_largest(count_ge, shape, k):
    c0 = count_ge(jnp.zeros(shape, I32))
    t0 = jnp.where(c0 >= k, 0, INT_MIN).astype(I32)

    def body(i, t):
        cand = t | (jnp.int32(1) << (30 - i))
        return jnp.where(count_ge(cand) >= k, cand, t)

    return lax.fori_loop(0, 31, body, t0)


def _limits(pos0, n_rows):
    pos = pos0 + lax.broadcasted_iota(I32, (n_rows, 1), 0)
    return (pos // CHUNK + 1) * CHUNK


def _stack_q_heads(q_ref, g):
    parts = [q_ref[:, (g * Q_PER_KV + i) * HEAD_DIM:(g * Q_PER_KV + i + 1) * HEAD_DIM] for i in range(Q_PER_KV)]
    return (jnp.concatenate(parts, axis=0) * (HEAD_DIM ** -0.5 * LOG2E)).astype(BF16)


def _dsa_prompt_kernel(q_ref, qi_ref, us_ref, ki_ref, k_ref, vt_ref, o_ref, key_s, bias_s, *, n_sel, kv_heads):
    QB, KB = DSA_QB, DSA_KB
    j = pl.program_id(1)
    nkb = ((j + 1) * QB + KB - 1) // KB
    wi = _head_weights(us_ref)
    qi_heads = [qi_ref[:, h * IDX_DIM:(h + 1) * IDX_DIM].astype(BF16) for h in range(IDX_HEADS)]
    pos = j * QB + lax.broadcasted_iota(I32, (1, QB), 1)
    lim = (pos // CHUNK + 1) * CHUNK

    def score_blk(kb, carry):
        off = pl.multiple_of(kb * KB, KB)
        sc = _indexer_scores(qi_heads, wi, ki_ref[pl.ds(off, KB), :])
        s_idx = off + lax.broadcasted_iota(I32, (KB, QB), 0)
        key_s[kb] = jnp.where(s_idx < lim, _sortable_key(sc.T), INT_MIN)
        return carry

    lax.fori_loop(0, nkb, score_blk, 0)

    def count_ge(cand):
        def body(kb, acc):
            m = (key_s[kb] >= cand).astype(I32)
            return acc + jnp.sum(m.reshape(KB // 8, 8, QB), axis=0)
        acc = lax.fori_loop(0, nkb, body, jnp.zeros((8, QB), I32))
        return jnp.sum(acc, axis=0, keepdims=True)

    thr = _kth_largest(count_ge, (1, QB), n_sel)

    def bias_blk(kb, carry):
        k = key_s[kb]
        bias_s[kb] = jnp.where((k >= thr) & (k > INT_MIN), 0.0, NEG)
        return carry

    lax.fori_loop(0, nkb, bias_blk, 0)

    cols = Q_PER_KV * QB
    GPL = DSA_GROUPS_PER_LOOP
    for g0 in range(0, kv_heads, GPL):
        groups = range(g0, g0 + GPL)
        qs = [_stack_q_heads(q_ref, g) for g in groups]

        def body(kb, carry):
            off = pl.multiple_of(kb * KB, KB)
            b = bias_s[kb]
            b4 = jnp.concatenate([b] * Q_PER_KV, axis=1)
            heads = [slice(g * HEAD_DIM, (g + 1) * HEAD_DIM) for g in groups]

            def logits(n):
                return _dot_nt(k_ref[pl.ds(off, KB), heads[n]], qs[n]) + b4

            t_next = logits(0)
            new = []
            for n in range(GPL):
                t = t_next
                if n + 1 < GPL:
                    t_next = logits(n + 1)
                m, l, acc = carry[n]
                m_new = jnp.maximum(m, jnp.max(t, axis=0, keepdims=True))
                alpha = jnp.exp2(m - m_new)
                p = jnp.exp2(t - m_new)
                l = alpha * l + jnp.sum(p, axis=0, keepdims=True)
                acc = alpha * acc + _dot(vt_ref[0, kb, heads[n], :], p.astype(BF16))
                new.append((m_new, l, acc))
            return tuple(new)

        init = tuple((jnp.full((1, cols), NEG, F32), jnp.zeros((1, cols), F32), jnp.zeros((HEAD_DIM, cols), F32))
                     for _ in groups)
        res = lax.fori_loop(0, nkb, body, init)
        for n, g in enumerate(groups):
            _, l, acc = res[n]
            out = acc / l
            for i in range(Q_PER_KV):
                o_ref[:, (g * Q_PER_KV + i) * HEAD_DIM:(g * Q_PER_KV + i + 1) * HEAD_DIM] = (
                    out[:, i * QB:(i + 1) * QB].T.astype(o_ref.dtype))


def _dsa_prompt(u_att, u_small, ki_b, k_b, vt_b, cols, *, n_batch, seq, kv_heads):
    M = u_att.shape[0]
    QB, KB = DSA_QB, DSA_KB
    nq = seq // QB
    nkb_max = seq // KB
    DQ = kv_heads * Q_PER_KV * HEAD_DIM
    DK = kv_heads * HEAD_DIM
    DQI = IDX_HEADS * IDX_DIM
    n_sel = min(TOPK_MAX, seq // 4)
    return pl.pallas_call(
        functools.partial(_dsa_prompt_kernel, n_sel=n_sel, kv_heads=kv_heads),
        grid=(n_batch, nq),
        in_specs=[pl.BlockSpec((QB, DQ), lambda b, j: (b * nq + j, cols['q'] // DQ)),
                  pl.BlockSpec((QB, DQI), lambda b, j: (b * nq + j, cols['qi'] // DQI)),
                  pl.BlockSpec((QB, US_W), lambda b, j: (b * nq + j, 0)),
                  pl.BlockSpec((seq, IDX_DIM), lambda b, j: (b, 0)),
                  pl.BlockSpec((seq, DK), lambda b, j: (b, 0)),
                  pl.BlockSpec((1, nkb_max, DK, KB), lambda b, j: (b, 0, 0, 0))],
        out_specs=pl.BlockSpec((QB, DQ), lambda b, j: (b * nq + j, 0)),
        out_shape=jax.ShapeDtypeStruct((M, DQ), BF16),
        scratch_shapes=[pltpu.VMEM((nkb_max, KB, QB), I32), pltpu.VMEM((nkb_max, KB, QB), F32)],
        compiler_params=_cparams(("parallel", "arbitrary")),
    )(u_att, u_att, u_small, ki_b, k_b, vt_b)


def _dsa_sample_kernel(q_ref, qi_ref, us_ref, kn_ref, vn_ref, ck_ref, cv_ref, cki_ref, o_ref, *,
                       n_sel, kv_heads, past):
    Q = q_ref.shape[0]
    wi = _head_weights(us_ref)
    qi_heads = [qi_ref[:, h * IDX_DIM:(h + 1) * IDX_DIM].astype(BF16) for h in range(IDX_HEADS)]
    lim = _limits(past, Q)
    ki_new = us_ref[:, US_KI:US_KI + IDX_DIM].astype(BF16)
    sc_p = _indexer_scores(qi_heads, wi, cki_ref[0].astype(BF16))
    sc_n = _indexer_scores(qi_heads, wi, ki_new)
    idx_p = lax.broadcasted_iota(I32, (Q, past), 1)
    idx_n = past + lax.broadcasted_iota(I32, (Q, Q), 1)
    key_p = jnp.where(idx_p < lim, _sortable_key(sc_p), INT_MIN)
    key_n = jnp.where(idx_n < lim, _sortable_key(sc_n), INT_MIN)

    def count_ge(cand):
        return (jnp.sum((key_p >= cand).astype(I32), axis=1, keepdims=True)
                + jnp.sum((key_n >= cand).astype(I32), axis=1, keepdims=True))

    thr = _kth_largest(count_ge, (Q, 1), n_sel)
    bias_p = jnp.where((key_p >= thr) & (key_p > INT_MIN), 0.0, NEG)
    bias_n = jnp.where((key_n >= thr) & (key_n > INT_MIN), 0.0, NEG)
    bias_p = jnp.concatenate([bias_p] * Q_PER_KV, axis=0)
    bias_n = jnp.concatenate([bias_n] * Q_PER_KV, axis=0)

    for g in range(kv_heads):
        hs = slice(g * HEAD_DIM, (g + 1) * HEAD_DIM)
        qs = _stack_q_heads(q_ref, g)
        ck = ck_ref[0, pl.ds(g, past, stride=kv_heads), :].astype(BF16)
        cv = cv_ref[0, pl.ds(g, past, stride=kv_heads), :].astype(BF16)
        t_p = _dot_nt(qs, ck) + bias_p
        t_n = _dot_nt(qs, kn_ref[:, hs].astype(BF16)) + bias_n
        m = jnp.maximum(jnp.max(t_p, axis=-1, keepdims=True), jnp.max(t_n, axis=-1, keepdims=True))
        p_p = jnp.exp2(t_p - m)
        p_n = jnp.exp2(t_n - m)
        l = jnp.sum(p_p, axis=-1, keepdims=True) + jnp.sum(p_n, axis=-1, keepdims=True)
        acc = _dot(p_p.astype(BF16), cv) + _dot(p_n.astype(BF16), vn_ref[:, hs].astype(BF16))
        out = acc / l
        for i in range(Q_PER_KV):
            o_ref[:, (g * Q_PER_KV + i) * HEAD_DIM:(g * Q_PER_KV + i + 1) * HEAD_DIM] = (
                out[i * Q:(i + 1) * Q].astype(o_ref.dtype))


def _dsa_sample(u_att, u_small, cache_k, cache_v, cache_ki, cols, *, n_batch, seq, row0, kv_heads, stacked):
    past = cache_ki.shape[1]
    DQ = kv_heads * Q_PER_KV * HEAD_DIM
    DK = kv_heads * HEAD_DIM
    DQI = IDX_HEADS * IDX_DIM
    n_sel = min(TOPK_MAX, (past + seq) // 4)
    blk0 = row0 // seq
    call = _stacked_call(
        functools.partial(_dsa_sample_kernel, n_sel=n_sel, kv_heads=kv_heads, past=past),
        grid=(n_batch,),
        in_specs=[pl.BlockSpec((seq, DQ), lambda b: (blk0 + b, cols['q'] // DQ)),
                  pl.BlockSpec((seq, DQI), lambda b: (blk0 + b, cols['qi'] // DQI)),
                  pl.BlockSpec((seq, US_W), lambda b: (blk0 + b, 0)),
                  pl.BlockSpec((seq, DK), lambda b: (blk0 + b, cols['k'] // DK)),
                  pl.BlockSpec((seq, DK), lambda b: (blk0 + b, cols['v'] // DK)),
                  pl.BlockSpec((1, past * kv_heads, HEAD_DIM), lambda b: (b, 0, 0)),
                  pl.BlockSpec((1, past * kv_heads, HEAD_DIM), lambda b: (b, 0, 0)),
                  pl.BlockSpec((1, past, IDX_DIM), lambda b: (b, 0, 0))],
        out_specs=[pl.BlockSpec((seq, DQ), lambda b: (blk0 + b, 0))],
        out_shape=[jax.ShapeDtypeStruct(stacked[0].shape, BF16)],
        stacked=stacked, n_stacked=1, sem=("parallel",))
    return call(u_att, u_att, u_small, u_att, u_att, cache_k, cache_v, cache_ki)[0]


def _pair_layout(v, T):
    return jnp.repeat(v.reshape(-1, 2), T, axis=1)


def _dt_transposed(dt_raw, T):
    n, H = dt_raw.shape
    return dt_raw.reshape(n // T, T, H // 2, 2).transpose(0, 2, 3, 1).reshape(n // T, H // 2, 2 * T)


def _ssd_params(conv_w, conv_b, dt_bias, a_log, d_skip, norm_g, T):
    H = dt_bias.shape[0]
    P = SSD_HEAD_DIM
    eye = jnp.eye(H, dtype=F32)
    return dict(conv_w=conv_w, conv_b=conv_b.reshape(1, -1),
                dtb=dt_bias.reshape(1, H), alog=a_log.reshape(1, H),
                dtbT=_pair_layout(dt_bias, T), alogT=_pair_layout(a_log, T),
                dskip=jnp.repeat(d_skip, P).reshape(1, H * P), ng=norm_g.reshape(1, -1),
                expP=jnp.repeat(eye, P, axis=1), expT=jnp.repeat(eye, T, axis=1))


def kernel(x_prompt, x_sample, mem_prompt, cache_k, cache_v, cache_idx_k, cache_mem_k, cache_mem_v,
           state_ssm, state_conv, ln1_g, ln1_b, ffn1_w1, ffn1_w3, ffn1_w2, w_in, conv_w, conv_b,
           dt_bias, a_log, d_skip, ssd_norm_g, w_ssd_br, w_att_br, w_out, ln2_g, ln2_b,
           w_mq, w_mk, w_mv, w_mo, ln3_g, ln3_b, ffn2_w1, ffn2_w3, ffn2_w2, ln4_g, ln4_b):
    assert x_prompt.ndim == 3 and ln1_g.shape[0] == DEPTH == 1
    l = 0
    BP, LP, D = x_prompt.shape
    BS, LS, _ = x_sample.shape
    MP, MS = BP * LP, BS * LS
    M = MP + MS
    H = dt_bias.shape[1]
    assert H == SSD_HEADS
    DI = H * SSD_HEAD_DIM
    BCW = conv_w.shape[2] - DI
    KVH = cache_k.shape[3]
    DQ = KVH * Q_PER_KV * HEAD_DIM
    DK = KVH * HEAD_DIM
    DQI = IDX_HEADS * IDX_DIM
    MT = mem_prompt.shape[1]
    PAST = cache_k.shape[2]
    tm_big = M // 8
    tm_half = M // 16

    w0t = w_in[l].T
    c_dt = 2 * DI + BCW
    c_q = c_dt + H
    c_wi = c_q + DQ + 2 * DK + DQI
    c_ki = c_wi + IDX_HEADS
    c_g = c_ki + IDX_DIM
    assert c_g + 2 * D == w0t.shape[0]
    cols_ssd = dict(z=0, xs=DI, bc=2 * DI)
    cols_att = dict(q=0, k=DQ, v=DQ + DK, qi=DQ + 2 * DK)
    n_small = H + IDX_HEADS + IDX_DIM
    assert n_small <= US_W
    w_small_t = jnp.concatenate([w0t[c_dt:c_q], w0t[c_wi:c_g], jnp.zeros((US_W - n_small, D), F32)], axis=0)

    x_p2, x_s2 = x_prompt.reshape(MP, D), x_sample.reshape(MS, D)
    x0b = _cast_rows(x_p2, out_rows=M, out_row0=0)
    x0b, = _cast_rows(x_s2, out_rows=M, out_row0=MP, stacked=x0b)

    def ffn_delta(xb, w1, w3, w2):
        h = _ffn_up(xb, w1[l], w3[l], tm=tm_big, tn=256)
        return _matmul(h, w2[l].astype(BF16), tm=tm_big, tn=256, out_dtype=F32, single_buffer_x=True)

    d1 = ffn_delta(x0b, ffn1_w1, ffn1_w3, ffn1_w2)
    ln1 = _res_layer_norm(x_p2, d1, ln1_g[l], ln1_b[l], scale=0.5, n_rows=MP, out_rows=M)
    x1, x1b = _res_layer_norm(x_s2, d1, ln1_g[l], ln1_b[l], scale=0.5, n_rows=MS, d_row0=MP, out_row0=MP,
                              stacked=ln1)

    u_ssd = _matmul_nt(x1b, w0t, row0=0, n_rows=c_dt, tm=tm_big, tn=512, out_dtype=F32)
    u_att = _matmul_nt(x1b, w0t, row0=c_q, n_rows=c_wi - c_q, tm=tm_big, tn=512, out_dtype=F32)
    u_gate = _matmul_nt(x1b, w0t, row0=c_g, n_rows=2 * D, tm=tm_big, tn=512, out_dtype=F32)
    u_small = _matmul_nt(x1b, w_small_t, row0=0, n_rows=US_W, tm=tm_big, tn=US_W, out_dtype=F32)

    dt_raw = u_small[:, US_DT:US_DT + H]
    zeros_conv = jnp.zeros((BP,) + state_conv.shape[2:], F32)
    zeros_h = jnp.zeros((BP, DI, SSD_STATE), F32)
    prm_p = _ssd_params(conv_w[l], conv_b[l], dt_bias[l], a_log[l], d_skip[l], ssd_norm_g[l], CHUNK)
    y_ssd, conv_p, h_p = _ssd_mixer(u_ssd, u_small, _dt_transposed(dt_raw[:MP], CHUNK), zeros_conv, zeros_h,
                                    prm_p, cols_ssd, n_batch=BP, seq=LP, row0=0, T=CHUNK)
    TS = min(CHUNK, LS)
    prm_s = _ssd_params(conv_w[l], conv_b[l], dt_bias[l], a_log[l], d_skip[l], ssd_norm_g[l], TS)
    y_ssd, conv_s, h_s = _ssd_mixer(u_ssd, u_small, _dt_transposed(dt_raw[MP:], TS), state_conv[l],
                                    state_ssm[l].reshape(BS, DI, SSD_STATE),
                                    prm_s, cols_ssd, n_batch=BS, seq=LS, row0=MP, T=TS, stacked=(y_ssd,))

    k_all = u_att[:, cols_att['k']:cols_att['k'] + DK]
    v_all = u_att[:, cols_att['v']:cols_att['v'] + DK]
    ki_all = u_small[:, US_KI:US_KI + IDX_DIM]
    vt_b = v_all[:MP].astype(BF16).reshape(BP, LP // DSA_KB, DSA_KB, DK).transpose(0, 1, 3, 2)
    y_att = _dsa_prompt(u_att, u_small, ki_all[:MP].astype(BF16), k_all[:MP].astype(BF16),
                        vt_b, cols_att, n_batch=BP, seq=LP, kv_heads=KVH)
    y_att = _dsa_sample(u_att, u_small, cache_k[l].reshape(BS, PAST * KVH, HEAD_DIM),
                        cache_v[l].reshape(BS, PAST * KVH, HEAD_DIM), cache_idx_k[l], cols_att,
                        n_batch=BS, seq=LS, row0=MP, kv_heads=KVH, stacked=(y_att,))

    merged = _gated_merge(y_ssd, y_att, w_ssd_br[l], w_att_br[l], u_gate, 0, D, tm=tm_big, tn=256)
    d2 = _matmul(merged, w_out[l], tm=tm_big, tn=512, out_dtype=F32)
    x2, x2b = _res_layer_norm(x1, d2, ln2_g[l], ln2_b[l], scale=1.0, n_rows=M)

    memb = mem_prompt.reshape(BP * MT, D).astype(BF16)
    mk_p = _matmul(memb, w_mk[l], tm=BP * MT, tn=512, out_dtype=F32)
    mv_p = _matmul(memb, w_mv[l], tm=BP * MT, tn=512, out_dtype=F32)
    qm = _matmul(x2b, w_mq[l], tm=tm_big, tn=512, out_dtype=BF16)
    o_m = _memory_attention(qm, mk_p, mv_p, n_batch=BP, rows_per_batch=LP, row0=0, tq=min(512, LP))
    o_m = _memory_attention(qm, cache_mem_k[l].reshape(BS * MT, D), cache_mem_v[l].reshape(BS * MT, D),
                            n_batch=BS, rows_per_batch=LS, row0=MP, tq=LS, stacked=(o_m,))
    d3 = _matmul(o_m, w_mo[l], tm=tm_big, tn=512, out_dtype=F32)
    x3, x3b = _res_layer_norm(x2, d3, ln3_g[l], ln3_b[l], scale=1.0, n_rows=M)

    d4 = ffn_delta(x3b, ffn2_w1, ffn2_w3, ffn2_w2)
    y_p, = _res_layer_norm(x3, d4, ln4_g[l], ln4_b[l], scale=0.5, n_rows=MP, emit_bf16=False)
    y_s, = _res_layer_norm(x3, d4, ln4_g[l], ln4_b[l], scale=0.5, n_rows=MS, x_row0=MP, d_row0=MP, emit_bf16=False)

    mh = D // MEM_HEADS
    return (y_p.reshape(BP, LP, D), y_s.reshape(BS, LS, D),
            h_p.reshape(1, BP, H, SSD_HEAD_DIM, SSD_STATE), conv_p[None],
            k_all[:MP].reshape(1, BP, LP, KVH, HEAD_DIM), v_all[:MP].reshape(1, BP, LP, KVH, HEAD_DIM),
            ki_all[:MP].reshape(1, BP, LP, IDX_DIM),
            mk_p.reshape(1, BP, MT, MEM_HEADS, mh), mv_p.reshape(1, BP, MT, MEM_HEADS, mh),
            h_s.reshape(1, BS, H, SSD_HEAD_DIM, SSD_STATE), conv_s[None],
            k_all[MP:].reshape(1, BS, LS, KVH, HEAD_DIM), v_all[MP:].reshape(1, BS, LS, KVH, HEAD_DIM),
            ki_all[MP:].reshape(1, BS, LS, IDX_DIM))
```

```python
import functools
import math

import jax
import jax.numpy as jnp
from jax import lax
from jax.experimental import pallas as pl
from jax.experimental.pallas import tpu as pltpu

F32 = jnp.float32
BF16 = jnp.bfloat16
I32 = jnp.int32

DEPTH = 1
CHUNK = 64
SSD_HEADS = 64
SSD_HEAD_DIM = 64
SSD_HEADS_PER_GROUP = 8
SSD_STATE = 128
SSD_CONV_W = 4
HEAD_DIM = 128
Q_PER_KV = 4
IDX_HEADS = 16
IDX_DIM = 64
TOPK_MAX = 256
MEM_HEADS = 4
ALPHA = (2.0 * DEPTH) ** 0.25
LN_EPS = 1e-5
RMS_EPS = 1e-5

LANES = 128
VMEM_LIMIT = 56 * 1024 * 1024

NEG = -1e30
INT_MIN = -2147483648
LOG2E = math.log2(math.e)

DSA_QB = 128
DSA_KB = 512
DSA_GROUPS_PER_LOOP = 8
IDX_STACK = 4
LN_ROWS = 256
MRL_ROWS = 512
MRL_K = 512

US_DT = 0
US_WI = US_DT + SSD_HEADS
US_KI = US_WI + IDX_HEADS
US_W = 256


def _cparams(sem):
    return pltpu.CompilerParams(dimension_semantics=sem, vmem_limit_bytes=VMEM_LIMIT)


def _dot(a, b):
    return jnp.dot(a, b, preferred_element_type=F32)


def _dot_nt(a, b):
    return lax.dot_general(a, b, (((1,), (1,)), ((), ())), preferred_element_type=F32)


def _dot_tn(a, b, precision=None):
    return lax.dot_general(a, b, (((0,), (0,)), ((), ())), preferred_element_type=F32,
                           precision=precision)


def _dot_hi(a, b):
    return jnp.dot(a, b, preferred_element_type=F32, precision=lax.Precision.HIGHEST)


def _sigmoid(x):
    return 1.0 / (1.0 + jnp.exp(-x))


def _silu(x):
    return x * _sigmoid(x)


def _softplus(x):
    return jnp.maximum(x, 0.0) + jnp.log1p(jnp.exp(-jnp.abs(x)))


def _stacked_call(kernel, *, grid, in_specs, out_specs, out_shape, stacked, n_stacked, sem, scratch_shapes=()):
    if stacked is None:
        return pl.pallas_call(kernel, grid=grid, in_specs=in_specs, out_specs=out_specs, out_shape=out_shape,
                              scratch_shapes=scratch_shapes, compiler_params=_cparams(sem))
    n_in = len(in_specs)

    def body(*refs):
        kernel(*refs[:n_in], *refs[n_in + n_stacked:])

    call = pl.pallas_call(
        body, grid=grid,
        in_specs=list(in_specs) + [pl.BlockSpec(memory_space=pl.ANY)] * n_stacked,
        out_specs=out_specs, out_shape=out_shape, scratch_shapes=scratch_shapes,
        input_output_aliases={n_in + i: i for i in range(n_stacked)},
        compiler_params=_cparams(sem))
    return lambda *args: call(*args, *stacked)


def _mm_kernel(x_ref, w_ref, o_ref):
    o_ref[...] = _dot(x_ref[...], w_ref[...].astype(BF16)).astype(o_ref.dtype)


def _matmul(x, w, *, tm, tn, out_dtype, col0=0, n_cols=None, single_buffer_x=False):
    M, K = x.shape
    n_cols = w.shape[1] - col0 if n_cols is None else n_cols
    assert col0 % tn == 0 and n_cols % tn == 0 and M % tm == 0
    cb = col0 // tn
    xmode = dict(pipeline_mode=pl.Buffered(1)) if single_buffer_x else {}
    return pl.pallas_call(
        _mm_kernel,
        grid=(M // tm, n_cols // tn),
        in_specs=[pl.BlockSpec((tm, K), lambda i, j: (i, 0), **xmode),
                  pl.BlockSpec((K, tn), lambda i, j: (0, cb + j))],
        out_specs=pl.BlockSpec((tm, tn), lambda i, j: (i, j)),
        out_shape=jax.ShapeDtypeStruct((M, n_cols), out_dtype),
        compiler_params=_cparams(("parallel", "arbitrary")),
    )(x, w)


def _mm_nt_kernel(x_ref, wt_ref, o_ref):
    o_ref[...] = _dot_nt(x_ref[...], wt_ref[...].astype(BF16)).astype(o_ref.dtype)


def _matmul_nt(x, wt, *, row0, n_rows, tm, tn, out_dtype):
    M, K = x.shape
    assert n_rows % tn == 0 and M % tm == 0 and row0 % 8 == 0
    return pl.pallas_call(
        _mm_nt_kernel,
        grid=(M // tm, n_rows // tn),
        in_specs=[pl.BlockSpec((tm, K), lambda i, j: (i, 0)),
                  pl.BlockSpec((pl.Element(tn), pl.Element(K)), lambda i, j: (pl.multiple_of(row0 + j * tn, 8), 0))],
        out_specs=pl.BlockSpec((tm, tn), lambda i, j: (i, j)),
        out_shape=jax.ShapeDtypeStruct((M, n_rows), out_dtype),
        compiler_params=_cparams(("parallel", "arbitrary")),
    )(x, wt)


def _cast_kernel(x_ref, o_ref):
    o_ref[...] = x_ref[...].astype(o_ref.dtype)


def _cast_rows(x, *, out_rows, out_row0, stacked=None):
    n, D = x.shape
    tr = max(r for r in range(16, LN_ROWS + 1, 16) if n % r == 0 and out_row0 % r == 0)
    ob = out_row0 // tr
    call = _stacked_call(
        _cast_kernel, grid=(n // tr,),
        in_specs=[pl.BlockSpec((tr, D), lambda i: (i, 0))],
        out_specs=[pl.BlockSpec((tr, D), lambda i: (ob + i, 0))],
        out_shape=[jax.ShapeDtypeStruct((out_rows, D), BF16)],
        stacked=stacked, n_stacked=1, sem=("parallel",))
    return call(x)


def _ffn_up_kernel(x_ref, w1_ref, w3_ref, o_ref):
    x = x_ref[...]
    a = _dot(x, w1_ref[...].astype(BF16))
    b = _dot(x, w3_ref[...].astype(BF16))
    o_ref[...] = (_silu(a) * b).astype(o_ref.dtype)


def _ffn_up(x, w1, w3, *, tm, tn):
    M, K = x.shape
    N = w1.shape[1]
    return pl.pallas_call(
        _ffn_up_kernel,
        grid=(M // tm, N // tn),
        in_specs=[pl.BlockSpec((tm, K), lambda i, j: (i, 0), pipeline_mode=pl.Buffered(1)),
                  pl.BlockSpec((K, tn), lambda i, j: (0, j)),
                  pl.BlockSpec((K, tn), lambda i, j: (0, j))],
        out_specs=pl.BlockSpec((tm, tn), lambda i, j: (i, j)),
        out_shape=jax.ShapeDtypeStruct((M, N), BF16),
        compiler_params=_cparams(("parallel", "arbitrary")),
    )(x, w1, w3)


def _gate_kernel(ys_ref, ya_ref, ws_ref, wa_ref, gs_ref, ga_ref, o_ref):
    s = _dot(ys_ref[...], ws_ref[...].astype(BF16))
    a = _dot(ya_ref[...], wa_ref[...].astype(BF16))
    o_ref[...] = (_sigmoid(gs_ref[...]) * s + _sigmoid(ga_ref[...]) * a).astype(o_ref.dtype)


def _gated_merge(y_ssd, y_att, w_ssd, w_att, u_gate, gs_col, ga_col, *, tm, tn):
    M, K = y_ssd.shape
    N = w_ssd.shape[1]
    gs_blk, ga_blk = gs_col // tn, ga_col // tn
    once = dict(pipeline_mode=pl.Buffered(1))
    return pl.pallas_call(
        _gate_kernel,
        grid=(M // tm, N // tn),
        in_specs=[pl.BlockSpec((tm, K), lambda i, j: (i, 0), **once),
                  pl.BlockSpec((tm, K), lambda i, j: (i, 0), **once),
                  pl.BlockSpec((K, tn), lambda i, j: (0, j)),
                  pl.BlockSpec((K, tn), lambda i, j: (0, j)),
                  pl.BlockSpec((tm, tn), lambda i, j: (i, gs_blk + j)),
                  pl.BlockSpec((tm, tn), lambda i, j: (i, ga_blk + j))],
        out_specs=pl.BlockSpec((tm, tn), lambda i, j: (i, j)),
        out_shape=jax.ShapeDtypeStruct((M, N), BF16),
        compiler_params=_cparams(("parallel", "arbitrary")),
    )(y_ssd, y_att, w_ssd, w_att, u_gate, u_gate)


def _layer_norm_rows(y, g, b):
    mu = jnp.mean(y, axis=-1, keepdims=True)
    yc = y - mu
    var = jnp.mean(yc * yc, axis=-1, keepdims=True)
    return yc * lax.rsqrt(var + LN_EPS) * g + b


def _mm_res_ln_kernel(x_ref, w_ref, r_ref, g_ref, b_ref, o_ref, ob_ref, acc_s, *, scale, nk):
    k = pl.program_id(1)

    @pl.when(k == 0)
    def _():
        acc_s[...] = jnp.zeros_like(acc_s)

    acc_s[...] += _dot(x_ref[...], w_ref[...])

    @pl.when(k == nk - 1)
    def _():
        o = _layer_norm_rows(ALPHA * r_ref[...] + scale * acc_s[...], g_ref[...], b_ref[...])
        o_ref[...] = o
        ob_ref[...] = o.astype(BF16)


def _matmul_res_ln(x, w, res, g, b, *, scale):
    M, K = x.shape
    D = w.shape[1]
    tm = max(r for r in range(16, MRL_ROWS + 1, 16) if M % r == 0)
    tk = MRL_K
    nk = K // tk
    row = pl.BlockSpec((tm, D), lambda i, k: (i, 0), pipeline_mode=pl.Buffered(1))
    vec = pl.BlockSpec((1, D), lambda i, k: (0, 0))
    return pl.pallas_call(
        functools.partial(_mm_res_ln_kernel, scale=scale, nk=nk),
        grid=(M // tm, nk),
        in_specs=[pl.BlockSpec((tm, tk), lambda i, k: (i, k)),
                  pl.BlockSpec((tk, D), lambda i, k: (k, 0)),
                  pl.BlockSpec((tm, D), lambda i, k: (i, 0), pipeline_mode=pl.Buffered(1)),
                  vec, vec],
        out_specs=[row, row],
        out_shape=[jax.ShapeDtypeStruct((M, D), F32), jax.ShapeDtypeStruct((M, D), BF16)],
        scratch_shapes=[pltpu.VMEM((tm, D), F32)],
        compiler_params=_cparams(("parallel", "arbitrary")),
    )(x, w, res, g.reshape(1, D), b.reshape(1, D))


def _ln_kernel(x_ref, d_ref, g_ref, b_ref, *o_refs, scale):
    o = _layer_norm_rows(ALPHA * x_ref[...] + scale * d_ref[...], g_ref[...], b_ref[...])
    for o_ref in o_refs:
        o_ref[...] = o.astype(o_ref.dtype)


def _res_layer_norm(x, delta, g, b, *, scale, n_rows, x_row0=0, d_row0=0, out_rows=None, out_row0=0,
                    stacked=None, emit_bf16=True):
    D = x.shape[1]
    out_rows = n_rows if out_rows is None else out_rows
    if stacked is not None:
        out_rows = stacked[0].shape[0]
    tr = max(r for r in range(8, LN_ROWS + 1, 8)
             if all(v % r == 0 for v in (n_rows, x_row0, d_row0, out_row0)))
    xb, db, ob = x_row0 // tr, d_row0 // tr, out_row0 // tr
    vec = pl.BlockSpec((1, D), lambda i: (0, 0))
    dts = (F32, BF16) if emit_bf16 else (F32,)
    call = _stacked_call(
        functools.partial(_ln_kernel, scale=scale),
        grid=(n_rows // tr,),
        in_specs=[pl.BlockSpec((tr, D), lambda i: (xb + i, 0)), pl.BlockSpec((tr, D), lambda i: (db + i, 0)), vec, vec],
        out_specs=[pl.BlockSpec((tr, D), lambda i: (ob + i, 0)) for _ in dts],
        out_shape=[jax.ShapeDtypeStruct((out_rows, D), dt) for dt in dts],
        stacked=stacked, n_stacked=len(dts), sem=("parallel",))
    return call(x, delta, g.reshape(1, D), b.reshape(1, D))


def _memattn_kernel(q_ref, k_ref, v_ref, o_ref):
    dh = q_ref.shape[1] // MEM_HEADS
    scale = dh ** -0.5
    for h in range(MEM_HEADS):
        sl = slice(h * dh, (h + 1) * dh)
        q = q_ref[:, sl]
        k = k_ref[:, sl].astype(BF16)
        v = v_ref[:, sl].astype(BF16)
        s = _dot_nt(q, k) * scale
        m = jnp.max(s, axis=-1, keepdims=True)
        p = jnp.exp(s - m)
        p = p / jnp.sum(p, axis=-1, keepdims=True)
        o_ref[:, sl] = _dot(p.astype(BF16), v).astype(o_ref.dtype)


def _memory_attention(q, mem_k, mem_v, *, n_batch, rows_per_batch, row0, tq, stacked=None):
    M, D = q.shape
    mt = mem_k.shape[0] // n_batch
    nq = rows_per_batch // tq
    blk0 = row0 // tq
    qspec = pl.BlockSpec((tq, D), lambda b, j: (blk0 + b * nq + j, 0))
    mspec = pl.BlockSpec((mt, D), lambda b, j: (b, 0))
    call = _stacked_call(
        _memattn_kernel, grid=(n_batch, nq), in_specs=[qspec, mspec, mspec], out_specs=[qspec],
        out_shape=[jax.ShapeDtypeStruct((M, D), BF16)], stacked=stacked, n_stacked=1,
        sem=("parallel", "arbitrary"))
    return call(q, mem_k, mem_v)[0]


def _ssd_kernel(z_ref, xs_ref, bc_ref, us_ref, dtT_ref, cprev_ref, h0_ref,
                convw_ref, convb_ref, dtb_ref, alog_ref, dtbT_ref, alogT_ref, dskip_ref, ng_ref,
                expP_ref, expT_ref,
                y_ref, cnew_ref, h_ref, extx_s, extbc_s, *, T, n_heads):
    P, R, N = SSD_HEAD_DIM, SSD_HEADS_PER_GROUP, SSD_STATE
    G = n_heads // R
    DI = n_heads * P
    GN = G * N
    c = pl.program_id(1)
    W1 = SSD_CONV_W - 1
    base = 8 - W1

    @pl.when(c == 0)
    def _():
        extx_s[0:base, :] = jnp.zeros((base, DI), F32)
        extbc_s[0:base, :] = jnp.zeros((base, 2 * GN), F32)
        extx_s[base:8, :] = cprev_ref[0, :, :DI]
        extbc_s[base:8, :] = cprev_ref[0, :, DI:]
        h_ref[0] = h0_ref[0]

    def conv(hist_s, x, lo, hi):
        xx = jnp.concatenate([hist_s[...], x], axis=0)
        out = convb_ref[:, lo:hi]
        for j in range(SSD_CONV_W):
            tap = xx if j == W1 else pltpu.roll(xx, W1 - j, axis=0)
            out = out + convw_ref[j:j + 1, lo:hi] * tap[8:8 + T]
        return _silu(out)

    x_in = xs_ref[...]
    bc_in = bc_ref[...]
    xs = conv(extx_s, x_in, 0, DI)
    bc = conv(extbc_s, bc_in, DI, DI + 2 * GN)
    newx = x_in[T - W1:T]
    newbc = bc_in[T - W1:T]
    cnew_ref[0, :, :DI] = newx
    cnew_ref[0, :, DI:] = newbc
    extx_s[base:8, :] = newx
    extbc_s[base:8, :] = newbc

    def expand(vals, e_ref):
        pieces = []
        for v in vals:
            for _ in range(3):
                p = v.astype(BF16)
                pieces.append(p)
                v = v - p.astype(F32)
        y = _dot(jnp.concatenate(pieces, axis=0), e_ref[...])
        return [y[(3 * n) * T:(3 * n + 1) * T] + y[(3 * n + 1) * T:(3 * n + 2) * T] + y[(3 * n + 2) * T:(3 * n + 3) * T]
                for n in range(len(vals))]

    dt = _softplus(us_ref[:, US_DT:US_DT + n_heads] + dtb_ref[...])
    d_a = dt * (-jnp.exp(alog_ref[...]))
    row = lax.broadcasted_iota(I32, (T, T), 0)
    col = lax.broadcasted_iota(I32, (T, T), 1)
    tri = (row >= col).astype(F32)
    a_cum = _dot_hi(tri, d_a)
    a_exp, dt_exp = expand([a_cum, dt], expP_ref)
    a_expt = a_exp if T == P else expand([a_cum], expT_ref)[0]
    d_a_t = _softplus(dtT_ref[0] + dtbT_ref[...]) * (-jnp.exp(alogT_ref[...]))
    r2 = lax.broadcasted_iota(I32, (2 * T, 2 * T), 0)
    c2 = lax.broadcasted_iota(I32, (2 * T, 2 * T), 1)
    tri2 = ((r2 // T == c2 // T) & (r2 <= c2)).astype(F32)
    a_cum_t = _dot_hi(d_a_t, tri2)

    l_idx = lax.broadcasted_iota(I32, (T, 2 * T), 0)
    j_idx = lax.broadcasted_iota(I32, (T, 2 * T), 1)
    causal2 = l_idx >= (j_idx % T)
    rr = lax.broadcasted_iota(I32, (2 * T, 2 * P), 0)
    cc = lax.broadcasted_iota(I32, (2 * T, 2 * P), 1)
    blockdiag = (rr // T) == (cc // P)
    chunk_decay = jnp.exp(a_cum_t)

    xdt = xs * dt_exp
    for g in range(G):
        gs = slice(g * R * P, (g + 1) * R * P)
        b_g = bc[:, g * N:(g + 1) * N].astype(BF16)
        c_g = bc[:, GN + g * N:GN + (g + 1) * N].astype(BF16)
        cb2 = _dot_nt(c_g, jnp.concatenate([b_g, b_g], axis=0))
        h_g = h_ref[0, gs, :]
        y_off = _dot_nt(c_g, h_g.astype(BF16)) * jnp.exp(a_exp[:, gs])
        pairs = []
        for pr in range(R // 2):
            i = g * (R // 2) + pr
            seg = a_expt[:, i * 2 * T:(i + 1) * 2 * T] - a_cum_t[i:i + 1, :]
            decay = jnp.exp(jnp.where(causal2, seg, -jnp.inf))
            m_pair = (cb2 * decay).astype(BF16)
            x_pair = xdt[:, i * 2 * P:(i + 1) * 2 * P]
            rhs = jnp.where(blockdiag, jnp.concatenate([x_pair, x_pair], axis=0), 0.0).astype(BF16)
            pairs.append(_dot(m_pair, rhs))
        y_g = jnp.concatenate(pairs, axis=1) + y_off + dskip_ref[:, gs] * xs[:, gs]
        a_g = a_exp[:, gs]
        dte = jnp.exp(a_g[T - 1:T, :] - a_g)
        st = _dot_tn((xdt[:, gs] * dte).astype(BF16), b_g)
        dec = jnp.concatenate(
            [jnp.broadcast_to(chunk_decay[(g * R + r) // 2:(g * R + r) // 2 + 1,
                                          ((g * R + r) % 2) * T + T - 1:((g * R + r) % 2) * T + T], (P, N))
             for r in range(R)], axis=0)
        h_ref[0, gs, :] = dec * h_g + st
        yg = y_g * _silu(z_ref[:, gs])
        ms = jnp.mean(yg * yg, axis=-1, keepdims=True)
        y_ref[:, gs] = (yg * lax.rsqrt(ms + RMS_EPS) * ng_ref[:, gs]).astype(y_ref.dtype)


def _ssd_mixer(u_ssd, u_small, dt_t, conv_prev, h0, prm, cols, *, n_batch, seq, row0, T, stacked=None):
    M = u_ssd.shape[0]
    n_heads = prm['dtb'].shape[1]
    DI = n_heads * SSD_HEAD_DIM
    BCW = conv_prev.shape[2] - DI
    nc = seq // T
    blk0 = row0 // T

    def rows(width, colblk):
        return pl.BlockSpec((T, width), lambda b, c: (blk0 + b * nc + c, colblk))

    def full(a):
        return pl.BlockSpec(a.shape, lambda b, c: (0,) * a.ndim)

    params = [prm['conv_w'], prm['conv_b'], prm['dtb'], prm['alog'], prm['dtbT'], prm['alogT'],
              prm['dskip'], prm['ng'], prm['expP'], prm['expT']]
    in_specs = [rows(DI, cols['z'] // DI), rows(DI, cols['xs'] // DI), rows(BCW, cols['bc'] // BCW),
                rows(US_W, 0),
                pl.BlockSpec((1,) + dt_t.shape[1:], lambda b, c: (b * nc + c, 0, 0)),
                pl.BlockSpec((1,) + conv_prev.shape[1:], lambda b, c: (b, 0, 0)),
                pl.BlockSpec((1,) + h0.shape[1:], lambda b, c: (b, 0, 0))] + [full(a) for a in params]
    out_specs = [rows(DI, 0),
                 pl.BlockSpec((1,) + conv_prev.shape[1:], lambda b, c: (b, 0, 0)),
                 pl.BlockSpec((1,) + h0.shape[1:], lambda b, c: (b, 0, 0))]
    call = _stacked_call(
        functools.partial(_ssd_kernel, T=T, n_heads=n_heads),
        grid=(n_batch, nc), in_specs=in_specs, out_specs=out_specs,
        out_shape=[jax.ShapeDtypeStruct((M, DI), BF16),
                   jax.ShapeDtypeStruct(conv_prev.shape, F32),
                   jax.ShapeDtypeStruct(h0.shape, F32)],
        stacked=stacked, n_stacked=1, sem=("parallel", "arbitrary"),
        scratch_shapes=[pltpu.VMEM((8, DI), F32), pltpu.VMEM((8, BCW), F32)])
    return call(u_ssd, u_ssd, u_ssd, u_small, dt_t, conv_prev, h0, *params)


def _sortable_key(x):
    b = lax.bitcast_convert_type(x, I32)
    return b ^ ((b >> 31) & 0x7FFFFFFF)


def _head_weights(us_ref):
    return us_ref[:, US_WI:US_WI + IDX_HEADS] * ((IDX_HEADS * IDX_DIM) ** -0.5)


def _indexer_scores(qi_heads, wi, ki_blk):
    Q = qi_heads[0].shape[0]
    acc = None
    for h0 in range(0, IDX_HEADS, IDX_STACK):
        l = _dot_nt(jnp.concatenate(qi_heads[h0:h0 + IDX_STACK], axis=0), ki_blk)
        for n in range(IDX_STACK):
            t = jnp.maximum(l[n * Q:(n + 1) * Q], 0.0) * wi[:, h0 + n:h0 + n + 1]
            acc = t if acc is None else acc + t
    return acc


def _kth_largest(count_ge, shape, k):
    c0 = count_ge(jnp.zeros(shape, I32))
    t0 = jnp.where(c0 >= k, 0, INT_MIN).astype(I32)

    def body(i, t):
        cand = t | (jnp.int32(1) << (30 - i))
        return jnp.where(count_ge(cand) >= k, cand, t)

    return lax.fori_loop(0, 31, body, t0)


def _limits(pos0, n_rows):
    pos = pos0 + lax.broadcasted_iota(I32, (n_rows, 1), 0)
    return (pos // CHUNK + 1) * CHUNK


def _stack_q_heads(q_ref, g):
    parts = [q_ref[:, (g * Q_PER_KV + i) * HEAD_DIM:(g * Q_PER_KV + i + 1) * HEAD_DIM] for i in range(Q_PER_KV)]
    return (jnp.concatenate(parts, axis=0) * (HEAD_DIM ** -0.5 * LOG2E)).astype(BF16)


def _dsa_prompt_kernel(q_ref, qi_ref, us_ref, ki_ref, k_ref, vt_ref, o_ref, key_s, bias_s, *, n_sel, kv_heads):
    QB, KB = DSA_QB, DSA_KB
    j = pl.program_id(1)
    nkb = ((j + 1) * QB + KB - 1) // KB
    wi = _head_weights(us_ref)
    qi_heads = [qi_ref[:, h * IDX_DIM:(h + 1) * IDX_DIM].astype(BF16) for h in range(IDX_HEADS)]
    pos = j * QB + lax.broadcasted_iota(I32, (1, QB), 1)
    lim = (pos // CHUNK + 1) * CHUNK

    def score_blk(kb, carry):
        off = pl.multiple_of(kb * KB, KB)
        sc = _indexer_scores(qi_heads, wi, ki_ref[pl.ds(off, KB), :])
        s_idx = off + lax.broadcasted_iota(I32, (KB, QB), 0)
        key_s[kb] = jnp.where(s_idx < lim, _sortable_key(sc.T), INT_MIN)
        return carry

    lax.fori_loop(0, nkb, score_blk, 0)

    def count_ge(cand):
        def body(kb, acc):
            m = (key_s[kb] >= cand).astype(I32)
            return acc + jnp.sum(m.reshape(KB // 8, 8, QB), axis=0)
        acc = lax.fori_loop(0, nkb, body, jnp.zeros((8, QB), I32))
        return jnp.sum(acc, axis=0, keepdims=True)

    thr = _kth_largest(count_ge, (1, QB), n_sel)

    def bias_blk(kb, carry):
        k = key_s[kb]
        bias_s[kb] = jnp.where((k >= thr) & (k > INT_MIN), 0.0, NEG)
        return carry

    lax.fori_loop(0, nkb, bias_blk, 0)

    cols = Q_PER_KV * QB
    GPL = DSA_GROUPS_PER_LOOP
    for g0 in range(0, kv_heads, GPL):
        groups = range(g0, g0 + GPL)
        qs = [_stack_q_heads(q_ref, g) for g in groups]

        def body(kb, carry):
            off = pl.multiple_of(kb * KB, KB)
            b = bias_s[kb]
            b4 = jnp.concatenate([b] * Q_PER_KV, axis=1)
            heads = [slice(g * HEAD_DIM, (g + 1) * HEAD_DIM) for g in groups]

            def logits(n):
                return _dot_nt(k_ref[pl.ds(off, KB), heads[n]], qs[n]) + b4

            t_next = logits(0)
            new = []
            for n in range(GPL):
                t = t_next
                if n + 1 < GPL:
                    t_next = logits(n + 1)
                m, l, acc = carry[n]
                m_new = jnp.maximum(m, jnp.max(t, axis=0, keepdims=True))
                alpha = jnp.exp2(m - m_new)
                p = jnp.exp2(t - m_new)
                l = alpha * l + jnp.sum(p, axis=0, keepdims=True)
                acc = alpha * acc + _dot(vt_ref[0, kb, heads[n], :], p.astype(BF16))
                new.append((m_new, l, acc))
            return tuple(new)

        init = tuple((jnp.full((1, cols), NEG, F32), jnp.zeros((1, cols), F32), jnp.zeros((HEAD_DIM, cols), F32))
                     for _ in groups)
        res = lax.fori_loop(0, nkb, body, init)
        for n, g in enumerate(groups):
            _, l, acc = res[n]
            out = acc / l
            for i in range(Q_PER_KV):
                o_ref[:, (g * Q_PER_KV + i) * HEAD_DIM:(g * Q_PER_KV + i + 1) * HEAD_DIM] = (
                    out[:, i * QB:(i + 1) * QB].T.astype(o_ref.dtype))


def _dsa_prompt(u_att, u_small, ki_b, k_b, vt_b, cols, *, n_batch, seq, kv_heads):
    M = u_att.shape[0]
    QB, KB = DSA_QB, DSA_KB
    nq = seq // QB
    nkb_max = seq // KB
    DQ = kv_heads * Q_PER_KV * HEAD_DIM
    DK = kv_heads * HEAD_DIM
    DQI = IDX_HEADS * IDX_DIM
    n_sel = min(TOPK_MAX, seq // 4)
    return pl.pallas_call(
        functools.partial(_dsa_prompt_kernel, n_sel=n_sel, kv_heads=kv_heads),
        grid=(n_batch, nq),
        in_specs=[pl.BlockSpec((QB, DQ), lambda b, j: (b * nq + j, cols['q'] // DQ)),
                  pl.BlockSpec((QB, DQI), lambda b, j: (b * nq + j, cols['qi'] // DQI)),
                  pl.BlockSpec((QB, US_W), lambda b, j: (b * nq + j, 0)),
                  pl.BlockSpec((seq, IDX_DIM), lambda b, j: (b, 0)),
                  pl.BlockSpec((seq, DK), lambda b, j: (b, 0)),
                  pl.BlockSpec((1, nkb_max, DK, KB), lambda b, j: (b, 0, 0, 0))],
        out_specs=pl.BlockSpec((QB, DQ), lambda b, j: (b * nq + j, 0)),
        out_shape=jax.ShapeDtypeStruct((M, DQ), BF16),
        scratch_shapes=[pltpu.VMEM((nkb_max, KB, QB), I32), pltpu.VMEM((nkb_max, KB, QB), F32)],
        compiler_params=_cparams(("parallel", "arbitrary")),
    )(u_att, u_att, u_small, ki_b, k_b, vt_b)


def _dsa_sample_kernel(q_ref, qi_ref, us_ref, kn_ref, vn_ref, ck_ref, cv_ref, cki_ref, o_ref, *,
                       n_sel, kv_heads, past):
    Q = q_ref.shape[0]
    wi = _head_weights(us_ref)
    qi_heads = [qi_ref[:, h * IDX_DIM:(h + 1) * IDX_DIM].astype(BF16) for h in range(IDX_HEADS)]
    lim = _limits(past, Q)
    ki_new = us_ref[:, US_KI:US_KI + IDX_DIM].astype(BF16)
    sc_p = _indexer_scores(qi_heads, wi, cki_ref[0].astype(BF16))
    sc_n = _indexer_scores(qi_heads, wi, ki_new)
    idx_p = lax.broadcasted_iota(I32, (Q, past), 1)
    idx_n = past + lax.broadcasted_iota(I32, (Q, Q), 1)
    key_p = jnp.where(idx_p < lim, _sortable_key(sc_p), INT_MIN)
    key_n = jnp.where(idx_n < lim, _sortable_key(sc_n), INT_MIN)

    def count_ge(cand):
        return (jnp.sum((key_p >= cand).astype(I32), axis=1, keepdims=True)
                + jnp.sum((key_n >= cand).astype(I32), axis=1, keepdims=True))

    thr = _kth_largest(count_ge, (Q, 1), n_sel)
    bias_p = jnp.where((key_p >= thr) & (key_p > INT_MIN), 0.0, NEG)
    bias_n = jnp.where((key_n >= thr) & (key_n > INT_MIN), 0.0, NEG)
    bias_p = jnp.concatenate([bias_p] * Q_PER_KV, axis=0)
    bias_n = jnp.concatenate([bias_n] * Q_PER_KV, axis=0)

    for g in range(kv_heads):
        hs = slice(g * HEAD_DIM, (g + 1) * HEAD_DIM)
        qs = _stack_q_heads(q_ref, g)
        ck = ck_ref[0, pl.ds(g, past, stride=kv_heads), :].astype(BF16)
        cv = cv_ref[0, pl.ds(g, past, stride=kv_heads), :].astype(BF16)
        t_p = _dot_nt(qs, ck) + bias_p
        t_n = _dot_nt(qs, kn_ref[:, hs].astype(BF16)) + bias_n
        m = jnp.maximum(jnp.max(t_p, axis=-1, keepdims=True), jnp.max(t_n, axis=-1, keepdims=True))
        p_p = jnp.exp2(t_p - m)
        p_n = jnp.exp2(t_n - m)
        l = jnp.sum(p_p, axis=-1, keepdims=True) + jnp.sum(p_n, axis=-1, keepdims=True)
        acc = _dot(p_p.astype(BF16), cv) + _dot(p_n.astype(BF16), vn_ref[:, hs].astype(BF16))
        out = acc / l
        for i in range(Q_PER_KV):
            o_ref[:, (g * Q_PER_KV + i) * HEAD_DIM:(g * Q_PER_KV + i + 1) * HEAD_DIM] = (
                out[i * Q:(i + 1) * Q].astype(o_ref.dtype))


def _dsa_sample(u_att, u_small, cache_k, cache_v, cache_ki, cols, *, n_batch, seq, row0, kv_heads, stacked):
    past = cache_ki.shape[1]
    DQ = kv_heads * Q_PER_KV * HEAD_DIM
    DK = kv_heads * HEAD_DIM
    DQI = IDX_HEADS * IDX_DIM
    n_sel = min(TOPK_MAX, (past + seq) // 4)
    blk0 = row0 // seq
    call = _stacked_call(
        functools.partial(_dsa_sample_kernel, n_sel=n_sel, kv_heads=kv_heads, past=past),
        grid=(n_batch,),
        in_specs=[pl.BlockSpec((seq, DQ), lambda b: (blk0 + b, cols['q'] // DQ)),
                  pl.BlockSpec((seq, DQI), lambda b: (blk0 + b, cols['qi'] // DQI)),
                  pl.BlockSpec((seq, US_W), lambda b: (blk0 + b, 0)),
                  pl.BlockSpec((seq, DK), lambda b: (blk0 + b, cols['k'] // DK)),
                  pl.BlockSpec((seq, DK), lambda b: (blk0 + b, cols['v'] // DK)),
                  pl.BlockSpec((1, past * kv_heads, HEAD_DIM), lambda b: (b, 0, 0)),
                  pl.BlockSpec((1, past * kv_heads, HEAD_DIM), lambda b: (b, 0, 0)),
                  pl.BlockSpec((1, past, IDX_DIM), lambda b: (b, 0, 0))],
        out_specs=[pl.BlockSpec((seq, DQ), lambda b: (blk0 + b, 0))],
        out_shape=[jax.ShapeDtypeStruct(stacked[0].shape, BF16)],
        stacked=stacked, n_stacked=1, sem=("parallel",))
    return call(u_att, u_att, u_small, u_att, u_att, cache_k, cache_v, cache_ki)[0]


def _pair_layout(v, T):
    return jnp.repeat(v.reshape(-1, 2), T, axis=1)


def _dt_transposed(dt_raw, T):
    n, H = dt_raw.shape
    return dt_raw.reshape(n // T, T, H // 2, 2).transpose(0, 2, 3, 1).reshape(n // T, H // 2, 2 * T)


def _ssd_params(conv_w, conv_b, dt_bias, a_log, d_skip, norm_g, T):
    H = dt_bias.shape[0]
    P = SSD_HEAD_DIM
    eye = jnp.eye(H, dtype=F32)
    return dict(conv_w=conv_w, conv_b=conv_b.reshape(1, -1),
                dtb=dt_bias.reshape(1, H), alog=a_log.reshape(1, H),
                dtbT=_pair_layout(dt_bias, T), alogT=_pair_layout(a_log, T),
                dskip=jnp.repeat(d_skip, P).reshape(1, H * P), ng=norm_g.reshape(1, -1),
                expP=jnp.repeat(eye, P, axis=1).astype(BF16), expT=jnp.repeat(eye, T, axis=1).astype(BF16))


def kernel(x_prompt, x_sample, mem_prompt, cache_k, cache_v, cache_idx_k, cache_mem_k, cache_mem_v,
           state_ssm, state_conv, ln1_g, ln1_b, ffn1_w1, ffn1_w3, ffn1_w2, w_in, conv_w, conv_b,
           dt_bias, a_log, d_skip, ssd_norm_g, w_ssd_br, w_att_br, w_out, ln2_g, ln2_b,
           w_mq, w_mk, w_mv, w_mo, ln3_g, ln3_b, ffn2_w1, ffn2_w3, ffn2_w2, ln4_g, ln4_b):
    assert x_prompt.ndim == 3 and ln1_g.shape[0] == DEPTH == 1
    l = 0
    BP, LP, D = x_prompt.shape
    BS, LS, _ = x_sample.shape
    MP, MS = BP * LP, BS * LS
    M = MP + MS
    H = dt_bias.shape[1]
    assert H == SSD_HEADS
    DI = H * SSD_HEAD_DIM
    BCW = conv_w.shape[2] - DI
    KVH = cache_k.shape[3]
    DQ = KVH * Q_PER_KV * HEAD_DIM
    DK = KVH * HEAD_DIM
    DQI = IDX_HEADS * IDX_DIM
    MT = mem_prompt.shape[1]
    PAST = cache_k.shape[2]
    tm_big = M // 8
    tm_half = M // 16

    w0t = w_in[l].T
    c_dt = 2 * DI + BCW
    c_q = c_dt + H
    c_wi = c_q + DQ + 2 * DK + DQI
    c_ki = c_wi + IDX_HEADS
    c_g = c_ki + IDX_DIM
    assert c_g + 2 * D == w0t.shape[0]
    cols_ssd = dict(z=0, xs=DI, bc=2 * DI)
    cols_att = dict(q=0, k=DQ, v=DQ + DK, qi=DQ + 2 * DK)
    n_small = H + IDX_HEADS + IDX_DIM
    assert n_small <= US_W
    w_small_t = jnp.concatenate([w0t[c_dt:c_q], w0t[c_wi:c_g], jnp.zeros((US_W - n_small, D), F32)], axis=0)

    x_p2, x_s2 = x_prompt.reshape(MP, D), x_sample.reshape(MS, D)
    x0b = _cast_rows(x_p2, out_rows=M, out_row0=0)
    x0b, = _cast_rows(x_s2, out_rows=M, out_row0=MP, stacked=x0b)

    def ffn_delta(xb, w1, w3, w2):
        h = _ffn_up(xb, w1[l], w3[l], tm=M // 4, tn=256)
        return _matmul(h, w2[l].astype(BF16), tm=tm_big, tn=256, out_dtype=F32, single_buffer_x=True)

    d1 = ffn_delta(x0b, ffn1_w1, ffn1_w3, ffn1_w2)
    ln1 = _res_layer_norm(x_p2, d1, ln1_g[l], ln1_b[l], scale=0.5, n_rows=MP, out_rows=M)
    x1, x1b = _res_layer_norm(x_s2, d1, ln1_g[l], ln1_b[l], scale=0.5, n_rows=MS, d_row0=MP, out_row0=MP,
                              stacked=ln1)

    u_ssd = _matmul_nt(x1b, w0t, row0=0, n_rows=c_dt, tm=tm_big, tn=512, out_dtype=F32)
    u_att = _matmul_nt(x1b, w0t, row0=c_q, n_rows=c_wi - c_q, tm=tm_big, tn=512, out_dtype=F32)
    u_gate = _matmul_nt(x1b, w0t, row0=c_g, n_rows=2 * D, tm=tm_big, tn=512, out_dtype=F32)
    u_small = _matmul_nt(x1b, w_small_t, row0=0, n_rows=US_W, tm=tm_big, tn=US_W, out_dtype=F32)

    dt_raw = u_small[:, US_DT:US_DT + H]
    zeros_conv = jnp.zeros((BP,) + state_conv.shape[2:], F32)
    zeros_h = jnp.zeros((BP, DI, SSD_STATE), F32)
    prm_p = _ssd_params(conv_w[l], conv_b[l], dt_bias[l], a_log[l], d_skip[l], ssd_norm_g[l], CHUNK)
    y_ssd, conv_p, h_p = _ssd_mixer(u_ssd, u_small, _dt_transposed(dt_raw[:MP], CHUNK), zeros_conv, zeros_h,
                                    prm_p, cols_ssd, n_batch=BP, seq=LP, row0=0, T=CHUNK)
    TS = min(CHUNK, LS)
    prm_s = _ssd_params(conv_w[l], conv_b[l], dt_bias[l], a_log[l], d_skip[l], ssd_norm_g[l], TS)
    y_ssd, conv_s, h_s = _ssd_mixer(u_ssd, u_small, _dt_transposed(dt_raw[MP:], TS), state_conv[l],
                                    state_ssm[l].reshape(BS, DI, SSD_STATE),
                                    prm_s, cols_ssd, n_batch=BS, seq=LS, row0=MP, T=TS, stacked=(y_ssd,))

    k_all = u_att[:, cols_att['k']:cols_att['k'] + DK]
    v_all = u_att[:, cols_att['v']:cols_att['v'] + DK]
    ki_all = u_small[:, US_KI:US_KI + IDX_DIM]
    vt_b = v_all[:MP].astype(BF16).reshape(BP, LP // DSA_KB, DSA_KB, DK).transpose(0, 1, 3, 2)
    y_att = _dsa_prompt(u_att, u_small, ki_all[:MP].astype(BF16), k_all[:MP].astype(BF16),
                        vt_b, cols_att, n_batch=BP, seq=LP, kv_heads=KVH)
    y_att = _dsa_sample(u_att, u_small, cache_k[l].reshape(BS, PAST * KVH, HEAD_DIM),
                        cache_v[l].reshape(BS, PAST * KVH, HEAD_DIM), cache_idx_k[l], cols_att,
                        n_batch=BS, seq=LS, row0=MP, kv_heads=KVH, stacked=(y_att,))

    merged = _gated_merge(y_ssd, y_att, w_ssd_br[l], w_att_br[l], u_gate, 0, D, tm=tm_big, tn=256)
    x2, x2b = _matmul_res_ln(merged, w_out[l].astype(BF16), x1, ln2_g[l], ln2_b[l], scale=1.0)

    memb = mem_prompt.reshape(BP * MT, D).astype(BF16)
    mk_p = _matmul(memb, w_mk[l], tm=BP * MT, tn=512, out_dtype=F32)
    mv_p = _matmul(memb, w_mv[l], tm=BP * MT, tn=512, out_dtype=F32)
    qm = _matmul(x2b, w_mq[l], tm=tm_big, tn=512, out_dtype=BF16)
    o_m = _memory_attention(qm, mk_p, mv_p, n_batch=BP, rows_per_batch=LP, row0=0, tq=min(512, LP))
    o_m = _memory_attention(qm, cache_mem_k[l].reshape(BS * MT, D), cache_mem_v[l].reshape(BS * MT, D),
                            n_batch=BS, rows_per_batch=LS, row0=MP, tq=LS, stacked=(o_m,))
    x3, x3b = _matmul_res_ln(o_m, w_mo[l].astype(BF16), x2, ln3_g[l], ln3_b[l], scale=1.0)

    d4 = ffn_delta(x3b, ffn2_w1, ffn2_w3, ffn2_w2)
    y_p, = _res_layer_norm(x3, d4, ln4_g[l], ln4_b[l], scale=0.5, n_rows=MP, emit_bf16=False)
    y_s, = _res_layer_norm(x3, d4, ln4_g[l], ln4_b[l], scale=0.5, n_rows=MS, x_row0=MP, d_row0=MP, emit_bf16=False)

    mh = D // MEM_HEADS
    return (y_p.reshape(BP, LP, D), y_s.reshape(BS, LS, D),
            h_p.reshape(1, BP, H, SSD_HEAD_DIM, SSD_STATE), conv_p[None],
            k_all[:MP].reshape(1, BP, LP, KVH, HEAD_DIM), v_all[:MP].reshape(1, BP, LP, KVH, HEAD_DIM),
            ki_all[:MP].reshape(1, BP, LP, IDX_DIM),
            mk_p.reshape(1, BP, MT, MEM_HEADS, mh), mv_p.reshape(1, BP, MT, MEM_HEADS, mh),
            h_s.reshape(1, BS, H, SSD_HEAD_DIM, SSD_STATE), conv_s[None],
            k_all[MP:].reshape(1, BS, LS, KVH, HEAD_DIM), v_all[MP:].reshape(1, BS, LS, KVH, HEAD_DIM),
            ki_all[MP:].reshape(1, BS, LS, IDX_DIM))
```

```python
import functools
import math

import jax
import jax.numpy as jnp
from jax import lax
from jax.experimental import pallas as pl
from jax.experimental.pallas import tpu as pltpu

F32 = jnp.float32
BF16 = jnp.bfloat16
I32 = jnp.int32

DEPTH = 1
CHUNK = 64
SSD_HEADS = 64
SSD_HEAD_DIM = 64
SSD_HEADS_PER_GROUP = 8
SSD_STATE = 128
SSD_CONV_W = 4
HEAD_DIM = 128
Q_PER_KV = 4
IDX_HEADS = 16
IDX_DIM = 64
TOPK_MAX = 256
MEM_HEADS = 4
ALPHA = (2.0 * DEPTH) ** 0.25
LN_EPS = 1e-5
RMS_EPS = 1e-5

LANES = 128
VMEM_LIMIT = 56 * 1024 * 1024

NEG = -1e30
INT_MIN = -2147483648
LOG2E = math.log2(math.e)

DSA_QB = 128
DSA_KB = 512
DSA_GROUPS_PER_LOOP = 8
IDX_STACK = 4
LN_ROWS = 256

US_DT = 0
US_WI = US_DT + SSD_HEADS
US_KI = US_WI + IDX_HEADS
US_W = 256


def _cparams(sem):
    return pltpu.CompilerParams(dimension_semantics=sem, vmem_limit_bytes=VMEM_LIMIT)


def _dot(a, b):
    return jnp.dot(a, b, preferred_element_type=F32)


def _dot_nt(a, b):
    return lax.dot_general(a, b, (((1,), (1,)), ((), ())), preferred_element_type=F32)


def _dot_tn(a, b, precision=None):
    return lax.dot_general(a, b, (((0,), (0,)), ((), ())), preferred_element_type=F32,
                           precision=precision)


def _dot_hi(a, b):
    return jnp.dot(a, b, preferred_element_type=F32, precision=lax.Precision.HIGHEST)


def _sigmoid(x):
    return 1.0 / (1.0 + jnp.exp(-x))


def _silu(x):
    return x * _sigmoid(x)


def _softplus(x):
    return jnp.maximum(x, 0.0) + jnp.log1p(jnp.exp(-jnp.abs(x)))


def _stacked_call(kernel, *, grid, in_specs, out_specs, out_shape, stacked, n_stacked, sem, scratch_shapes=()):
    if stacked is None:
        return pl.pallas_call(kernel, grid=grid, in_specs=in_specs, out_specs=out_specs, out_shape=out_shape,
                              scratch_shapes=scratch_shapes, compiler_params=_cparams(sem))
    n_in = len(in_specs)

    def body(*refs):
        kernel(*refs[:n_in], *refs[n_in + n_stacked:])

    call = pl.pallas_call(
        body, grid=grid,
        in_specs=list(in_specs) + [pl.BlockSpec(memory_space=pl.ANY)] * n_stacked,
        out_specs=out_specs, out_shape=out_shape, scratch_shapes=scratch_shapes,
        input_output_aliases={n_in + i: i for i in range(n_stacked)},
        compiler_params=_cparams(sem))
    return lambda *args: call(*args, *stacked)


def _mm_kernel(x_ref, w_ref, o_ref):
    o_ref[...] = _dot(x_ref[...], w_ref[...].astype(BF16)).astype(o_ref.dtype)


def _mm_res_kernel(x_ref, w_ref, r_ref, o_ref, *, scale):
    o_ref[...] = ALPHA * r_ref[...] + scale * _dot(x_ref[...], w_ref[...].astype(BF16))


def _matmul(x, w, *, tm, tn, out_dtype, single_buffer_x=False, residual=None, scale=1.0):
    M, K = x.shape
    N = w.shape[1]
    assert N % tn == 0 and M % tm == 0
    xmode = dict(pipeline_mode=pl.Buffered(1)) if single_buffer_x else {}
    tile = pl.BlockSpec((tm, tn), lambda i, j: (i, j))
    in_specs = [pl.BlockSpec((tm, K), lambda i, j: (i, 0), **xmode),
                pl.BlockSpec((K, tn), lambda i, j: (0, j))]
    if residual is None:
        body, args = _mm_kernel, (x, w)
    else:
        assert out_dtype == F32
        body, args = functools.partial(_mm_res_kernel, scale=scale), (x, w, residual)
        in_specs.append(tile)
    return pl.pallas_call(
        body,
        grid=(M // tm, N // tn),
        in_specs=in_specs,
        out_specs=tile,
        out_shape=jax.ShapeDtypeStruct((M, N), out_dtype),
        compiler_params=_cparams(("parallel", "arbitrary")),
    )(*args)


def _mm_nt_kernel(x_ref, wt_ref, o_ref):
    o_ref[...] = _dot_nt(x_ref[...], wt_ref[...].astype(BF16)).astype(o_ref.dtype)


def _matmul_nt(x, wt, *, row0, n_rows, tm, tn, out_dtype):
    M, K = x.shape
    assert n_rows % tn == 0 and M % tm == 0 and row0 % 8 == 0
    return pl.pallas_call(
        _mm_nt_kernel,
        grid=(M // tm, n_rows // tn),
        in_specs=[pl.BlockSpec((tm, K), lambda i, j: (i, 0)),
                  pl.BlockSpec((pl.Element(tn), pl.Element(K)), lambda i, j: (pl.multiple_of(row0 + j * tn, 8), 0))],
        out_specs=pl.BlockSpec((tm, tn), lambda i, j: (i, j)),
        out_shape=jax.ShapeDtypeStruct((M, n_rows), out_dtype),
        compiler_params=_cparams(("parallel", "arbitrary")),
    )(x, wt)


def _cast_kernel(x_ref, o_ref):
    o_ref[...] = x_ref[...].astype(o_ref.dtype)


def _cast_rows(x, *, out_rows, out_row0, stacked=None):
    n, D = x.shape
    tr = max(r for r in range(16, LN_ROWS + 1, 16) if n % r == 0 and out_row0 % r == 0)
    ob = out_row0 // tr
    call = _stacked_call(
        _cast_kernel, grid=(n // tr,),
        in_specs=[pl.BlockSpec((tr, D), lambda i: (i, 0))],
        out_specs=[pl.BlockSpec((tr, D), lambda i: (ob + i, 0))],
        out_shape=[jax.ShapeDtypeStruct((out_rows, D), BF16)],
        stacked=stacked, n_stacked=1, sem=("parallel",))
    return call(x)


def _ffn_up_kernel(x_ref, w1_ref, w3_ref, o_ref):
    x = x_ref[...]
    a = _dot(x, w1_ref[...].astype(BF16))
    b = _dot(x, w3_ref[...].astype(BF16))
    o_ref[...] = (_silu(a) * b).astype(o_ref.dtype)


def _ffn_up(x, w1, w3, *, tm, tn):
    M, K = x.shape
    N = w1.shape[1]
    return pl.pallas_call(
        _ffn_up_kernel,
        grid=(M // tm, N // tn),
        in_specs=[pl.BlockSpec((tm, K), lambda i, j: (i, 0), pipeline_mode=pl.Buffered(1)),
                  pl.BlockSpec((K, tn), lambda i, j: (0, j)),
                  pl.BlockSpec((K, tn), lambda i, j: (0, j))],
        out_specs=pl.BlockSpec((tm, tn), lambda i, j: (i, j)),
        out_shape=jax.ShapeDtypeStruct((M, N), BF16),
        compiler_params=_cparams(("parallel", "arbitrary")),
    )(x, w1, w3)


def _gate_kernel(ys_ref, ya_ref, ws_ref, wa_ref, gs_ref, ga_ref, o_ref):
    s = _dot(ys_ref[...], ws_ref[...].astype(BF16))
    a = _dot(ya_ref[...], wa_ref[...].astype(BF16))
    o_ref[...] = (_sigmoid(gs_ref[...]) * s + _sigmoid(ga_ref[...]) * a).astype(o_ref.dtype)


def _gated_merge(y_ssd, y_att, w_ssd, w_att, u_gate, gs_col, ga_col, *, tm, tn):
    M, K = y_ssd.shape
    N = w_ssd.shape[1]
    gs_blk, ga_blk = gs_col // tn, ga_col // tn
    once = dict(pipeline_mode=pl.Buffered(1))
    return pl.pallas_call(
        _gate_kernel,
        grid=(M // tm, N // tn),
        in_specs=[pl.BlockSpec((tm, K), lambda i, j: (i, 0), **once),
                  pl.BlockSpec((tm, K), lambda i, j: (i, 0), **once),
                  pl.BlockSpec((K, tn), lambda i, j: (0, j)),
                  pl.BlockSpec((K, tn), lambda i, j: (0, j)),
                  pl.BlockSpec((tm, tn), lambda i, j: (i, gs_blk + j)),
                  pl.BlockSpec((tm, tn), lambda i, j: (i, ga_blk + j))],
        out_specs=pl.BlockSpec((tm, tn), lambda i, j: (i, j)),
        out_shape=jax.ShapeDtypeStruct((M, N), BF16),
        compiler_params=_cparams(("parallel", "arbitrary")),
    )(y_ssd, y_att, w_ssd, w_att, u_gate, u_gate)


def _layer_norm_rows(y, g, b):
    mu = jnp.mean(y, axis=-1, keepdims=True)
    yc = y - mu
    var = jnp.mean(yc * yc, axis=-1, keepdims=True)
    return yc * lax.rsqrt(var + LN_EPS) * g + b


def _ln_kernel(x_ref, d_ref, g_ref, b_ref, *o_refs, scale):
    o = _layer_norm_rows(ALPHA * x_ref[...] + scale * d_ref[...], g_ref[...], b_ref[...])
    for o_ref in o_refs:
        o_ref[...] = o.astype(o_ref.dtype)


def _ln_presummed_kernel(y_ref, g_ref, b_ref, *o_refs):
    o = _layer_norm_rows(y_ref[...], g_ref[...], b_ref[...])
    for o_ref in o_refs:
        o_ref[...] = o.astype(o_ref.dtype)


def _res_layer_norm(x, delta, g, b, *, scale=None, n_rows, x_row0=0, d_row0=0, out_rows=None, out_row0=0,
                    stacked=None, emit_bf16=True):
    D = x.shape[1]
    out_rows = n_rows if out_rows is None else out_rows
    if stacked is not None:
        out_rows = stacked[0].shape[0]
    tr = max(r for r in range(8, LN_ROWS + 1, 8)
             if all(v % r == 0 for v in (n_rows, x_row0, d_row0, out_row0)))
    xb, db, ob = x_row0 // tr, d_row0 // tr, out_row0 // tr
    vec = pl.BlockSpec((1, D), lambda i: (0, 0))
    dts = (F32, BF16) if emit_bf16 else (F32,)
    if delta is None:
        body, args = _ln_presummed_kernel, (x,)
        in_specs = [pl.BlockSpec((tr, D), lambda i: (xb + i, 0)), vec, vec]
    else:
        body, args = functools.partial(_ln_kernel, scale=scale), (x, delta)
        in_specs = [pl.BlockSpec((tr, D), lambda i: (xb + i, 0)), pl.BlockSpec((tr, D), lambda i: (db + i, 0)), vec, vec]
    call = _stacked_call(
        body, grid=(n_rows // tr,), in_specs=in_specs,
        out_specs=[pl.BlockSpec((tr, D), lambda i: (ob + i, 0)) for _ in dts],
        out_shape=[jax.ShapeDtypeStruct((out_rows, D), dt) for dt in dts],
        stacked=stacked, n_stacked=len(dts), sem=("parallel",))
    return call(*args, g.reshape(1, D), b.reshape(1, D))


def _memattn_kernel(q_ref, k_ref, v_ref, o_ref):
    dh = q_ref.shape[1] // MEM_HEADS
    scale = dh ** -0.5
    for h in range(MEM_HEADS):
        sl = slice(h * dh, (h + 1) * dh)
        q = q_ref[:, sl]
        k = k_ref[:, sl].astype(BF16)
        v = v_ref[:, sl].astype(BF16)
        s = _dot_nt(q, k) * scale
        m = jnp.max(s, axis=-1, keepdims=True)
        p = jnp.exp(s - m)
        p = p / jnp.sum(p, axis=-1, keepdims=True)
        o_ref[:, sl] = _dot(p.astype(BF16), v).astype(o_ref.dtype)


def _memory_attention(q, mem_k, mem_v, *, n_batch, rows_per_batch, row0, tq, stacked=None):
    M, D = q.shape
    mt = mem_k.shape[0] // n_batch
    nq = rows_per_batch // tq
    blk0 = row0 // tq
    qspec = pl.BlockSpec((tq, D), lambda b, j: (blk0 + b * nq + j, 0))
    mspec = pl.BlockSpec((mt, D), lambda b, j: (b, 0))
    call = _stacked_call(
        _memattn_kernel, grid=(n_batch, nq), in_specs=[qspec, mspec, mspec], out_specs=[qspec],
        out_shape=[jax.ShapeDtypeStruct((M, D), BF16)], stacked=stacked, n_stacked=1,
        sem=("parallel", "arbitrary"))
    return call(q, mem_k, mem_v)[0]


def _ssd_kernel(z_ref, xs_ref, bc_ref, us_ref, dtT_ref, cprev_ref, h0_ref,
                convw_ref, convb_ref, dtb_ref, alog_ref, dtbT_ref, alogT_ref, dskip_ref, ng_ref,
                expP_ref, expT_ref,
                y_ref, cnew_ref, h_ref, extx_s, extbc_s, *, T, n_heads):
    P, R, N = SSD_HEAD_DIM, SSD_HEADS_PER_GROUP, SSD_STATE
    G = n_heads // R
    DI = n_heads * P
    GN = G * N
    c = pl.program_id(1)
    W1 = SSD_CONV_W - 1
    base = 8 - W1

    @pl.when(c == 0)
    def _():
        extx_s[0:base, :] = jnp.zeros((base, DI), F32)
        extbc_s[0:base, :] = jnp.zeros((base, 2 * GN), F32)
        extx_s[base:8, :] = cprev_ref[0, :, :DI]
        extbc_s[base:8, :] = cprev_ref[0, :, DI:]
        h_ref[0] = h0_ref[0]

    def conv(hist_s, x, lo, hi):
        xx = jnp.concatenate([hist_s[...], x], axis=0)
        out = convb_ref[:, lo:hi]
        for j in range(SSD_CONV_W):
            tap = xx if j == W1 else pltpu.roll(xx, W1 - j, axis=0)
            out = out + convw_ref[j:j + 1, lo:hi] * tap[8:8 + T]
        return _silu(out)

    x_in = xs_ref[...]
    bc_in = bc_ref[...]
    xs = conv(extx_s, x_in, 0, DI)
    bc = conv(extbc_s, bc_in, DI, DI + 2 * GN)
    newx = x_in[T - W1:T]
    newbc = bc_in[T - W1:T]
    cnew_ref[0, :, :DI] = newx
    cnew_ref[0, :, DI:] = newbc
    extx_s[base:8, :] = newx
    extbc_s[base:8, :] = newbc

    def expand(vals, e_ref):
        pieces = []
        for v in vals:
            for _ in range(3):
                p = v.astype(BF16)
                pieces.append(p)
                v = v - p.astype(F32)
        y = _dot(jnp.concatenate(pieces, axis=0), e_ref[...])
        return [y[(3 * n) * T:(3 * n + 1) * T] + y[(3 * n + 1) * T:(3 * n + 2) * T] + y[(3 * n + 2) * T:(3 * n + 3) * T]
                for n in range(len(vals))]

    dt = _softplus(us_ref[:, US_DT:US_DT + n_heads] + dtb_ref[...])
    d_a = dt * (-jnp.exp(alog_ref[...]))
    row = lax.broadcasted_iota(I32, (T, T), 0)
    col = lax.broadcasted_iota(I32, (T, T), 1)
    tri = (row >= col).astype(F32)
    a_cum = _dot_hi(tri, d_a)
    a_exp, dt_exp = expand([a_cum, dt], expP_ref)
    a_expt = a_exp if T == P else expand([a_cum], expT_ref)[0]
    d_a_t = _softplus(dtT_ref[0] + dtbT_ref[...]) * (-jnp.exp(alogT_ref[...]))
    r2 = lax.broadcasted_iota(I32, (2 * T, 2 * T), 0)
    c2 = lax.broadcasted_iota(I32, (2 * T, 2 * T), 1)
    tri2 = ((r2 // T == c2 // T) & (r2 <= c2)).astype(F32)
    a_cum_t = _dot_hi(d_a_t, tri2)

    l_idx = lax.broadcasted_iota(I32, (T, 2 * T), 0)
    j_idx = lax.broadcasted_iota(I32, (T, 2 * T), 1)
    causal2 = l_idx >= (j_idx % T)
    rr = lax.broadcasted_iota(I32, (2 * T, 2 * P), 0)
    cc = lax.broadcasted_iota(I32, (2 * T, 2 * P), 1)
    blockdiag = (rr // T) == (cc // P)
    chunk_decay = jnp.exp(a_cum_t)

    xdt = xs * dt_exp
    for g in range(G):
        gs = slice(g * R * P, (g + 1) * R * P)
        b_g = bc[:, g * N:(g + 1) * N].astype(BF16)
        c_g = bc[:, GN + g * N:GN + (g + 1) * N].astype(BF16)
        cb2 = _dot_nt(c_g, jnp.concatenate([b_g, b_g], axis=0))
        h_g = h_ref[0, gs, :]
        y_off = _dot_nt(c_g, h_g.astype(BF16)) * jnp.exp(a_exp[:, gs])
        pairs = []
        for pr in range(R // 2):
            i = g * (R // 2) + pr
            seg = a_expt[:, i * 2 * T:(i + 1) * 2 * T] - a_cum_t[i:i + 1, :]
            decay = jnp.exp(jnp.where(causal2, seg, -jnp.inf))
            m_pair = (cb2 * decay).astype(BF16)
            x_pair = xdt[:, i * 2 * P:(i + 1) * 2 * P]
            rhs = jnp.where(blockdiag, jnp.concatenate([x_pair, x_pair], axis=0), 0.0).astype(BF16)
            pairs.append(_dot(m_pair, rhs))
        y_g = jnp.concatenate(pairs, axis=1) + y_off + dskip_ref[:, gs] * xs[:, gs]
        a_g = a_exp[:, gs]
        dte = jnp.exp(a_g[T - 1:T, :] - a_g)
        st = _dot_tn((xdt[:, gs] * dte).astype(BF16), b_g)
        dec = jnp.concatenate(
            [jnp.broadcast_to(chunk_decay[(g * R + r) // 2:(g * R + r) // 2 + 1,
                                          ((g * R + r) % 2) * T + T - 1:((g * R + r) % 2) * T + T], (P, N))
             for r in range(R)], axis=0)
        h_ref[0, gs, :] = dec * h_g + st
        yg = y_g * _silu(z_ref[:, gs])
        ms = jnp.mean(yg * yg, axis=-1, keepdims=True)
        y_ref[:, gs] = (yg * lax.rsqrt(ms + RMS_EPS) * ng_ref[:, gs]).astype(y_ref.dtype)


def _ssd_mixer(u_ssd, u_small, dt_t, conv_prev, h0, prm, cols, *, n_batch, seq, row0, T, stacked=None):
    M = u_ssd.shape[0]
    n_heads = prm['dtb'].shape[1]
    DI = n_heads * SSD_HEAD_DIM
    BCW = conv_prev.shape[2] - DI
    nc = seq // T
    blk0 = row0 // T

    def rows(width, colblk):
        return pl.BlockSpec((T, width), lambda b, c: (blk0 + b * nc + c, colblk))

    def full(a):
        return pl.BlockSpec(a.shape, lambda b, c: (0,) * a.ndim)

    params = [prm['conv_w'], prm['conv_b'], prm['dtb'], prm['alog'], prm['dtbT'], prm['alogT'],
              prm['dskip'], prm['ng'], prm['expP'], prm['expT']]
    in_specs = [rows(DI, cols['z'] // DI), rows(DI, cols['xs'] // DI), rows(BCW, cols['bc'] // BCW),
                rows(US_W, 0),
                pl.BlockSpec((1,) + dt_t.shape[1:], lambda b, c: (b * nc + c, 0, 0)),
                pl.BlockSpec((1,) + conv_prev.shape[1:], lambda b, c: (b, 0, 0)),
                pl.BlockSpec((1,) + h0.shape[1:], lambda b, c: (b, 0, 0))] + [full(a) for a in params]
    out_specs = [rows(DI, 0),
                 pl.BlockSpec((1,) + conv_prev.shape[1:], lambda b, c: (b, 0, 0)),
                 pl.BlockSpec((1,) + h0.shape[1:], lambda b, c: (b, 0, 0))]
    call = _stacked_call(
        functools.partial(_ssd_kernel, T=T, n_heads=n_heads),
        grid=(n_batch, nc), in_specs=in_specs, out_specs=out_specs,
        out_shape=[jax.ShapeDtypeStruct((M, DI), BF16),
                   jax.ShapeDtypeStruct(conv_prev.shape, F32),
                   jax.ShapeDtypeStruct(h0.shape, F32)],
        stacked=stacked, n_stacked=1, sem=("parallel", "arbitrary"),
        scratch_shapes=[pltpu.VMEM((8, DI), F32), pltpu.VMEM((8, BCW), F32)])
    return call(u_ssd, u_ssd, u_ssd, u_small, dt_t, conv_prev, h0, *params)


def _sortable_key(x):
    b = lax.bitcast_convert_type(x, I32)
    return b ^ ((b >> 31) & 0x7FFFFFFF)


def _head_weights(us_ref):
    return us_ref[:, US_WI:US_WI + IDX_HEADS] * ((IDX_HEADS * IDX_DIM) ** -0.5)


def _indexer_scores(qi_heads, wi, ki_blk):
    Q = qi_heads[0].shape[0]
    acc = None
    for h0 in range(0, IDX_HEADS, IDX_STACK):
        l = _dot_nt(jnp.concatenate(qi_heads[h0:h0 + IDX_STACK], axis=0), ki_blk)
        for n in range(IDX_STACK):
            t = jnp.maximum(l[n * Q:(n + 1) * Q], 0.0) * wi[:, h0 + n:h0 + n + 1]
            acc = t if acc is None else acc + t
    return acc


def _kth_largest(count_ge, shape, k):
    c0 = count_ge(jnp.zeros(shape, I32))
    t0 = jnp.where(c0 >= k, 0, INT_MIN).astype(I32)

    def body(i, t):
        cand = t | (jnp.int32(1) << (30 - i))
        return jnp.where(count_ge(cand) >= k, cand, t)

    return lax.fori_loop(0, 31, body, t0)


def _tie_cutoff(count_ge, count_tie_below, thr, k, n_keys, cut_s):
    cut_s[...] = jnp.full(thr.shape, 2147483647, I32)
    n_ge = count_ge(thr)

    @pl.when(jnp.max(n_ge) > k)
    def _():
        n_gt = jnp.where(thr == 2147483647, 0, count_ge(thr + 1))
        need = k - n_gt

        def body(i, v):
            cand = v | (jnp.int32(1) << (n_keys.bit_length() - 1 - i))
            return jnp.where(count_tie_below(cand) < need, cand, v)

        v = lax.fori_loop(0, n_keys.bit_length(), body, jnp.zeros(thr.shape, I32))
        cut_s[...] = jnp.where(n_ge > k, v + 1, 2147483647)


def _selected(key, idx, thr, cut):
    return (key > INT_MIN) & ((key > thr) | ((key == thr) & (idx < cut)))


def _limits(pos0, n_rows):
    pos = pos0 + lax.broadcasted_iota(I32, (n_rows, 1), 0)
    return (pos // CHUNK + 1) * CHUNK


def _stack_q_heads(q_ref, g):
    parts = [q_ref[:, (g * Q_PER_KV + i) * HEAD_DIM:(g * Q_PER_KV + i + 1) * HEAD_DIM] for i in range(Q_PER_KV)]
    return (jnp.concatenate(parts, axis=0) * (HEAD_DIM ** -0.5 * LOG2E)).astype(BF16)


def _dsa_prompt_kernel(q_ref, qi_ref, us_ref, ki_ref, k_ref, vt_ref, o_ref, key_s, bias_s, cut_s, *,
                       n_sel, kv_heads):
    QB, KB = DSA_QB, DSA_KB
    nkb_max = key_s.shape[0]
    j = pl.program_id(1)
    nkb = ((j + 1) * QB + KB - 1) // KB
    wi = _head_weights(us_ref)
    qi_heads = [qi_ref[:, h * IDX_DIM:(h + 1) * IDX_DIM].astype(BF16) for h in range(IDX_HEADS)]
    pos = j * QB + lax.broadcasted_iota(I32, (1, QB), 1)
    lim = (pos // CHUNK + 1) * CHUNK

    def score_blk(kb, carry):
        off = pl.multiple_of(kb * KB, KB)
        sc = _indexer_scores(qi_heads, wi, ki_ref[pl.ds(off, KB), :])
        s_idx = off + lax.broadcasted_iota(I32, (KB, QB), 0)
        key_s[kb] = jnp.where(s_idx < lim, _sortable_key(sc.T), INT_MIN)
        return carry

    lax.fori_loop(0, nkb, score_blk, 0)

    def count_ge(cand):
        def body(kb, acc):
            m = (key_s[kb] >= cand).astype(I32)
            return acc + jnp.sum(m.reshape(KB // 8, 8, QB), axis=0)
        acc = lax.fori_loop(0, nkb, body, jnp.zeros((8, QB), I32))
        return jnp.sum(acc, axis=0, keepdims=True)

    thr = _kth_largest(count_ge, (1, QB), n_sel)

    def key_index(kb):
        return kb * KB + lax.broadcasted_iota(I32, (KB, QB), 0)

    def count_tie_below(c):
        def body(kb, acc):
            m = ((key_s[kb] == thr) & (key_index(kb) < c)).astype(I32)
            return acc + jnp.sum(m.reshape(KB // 8, 8, QB), axis=0)
        acc = lax.fori_loop(0, nkb, body, jnp.zeros((8, QB), I32))
        return jnp.sum(acc, axis=0, keepdims=True)

    _tie_cutoff(count_ge, count_tie_below, thr, n_sel, nkb_max * KB, cut_s)
    cut = cut_s[...]

    def bias_blk(kb, carry):
        bias_s[kb] = jnp.where(_selected(key_s[kb], key_index(kb), thr, cut), 0.0, NEG)
        return carry

    lax.fori_loop(0, nkb, bias_blk, 0)

    cols = Q_PER_KV * QB
    GPL = DSA_GROUPS_PER_LOOP
    for g0 in range(0, kv_heads, GPL):
        groups = range(g0, g0 + GPL)
        qs = [_stack_q_heads(q_ref, g) for g in groups]

        def body(kb, carry):
            off = pl.multiple_of(kb * KB, KB)
            b = bias_s[kb]
            b4 = jnp.concatenate([b] * Q_PER_KV, axis=1)
            heads = [slice(g * HEAD_DIM, (g + 1) * HEAD_DIM) for g in groups]

            def logits(n):
                return _dot_nt(k_ref[pl.ds(off, KB), heads[n]], qs[n]) + b4

            t_next = logits(0)
            new = []
            for n in range(GPL):
                t = t_next
                if n + 1 < GPL:
                    t_next = logits(n + 1)
                m, l, acc = carry[n]
                m_new = jnp.maximum(m, jnp.max(t, axis=0, keepdims=True))
                alpha = jnp.exp2(m - m_new)
                p = jnp.exp2(t - m_new)
                l = alpha * l + jnp.sum(p, axis=0, keepdims=True)
                acc = alpha * acc + _dot(vt_ref[0, kb, heads[n], :], p.astype(BF16))
                new.append((m_new, l, acc))
            return tuple(new)

        init = tuple((jnp.full((1, cols), NEG, F32), jnp.zeros((1, cols), F32), jnp.zeros((HEAD_DIM, cols), F32))
                     for _ in groups)
        res = lax.fori_loop(0, nkb, body, init)
        for n, g in enumerate(groups):
            _, l, acc = res[n]
            out = acc / l
            for i in range(Q_PER_KV):
                o_ref[:, (g * Q_PER_KV + i) * HEAD_DIM:(g * Q_PER_KV + i + 1) * HEAD_DIM] = (
                    out[:, i * QB:(i + 1) * QB].T.astype(o_ref.dtype))


def _dsa_prompt(u_att, u_small, ki_b, k_b, vt_b, cols, *, n_batch, seq, kv_heads):
    M = u_att.shape[0]
    QB, KB = DSA_QB, DSA_KB
    nq = seq // QB
    nkb_max = seq // KB
    DQ = kv_heads * Q_PER_KV * HEAD_DIM
    DK = kv_heads * HEAD_DIM
    DQI = IDX_HEADS * IDX_DIM
    n_sel = min(TOPK_MAX, seq // 4)
    return pl.pallas_call(
        functools.partial(_dsa_prompt_kernel, n_sel=n_sel, kv_heads=kv_heads),
        grid=(n_batch, nq),
        in_specs=[pl.BlockSpec((QB, DQ), lambda b, j: (b * nq + j, cols['q'] // DQ)),
                  pl.BlockSpec((QB, DQI), lambda b, j: (b * nq + j, cols['qi'] // DQI)),
                  pl.BlockSpec((QB, US_W), lambda b, j: (b * nq + j, 0)),
                  pl.BlockSpec((seq, IDX_DIM), lambda b, j: (b, 0)),
                  pl.BlockSpec((seq, DK), lambda b, j: (b, 0)),
                  pl.BlockSpec((1, nkb_max, DK, KB), lambda b, j: (b, 0, 0, 0))],
        out_specs=pl.BlockSpec((QB, DQ), lambda b, j: (b * nq + j, 0)),
        out_shape=jax.ShapeDtypeStruct((M, DQ), BF16),
        scratch_shapes=[pltpu.VMEM((nkb_max, KB, QB), I32), pltpu.VMEM((nkb_max, KB, QB), F32),
                        pltpu.VMEM((1, QB), I32)],
        compiler_params=_cparams(("parallel", "arbitrary")),
    )(u_att, u_att, u_small, ki_b, k_b, vt_b)


def _dsa_sample_kernel(q_ref, qi_ref, us_ref, kn_ref, vn_ref, ck_ref, cv_ref, cki_ref, o_ref, cut_s, *,
                       n_sel, kv_heads, past):
    Q = q_ref.shape[0]
    wi = _head_weights(us_ref)
    qi_heads = [qi_ref[:, h * IDX_DIM:(h + 1) * IDX_DIM].astype(BF16) for h in range(IDX_HEADS)]
    lim = _limits(past, Q)
    ki_new = us_ref[:, US_KI:US_KI + IDX_DIM].astype(BF16)
    sc_p = _indexer_scores(qi_heads, wi, cki_ref[0].astype(BF16))
    sc_n = _indexer_scores(qi_heads, wi, ki_new)
    idx_p = lax.broadcasted_iota(I32, (Q, past), 1)
    idx_n = past + lax.broadcasted_iota(I32, (Q, Q), 1)
    key_p = jnp.where(idx_p < lim, _sortable_key(sc_p), INT_MIN)
    key_n = jnp.where(idx_n < lim, _sortable_key(sc_n), INT_MIN)

    def count_ge(cand):
        return (jnp.sum((key_p >= cand).astype(I32), axis=1, keepdims=True)
                + jnp.sum((key_n >= cand).astype(I32), axis=1, keepdims=True))

    thr = _kth_largest(count_ge, (Q, 1), n_sel)

    def count_tie_below(c):
        return (jnp.sum(((key_p == thr) & (idx_p < c)).astype(I32), axis=1, keepdims=True)
                + jnp.sum(((key_n == thr) & (idx_n < c)).astype(I32), axis=1, keepdims=True))

    _tie_cutoff(count_ge, count_tie_below, thr, n_sel, past + Q, cut_s)
    cut = cut_s[...]
    bias_p = jnp.where(_selected(key_p, idx_p, thr, cut), 0.0, NEG)
    bias_n = jnp.where(_selected(key_n, idx_n, thr, cut), 0.0, NEG)
    bias_p = jnp.concatenate([bias_p] * Q_PER_KV, axis=0)
    bias_n = jnp.concatenate([bias_n] * Q_PER_KV, axis=0)

    for g in range(kv_heads):
        hs = slice(g * HEAD_DIM, (g + 1) * HEAD_DIM)
        qs = _stack_q_heads(q_ref, g)
        ck = ck_ref[0, pl.ds(g, past, stride=kv_heads), :].astype(BF16)
        cv = cv_ref[0, pl.ds(g, past, stride=kv_heads), :].astype(BF16)
        t_p = _dot_nt(qs, ck) + bias_p
        t_n = _dot_nt(qs, kn_ref[:, hs].astype(BF16)) + bias_n
        m = jnp.maximum(jnp.max(t_p, axis=-1, keepdims=True), jnp.max(t_n, axis=-1, keepdims=True))
        p_p = jnp.exp2(t_p - m)
        p_n = jnp.exp2(t_n - m)
        l = jnp.sum(p_p, axis=-1, keepdims=True) + jnp.sum(p_n, axis=-1, keepdims=True)
        acc = _dot(p_p.astype(BF16), cv) + _dot(p_n.astype(BF16), vn_ref[:, hs].astype(BF16))
        out = acc / l
        for i in range(Q_PER_KV):
            o_ref[:, (g * Q_PER_KV + i) * HEAD_DIM:(g * Q_PER_KV + i + 1) * HEAD_DIM] = (
                out[i * Q:(i + 1) * Q].astype(o_ref.dtype))


def _dsa_sample(u_att, u_small, cache_k, cache_v, cache_ki, cols, *, n_batch, seq, row0, kv_heads, stacked):
    past = cache_ki.shape[1]
    DQ = kv_heads * Q_PER_KV * HEAD_DIM
    DK = kv_heads * HEAD_DIM
    DQI = IDX_HEADS * IDX_DIM
    n_sel = min(TOPK_MAX, (past + seq) // 4)
    blk0 = row0 // seq
    call = _stacked_call(
        functools.partial(_dsa_sample_kernel, n_sel=n_sel, kv_heads=kv_heads, past=past),
        grid=(n_batch,),
        in_specs=[pl.BlockSpec((seq, DQ), lambda b: (blk0 + b, cols['q'] // DQ)),
                  pl.BlockSpec((seq, DQI), lambda b: (blk0 + b, cols['qi'] // DQI)),
                  pl.BlockSpec((seq, US_W), lambda b: (blk0 + b, 0)),
                  pl.BlockSpec((seq, DK), lambda b: (blk0 + b, cols['k'] // DK)),
                  pl.BlockSpec((seq, DK), lambda b: (blk0 + b, cols['v'] // DK)),
                  pl.BlockSpec((1, past * kv_heads, HEAD_DIM), lambda b: (b, 0, 0)),
                  pl.BlockSpec((1, past * kv_heads, HEAD_DIM), lambda b: (b, 0, 0)),
                  pl.BlockSpec((1, past, IDX_DIM), lambda b: (b, 0, 0))],
        out_specs=[pl.BlockSpec((seq, DQ), lambda b: (blk0 + b, 0))],
        out_shape=[jax.ShapeDtypeStruct(stacked[0].shape, BF16)],
        stacked=stacked, n_stacked=1, sem=("parallel",), scratch_shapes=[pltpu.VMEM((seq, 1), I32)])
    return call(u_att, u_att, u_small, u_att, u_att, cache_k, cache_v, cache_ki)[0]


def _pair_layout(v, T):
    return jnp.repeat(v.reshape(-1, 2), T, axis=1)


def _dt_transposed(dt_raw, T):
    n, H = dt_raw.shape
    return dt_raw.reshape(n // T, T, H // 2, 2).transpose(0, 2, 3, 1).reshape(n // T, H // 2, 2 * T)


def _ssd_params(conv_w, conv_b, dt_bias, a_log, d_skip, norm_g, T):
    H = dt_bias.shape[0]
    P = SSD_HEAD_DIM
    eye = jnp.eye(H, dtype=F32)
    return dict(conv_w=conv_w, conv_b=conv_b.reshape(1, -1),
                dtb=dt_bias.reshape(1, H), alog=a_log.reshape(1, H),
                dtbT=_pair_layout(dt_bias, T), alogT=_pair_layout(a_log, T),
                dskip=jnp.repeat(d_skip, P).reshape(1, H * P), ng=norm_g.reshape(1, -1),
                expP=jnp.repeat(eye, P, axis=1).astype(BF16), expT=jnp.repeat(eye, T, axis=1).astype(BF16))


def kernel(x_prompt, x_sample, mem_prompt, cache_k, cache_v, cache_idx_k, cache_mem_k, cache_mem_v,
           state_ssm, state_conv, ln1_g, ln1_b, ffn1_w1, ffn1_w3, ffn1_w2, w_in, conv_w, conv_b,
           dt_bias, a_log, d_skip, ssd_norm_g, w_ssd_br, w_att_br, w_out, ln2_g, ln2_b,
           w_mq, w_mk, w_mv, w_mo, ln3_g, ln3_b, ffn2_w1, ffn2_w3, ffn2_w2, ln4_g, ln4_b):
    assert x_prompt.ndim == 3 and ln1_g.shape[0] == DEPTH == 1
    l = 0
    BP, LP, D = x_prompt.shape
    BS, LS, _ = x_sample.shape
    MP, MS = BP * LP, BS * LS
    M = MP + MS
    H = dt_bias.shape[1]
    assert H == SSD_HEADS
    DI = H * SSD_HEAD_DIM
    BCW = conv_w.shape[2] - DI
    KVH = cache_k.shape[3]
    DQ = KVH * Q_PER_KV * HEAD_DIM
    DK = KVH * HEAD_DIM
    DQI = IDX_HEADS * IDX_DIM
    MT = mem_prompt.shape[1]
    PAST = cache_k.shape[2]
    tm_big = M // 8
    tm_half = M // 16

    w0t = w_in[l].T
    c_dt = 2 * DI + BCW
    c_q = c_dt + H
    c_wi = c_q + DQ + 2 * DK + DQI
    c_ki = c_wi + IDX_HEADS
    c_g = c_ki + IDX_DIM
    assert c_g + 2 * D == w0t.shape[0]
    cols_ssd = dict(z=0, xs=DI, bc=2 * DI)
    cols_att = dict(q=0, k=DQ, v=DQ + DK, qi=DQ + 2 * DK)
    n_small = H + IDX_HEADS + IDX_DIM
    assert n_small <= US_W
    w_small_t = jnp.concatenate([w0t[c_dt:c_q], w0t[c_wi:c_g], jnp.zeros((US_W - n_small, D), F32)], axis=0)

    x_p2, x_s2 = x_prompt.reshape(MP, D), x_sample.reshape(MS, D)
    x0b = _cast_rows(x_p2, out_rows=M, out_row0=0)
    x0b, = _cast_rows(x_s2, out_rows=M, out_row0=MP, stacked=x0b)

    def ffn_delta(xb, w1, w3, w2, residual=None):
        h = _ffn_up(xb, w1[l], w3[l], tm=M // 4, tn=256)
        return _matmul(h, w2[l].astype(BF16), tm=tm_big, tn=256, out_dtype=F32, single_buffer_x=True,
                       residual=residual, scale=0.5)

    d1 = ffn_delta(x0b, ffn1_w1, ffn1_w3, ffn1_w2)
    ln1 = _res_layer_norm(x_p2, d1, ln1_g[l], ln1_b[l], scale=0.5, n_rows=MP, out_rows=M)
    x1, x1b = _res_layer_norm(x_s2, d1, ln1_g[l], ln1_b[l], scale=0.5, n_rows=MS, d_row0=MP, out_row0=MP,
                              stacked=ln1)

    u_ssd = _matmul_nt(x1b, w0t, row0=0, n_rows=c_dt, tm=tm_big, tn=512, out_dtype=F32)
    u_att = _matmul_nt(x1b, w0t, row0=c_q, n_rows=c_wi - c_q, tm=tm_big, tn=512, out_dtype=F32)
    u_gate = _matmul_nt(x1b, w0t, row0=c_g, n_rows=2 * D, tm=tm_big, tn=512, out_dtype=F32)
    u_small = _matmul_nt(x1b, w_small_t, row0=0, n_rows=US_W, tm=tm_big, tn=US_W, out_dtype=F32)

    dt_raw = u_small[:, US_DT:US_DT + H]
    zeros_conv = jnp.zeros((BP,) + state_conv.shape[2:], F32)
    zeros_h = jnp.zeros((BP, DI, SSD_STATE), F32)
    prm_p = _ssd_params(conv_w[l], conv_b[l], dt_bias[l], a_log[l], d_skip[l], ssd_norm_g[l], CHUNK)
    y_ssd, conv_p, h_p = _ssd_mixer(u_ssd, u_small, _dt_transposed(dt_raw[:MP], CHUNK), zeros_conv, zeros_h,
                                    prm_p, cols_ssd, n_batch=BP, seq=LP, row0=0, T=CHUNK)
    TS = min(CHUNK, LS)
    prm_s = _ssd_params(conv_w[l], conv_b[l], dt_bias[l], a_log[l], d_skip[l], ssd_norm_g[l], TS)
    y_ssd, conv_s, h_s = _ssd_mixer(u_ssd, u_small, _dt_transposed(dt_raw[MP:], TS), state_conv[l],
                                    state_ssm[l].reshape(BS, DI, SSD_STATE),
                                    prm_s, cols_ssd, n_batch=BS, seq=LS, row0=MP, T=TS, stacked=(y_ssd,))

    k_all = u_att[:, cols_att['k']:cols_att['k'] + DK]
    v_all = u_att[:, cols_att['v']:cols_att['v'] + DK]
    ki_all = u_small[:, US_KI:US_KI + IDX_DIM]
    vt_b = v_all[:MP].astype(BF16).reshape(BP, LP // DSA_KB, DSA_KB, DK).transpose(0, 1, 3, 2)
    y_att = _dsa_prompt(u_att, u_small, ki_all[:MP].astype(BF16), k_all[:MP].astype(BF16),
                        vt_b, cols_att, n_batch=BP, seq=LP, kv_heads=KVH)
    y_att = _dsa_sample(u_att, u_small, cache_k[l].reshape(BS, PAST * KVH, HEAD_DIM),
                        cache_v[l].reshape(BS, PAST * KVH, HEAD_DIM), cache_idx_k[l], cols_att,
                        n_batch=BS, seq=LS, row0=MP, kv_heads=KVH, stacked=(y_att,))

    merged = _gated_merge(y_ssd, y_att, w_ssd_br[l], w_att_br[l], u_gate, 0, D, tm=tm_big, tn=256)
    s2 = _matmul(merged, w_out[l], tm=tm_big, tn=512, out_dtype=F32, residual=x1)
    x2, x2b = _res_layer_norm(s2, None, ln2_g[l], ln2_b[l], n_rows=M)

    memb = mem_prompt.reshape(BP * MT, D).astype(BF16)
    mk_p = _matmul(memb, w_mk[l], tm=BP * MT, tn=512, out_dtype=F32)
    mv_p = _matmul(memb, w_mv[l], tm=BP * MT, tn=512, out_dtype=F32)
    qm = _matmul(x2b, w_mq[l], tm=tm_big, tn=512, out_dtype=BF16)
    o_m = _memory_attention(qm, mk_p, mv_p, n_batch=BP, rows_per_batch=LP, row0=0, tq=min(512, LP))
    o_m = _memory_attention(qm, cache_mem_k[l].reshape(BS * MT, D), cache_mem_v[l].reshape(BS * MT, D),
                            n_batch=BS, rows_per_batch=LS, row0=MP, tq=LS, stacked=(o_m,))
    s3 = _matmul(o_m, w_mo[l], tm=tm_big, tn=512, out_dtype=F32, residual=x2)
    x3, x3b = _res_layer_norm(s3, None, ln3_g[l], ln3_b[l], n_rows=M)

    s4 = ffn_delta(x3b, ffn2_w1, ffn2_w3, ffn2_w2, residual=x3)
    y_p, = _res_layer_norm(s4, None, ln4_g[l], ln4_b[l], n_rows=MP, emit_bf16=False)
    y_s, = _res_layer_norm(s4, None, ln4_g[l], ln4_b[l], n_rows=MS, x_row0=MP, emit_bf16=False)

    mh = D // MEM_HEADS
    return (y_p.reshape(BP, LP, D), y_s.reshape(BS, LS, D),
            h_p.reshape(1, BP, H, SSD_HEAD_DIM, SSD_STATE), conv_p[None],
            k_all[:MP].reshape(1, BP, LP, KVH, HEAD_DIM), v_all[:MP].reshape(1, BP, LP, KVH, HEAD_DIM),
            ki_all[:MP].reshape(1, BP, LP, IDX_DIM),
            mk_p.reshape(1, BP, MT, MEM_HEADS, mh), mv_p.reshape(1, BP, MT, MEM_HEADS, mh),
            h_s.reshape(1, BS, H, SSD_HEAD_DIM, SSD_STATE), conv_s[None],
            k_all[MP:].reshape(1, BS, LS, KVH, HEAD_DIM), v_all[MP:].reshape(1, BS, LS, KVH, HEAD_DIM),
            ki_all[MP:].reshape(1, BS, LS, IDX_DIM))
```

```python
import functools
import math

import jax
import jax.numpy as jnp
from jax import lax
from jax.experimental import pallas as pl
from jax.experimental.pallas import tpu as pltpu

F32 = jnp.float32
BF16 = jnp.bfloat16
I32 = jnp.int32

DEPTH = 1
CHUNK = 64
SSD_HEADS = 64
SSD_HEAD_DIM = 64
SSD_HEADS_PER_GROUP = 8
SSD_STATE = 128
SSD_CONV_W = 4
HEAD_DIM = 128
Q_PER_KV = 4
IDX_HEADS = 16
IDX_DIM = 64
TOPK_MAX = 256
MEM_HEADS = 4
ALPHA = (2.0 * DEPTH) ** 0.25
LN_EPS = 1e-5
RMS_EPS = 1e-5

LANES = 128
VMEM_LIMIT = 56 * 1024 * 1024

NEG = -1e30
INT_MIN = -2147483648
LOG2E = math.log2(math.e)

DSA_QB = 128
DSA_KB = 512
DSA_GROUPS_PER_LOOP = 8
IDX_STACK = 4
VPAD = 16
VROWS = HEAD_DIM + VPAD
LN_ROWS = 256

US_DT = 0
US_WI = US_DT + SSD_HEADS
US_KI = US_WI + IDX_HEADS
US_W = 256


def _cparams(sem):
    return pltpu.CompilerParams(dimension_semantics=sem, vmem_limit_bytes=VMEM_LIMIT)


def _dot(a, b):
    return jnp.dot(a, b, preferred_element_type=F32)


def _dot_nt(a, b):
    return lax.dot_general(a, b, (((1,), (1,)), ((), ())), preferred_element_type=F32)


def _dot_tn(a, b, precision=None):
    return lax.dot_general(a, b, (((0,), (0,)), ((), ())), preferred_element_type=F32,
                           precision=precision)


def _dot_hi(a, b):
    return jnp.dot(a, b, preferred_element_type=F32, precision=lax.Precision.HIGHEST)


def _sigmoid(x):
    return 1.0 / (1.0 + jnp.exp(-x))


def _silu(x):
    return x * _sigmoid(x)


def _softplus(x):
    return jnp.maximum(x, 0.0) + jnp.log1p(jnp.exp(-jnp.abs(x)))


def _stacked_call(kernel, *, grid, in_specs, out_specs, out_shape, stacked, n_stacked, sem, scratch_shapes=()):
    if stacked is None:
        return pl.pallas_call(kernel, grid=grid, in_specs=in_specs, out_specs=out_specs, out_shape=out_shape,
                              scratch_shapes=scratch_shapes, compiler_params=_cparams(sem))
    n_in = len(in_specs)

    def body(*refs):
        kernel(*refs[:n_in], *refs[n_in + n_stacked:])

    call = pl.pallas_call(
        body, grid=grid,
        in_specs=list(in_specs) + [pl.BlockSpec(memory_space=pl.ANY)] * n_stacked,
        out_specs=out_specs, out_shape=out_shape, scratch_shapes=scratch_shapes,
        input_output_aliases={n_in + i: i for i in range(n_stacked)},
        compiler_params=_cparams(sem))
    return lambda *args: call(*args, *stacked)


def _mm_kernel(x_ref, w_ref, o_ref):
    o_ref[...] = _dot(x_ref[...], w_ref[...].astype(BF16)).astype(o_ref.dtype)


def _mm_res_kernel(x_ref, w_ref, r_ref, o_ref, *, scale):
    o_ref[...] = ALPHA * r_ref[...] + scale * _dot(x_ref[...], w_ref[...].astype(BF16))


def _matmul(x, w, *, tm, tn, out_dtype, single_buffer_x=False, residual=None, scale=1.0):
    M, K = x.shape
    N = w.shape[1]
    assert N % tn == 0 and M % tm == 0
    xmode = dict(pipeline_mode=pl.Buffered(1)) if single_buffer_x else {}
    tile = pl.BlockSpec((tm, tn), lambda i, j: (i, j))
    in_specs = [pl.BlockSpec((tm, K), lambda i, j: (i, 0), **xmode),
                pl.BlockSpec((K, tn), lambda i, j: (0, j))]
    if residual is None:
        body, args = _mm_kernel, (x, w)
    else:
        assert out_dtype == F32
        body, args = functools.partial(_mm_res_kernel, scale=scale), (x, w, residual)
        in_specs.append(tile)
    return pl.pallas_call(
        body,
        grid=(M // tm, N // tn),
        in_specs=in_specs,
        out_specs=tile,
        out_shape=jax.ShapeDtypeStruct((M, N), out_dtype),
        compiler_params=_cparams(("parallel", "arbitrary")),
    )(*args)


def _mm_nt_kernel(x_ref, wt_ref, o_ref):
    o_ref[...] = _dot_nt(x_ref[...], wt_ref[...].astype(BF16)).astype(o_ref.dtype)


def _matmul_nt(x, wt, *, row0, n_rows, tm, tn, out_dtype):
    M, K = x.shape
    assert n_rows % tn == 0 and M % tm == 0 and row0 % 8 == 0
    return pl.pallas_call(
        _mm_nt_kernel,
        grid=(M // tm, n_rows // tn),
        in_specs=[pl.BlockSpec((tm, K), lambda i, j: (i, 0), pipeline_mode=pl.Buffered(1)),
                  pl.BlockSpec((pl.Element(tn), pl.Element(K)), lambda i, j: (pl.multiple_of(row0 + j * tn, 8), 0))],
        out_specs=pl.BlockSpec((tm, tn), lambda i, j: (i, j)),
        out_shape=jax.ShapeDtypeStruct((M, n_rows), out_dtype),
        compiler_params=_cparams(("parallel", "arbitrary")),
    )(x, wt)


def _cast_kernel(x_ref, o_ref):
    o_ref[...] = x_ref[...].astype(o_ref.dtype)


def _cast_rows(x, *, out_rows, out_row0, stacked=None):
    n, D = x.shape
    tr = max(r for r in range(16, LN_ROWS + 1, 16) if n % r == 0 and out_row0 % r == 0)
    ob = out_row0 // tr
    call = _stacked_call(
        _cast_kernel, grid=(n // tr,),
        in_specs=[pl.BlockSpec((tr, D), lambda i: (i, 0))],
        out_specs=[pl.BlockSpec((tr, D), lambda i: (ob + i, 0))],
        out_shape=[jax.ShapeDtypeStruct((out_rows, D), BF16)],
        stacked=stacked, n_stacked=1, sem=("parallel",))
    return call(x)


def _ffn_up_kernel(x_ref, w1_ref, w3_ref, o_ref):
    x = x_ref[...]
    a = _dot(x, w1_ref[...].astype(BF16))
    b = _dot(x, w3_ref[...].astype(BF16))
    o_ref[...] = (_silu(a) * b).astype(o_ref.dtype)


def _ffn_up(x, w1, w3, *, tm, tn):
    M, K = x.shape
    N = w1.shape[1]
    return pl.pallas_call(
        _ffn_up_kernel,
        grid=(M // tm, N // tn),
        in_specs=[pl.BlockSpec((tm, K), lambda i, j: (i, 0), pipeline_mode=pl.Buffered(1)),
                  pl.BlockSpec((K, tn), lambda i, j: (0, j)),
                  pl.BlockSpec((K, tn), lambda i, j: (0, j))],
        out_specs=pl.BlockSpec((tm, tn), lambda i, j: (i, j)),
        out_shape=jax.ShapeDtypeStruct((M, N), BF16),
        compiler_params=_cparams(("parallel", "arbitrary")),
    )(x, w1, w3)


def _gate_kernel(ys_ref, ya_ref, ws_ref, wa_ref, gs_ref, ga_ref, o_ref):
    s = _dot(ys_ref[...], ws_ref[...].astype(BF16))
    a = _dot(ya_ref[...], wa_ref[...].astype(BF16))
    o_ref[...] = (_sigmoid(gs_ref[...]) * s + _sigmoid(ga_ref[...]) * a).astype(o_ref.dtype)


def _gated_merge(y_ssd, y_att, w_ssd, w_att, u_gate, gs_col, ga_col, *, tm, tn):
    M, K = y_ssd.shape
    N = w_ssd.shape[1]
    gs_blk, ga_blk = gs_col // tn, ga_col // tn
    once = dict(pipeline_mode=pl.Buffered(1))
    return pl.pallas_call(
        _gate_kernel,
        grid=(M // tm, N // tn),
        in_specs=[pl.BlockSpec((tm, K), lambda i, j: (i, 0), **once),
                  pl.BlockSpec((tm, K), lambda i, j: (i, 0), **once),
                  pl.BlockSpec((K, tn), lambda i, j: (0, j)),
                  pl.BlockSpec((K, tn), lambda i, j: (0, j)),
                  pl.BlockSpec((tm, tn), lambda i, j: (i, gs_blk + j)),
                  pl.BlockSpec((tm, tn), lambda i, j: (i, ga_blk + j))],
        out_specs=pl.BlockSpec((tm, tn), lambda i, j: (i, j)),
        out_shape=jax.ShapeDtypeStruct((M, N), BF16),
        compiler_params=_cparams(("parallel", "arbitrary")),
    )(y_ssd, y_att, w_ssd, w_att, u_gate, u_gate)


def _layer_norm_rows(y, g, b):
    mu = jnp.mean(y, axis=-1, keepdims=True)
    yc = y - mu
    var = jnp.mean(yc * yc, axis=-1, keepdims=True)
    return yc * lax.rsqrt(var + LN_EPS) * g + b


def _ln_kernel(x_ref, d_ref, g_ref, b_ref, *o_refs, scale):
    o = _layer_norm_rows(ALPHA * x_ref[...] + scale * d_ref[...], g_ref[...], b_ref[...])
    for o_ref in o_refs:
        o_ref[...] = o.astype(o_ref.dtype)


def _ln_presummed_kernel(y_ref, g_ref, b_ref, *o_refs):
    o = _layer_norm_rows(y_ref[...], g_ref[...], b_ref[...])
    for o_ref in o_refs:
        o_ref[...] = o.astype(o_ref.dtype)


def _res_layer_norm(x, delta, g, b, *, scale=None, n_rows, x_row0=0, d_row0=0, out_rows=None, out_row0=0,
                    stacked=None, emit_bf16=True):
    D = x.shape[1]
    out_rows = n_rows if out_rows is None else out_rows
    if stacked is not None:
        out_rows = stacked[0].shape[0]
    tr = max(r for r in range(8, LN_ROWS + 1, 8)
             if all(v % r == 0 for v in (n_rows, x_row0, d_row0, out_row0)))
    xb, db, ob = x_row0 // tr, d_row0 // tr, out_row0 // tr
    vec = pl.BlockSpec((1, D), lambda i: (0, 0))
    dts = (F32, BF16) if emit_bf16 else (F32,)
    if delta is None:
        body, args = _ln_presummed_kernel, (x,)
        in_specs = [pl.BlockSpec((tr, D), lambda i: (xb + i, 0)), vec, vec]
    else:
        body, args = functools.partial(_ln_kernel, scale=scale), (x, delta)
        in_specs = [pl.BlockSpec((tr, D), lambda i: (xb + i, 0)), pl.BlockSpec((tr, D), lambda i: (db + i, 0)), vec, vec]
    call = _stacked_call(
        body, grid=(n_rows // tr,), in_specs=in_specs,
        out_specs=[pl.BlockSpec((tr, D), lambda i: (ob + i, 0)) for _ in dts],
        out_shape=[jax.ShapeDtypeStruct((out_rows, D), dt) for dt in dts],
        stacked=stacked, n_stacked=len(dts), sem=("parallel",))
    return call(*args, g.reshape(1, D), b.reshape(1, D))


def _memattn_kernel(q_ref, k_ref, v_ref, o_ref):
    dh = q_ref.shape[1] // MEM_HEADS
    scale = dh ** -0.5
    for h in range(MEM_HEADS):
        sl = slice(h * dh, (h + 1) * dh)
        q = q_ref[:, sl]
        k = k_ref[:, sl].astype(BF16)
        v = v_ref[:, sl].astype(BF16)
        s = _dot_nt(q, k) * scale
        m = jnp.max(s, axis=-1, keepdims=True)
        p = jnp.exp(s - m)
        p = p / jnp.sum(p, axis=-1, keepdims=True)
        o_ref[:, sl] = _dot(p.astype(BF16), v).astype(o_ref.dtype)


def _memory_attention(q, mem_k, mem_v, *, n_batch, rows_per_batch, row0, tq, stacked=None):
    M, D = q.shape
    mt = mem_k.shape[0] // n_batch
    nq = rows_per_batch // tq
    blk0 = row0 // tq
    qspec = pl.BlockSpec((tq, D), lambda b, j: (blk0 + b * nq + j, 0))
    mspec = pl.BlockSpec((mt, D), lambda b, j: (b, 0))
    call = _stacked_call(
        _memattn_kernel, grid=(n_batch, nq), in_specs=[qspec, mspec, mspec], out_specs=[qspec],
        out_shape=[jax.ShapeDtypeStruct((M, D), BF16)], stacked=stacked, n_stacked=1,
        sem=("parallel", "arbitrary"))
    return call(q, mem_k, mem_v)[0]


def _ssd_kernel(z_ref, xs_ref, bc_ref, us_ref, dtT_ref, cprev_ref, h0_ref,
                convw_ref, convb_ref, dtb_ref, alog_ref, dtbT_ref, alogT_ref, dskip_ref, ng_ref,
                expP_ref, expT_ref,
                y_ref, cnew_ref, h_ref, extx_s, extbc_s, *, T, n_heads):
    P, R, N = SSD_HEAD_DIM, SSD_HEADS_PER_GROUP, SSD_STATE
    G = n_heads // R
    DI = n_heads * P
    GN = G * N
    c = pl.program_id(1)
    W1 = SSD_CONV_W - 1
    base = 8 - W1

    @pl.when(c == 0)
    def _():
        extx_s[0:base, :] = jnp.zeros((base, DI), F32)
        extbc_s[0:base, :] = jnp.zeros((base, 2 * GN), F32)
        extx_s[base:8, :] = cprev_ref[0, :, :DI]
        extbc_s[base:8, :] = cprev_ref[0, :, DI:]
        h_ref[0] = h0_ref[0]

    def conv(hist_s, x, lo, hi):
        xx = jnp.concatenate([hist_s[...], x], axis=0)
        out = convb_ref[:, lo:hi]
        for j in range(SSD_CONV_W):
            tap = xx if j == W1 else pltpu.roll(xx, W1 - j, axis=0)
            out = out + convw_ref[j:j + 1, lo:hi] * tap[8:8 + T]
        return _silu(out)

    x_in = xs_ref[...]
    bc_in = bc_ref[...]
    xs = conv(extx_s, x_in, 0, DI)
    bc = conv(extbc_s, bc_in, DI, DI + 2 * GN)
    newx = x_in[T - W1:T]
    newbc = bc_in[T - W1:T]
    cnew_ref[0, :, :DI] = newx
    cnew_ref[0, :, DI:] = newbc
    extx_s[base:8, :] = newx
    extbc_s[base:8, :] = newbc

    def expand(vals, e_ref):
        pieces = []
        for v in vals:
            for _ in range(3):
                p = v.astype(BF16)
                pieces.append(p)
                v = v - p.astype(F32)
        y = _dot(jnp.concatenate(pieces, axis=0), e_ref[...])
        return [y[(3 * n) * T:(3 * n + 1) * T] + y[(3 * n + 1) * T:(3 * n + 2) * T] + y[(3 * n + 2) * T:(3 * n + 3) * T]
                for n in range(len(vals))]

    dt = _softplus(us_ref[:, US_DT:US_DT + n_heads] + dtb_ref[...])
    d_a = dt * (-jnp.exp(alog_ref[...]))
    row = lax.broadcasted_iota(I32, (T, T), 0)
    col = lax.broadcasted_iota(I32, (T, T), 1)
    tri = (row >= col).astype(F32)
    a_cum = _dot_hi(tri, d_a)
    a_exp, dt_exp = expand([a_cum, dt], expP_ref)
    a_expt = a_exp if T == P else expand([a_cum], expT_ref)[0]
    d_a_t = _softplus(dtT_ref[0] + dtbT_ref[...]) * (-jnp.exp(alogT_ref[...]))
    r2 = lax.broadcasted_iota(I32, (2 * T, 2 * T), 0)
    c2 = lax.broadcasted_iota(I32, (2 * T, 2 * T), 1)
    tri2 = ((r2 // T == c2 // T) & (r2 <= c2)).astype(F32)
    a_cum_t = _dot_hi(d_a_t, tri2)

    l_idx = lax.broadcasted_iota(I32, (T, 2 * T), 0)
    j_idx = lax.broadcasted_iota(I32, (T, 2 * T), 1)
    causal2 = l_idx >= (j_idx % T)
    rr = lax.broadcasted_iota(I32, (2 * T, 2 * P), 0)
    cc = lax.broadcasted_iota(I32, (2 * T, 2 * P), 1)
    blockdiag = (rr // T) == (cc // P)
    chunk_decay = jnp.exp(a_cum_t)

    xdt = xs * dt_exp
    for g in range(G):
        gs = slice(g * R * P, (g + 1) * R * P)
        b_g = bc[:, g * N:(g + 1) * N].astype(BF16)
        c_g = bc[:, GN + g * N:GN + (g + 1) * N].astype(BF16)
        cb2 = _dot_nt(c_g, jnp.concatenate([b_g, b_g], axis=0))
        h_g = h_ref[0, gs, :]
        y_off = _dot_nt(c_g, h_g.astype(BF16)) * jnp.exp(a_exp[:, gs])
        pairs = []
        for pr in range(R // 2):
            i = g * (R // 2) + pr
            seg = a_expt[:, i * 2 * T:(i + 1) * 2 * T] - a_cum_t[i:i + 1, :]
            decay = jnp.exp(jnp.where(causal2, seg, -jnp.inf))
            m_pair = (cb2 * decay).astype(BF16)
            x_pair = xdt[:, i * 2 * P:(i + 1) * 2 * P]
            rhs = jnp.where(blockdiag, jnp.concatenate([x_pair, x_pair], axis=0), 0.0).astype(BF16)
            pairs.append(_dot(m_pair, rhs))
        y_g = jnp.concatenate(pairs, axis=1) + y_off + dskip_ref[:, gs] * xs[:, gs]
        a_g = a_exp[:, gs]
        dte = jnp.exp(a_g[T - 1:T, :] - a_g)
        st = _dot_tn((xdt[:, gs] * dte).astype(BF16), b_g)
        dec = jnp.concatenate(
            [jnp.broadcast_to(chunk_decay[(g * R + r) // 2:(g * R + r) // 2 + 1,
                                          ((g * R + r) % 2) * T + T - 1:((g * R + r) % 2) * T + T], (P, N))
             for r in range(R)], axis=0)
        h_ref[0, gs, :] = dec * h_g + st
        yg = y_g * _silu(z_ref[:, gs])
        ms = jnp.mean(yg * yg, axis=-1, keepdims=True)
        y_ref[:, gs] = (yg * lax.rsqrt(ms + RMS_EPS) * ng_ref[:, gs]).astype(y_ref.dtype)


def _ssd_mixer(u_ssd, u_small, dt_t, conv_prev, h0, prm, cols, *, n_batch, seq, row0, T, stacked=None):
    M = u_ssd.shape[0]
    n_heads = prm['dtb'].shape[1]
    DI = n_heads * SSD_HEAD_DIM
    BCW = conv_prev.shape[2] - DI
    nc = seq // T
    blk0 = row0 // T

    def rows(width, colblk):
        return pl.BlockSpec((T, width), lambda b, c: (blk0 + b * nc + c, colblk))

    def full(a):
        return pl.BlockSpec(a.shape, lambda b, c: (0,) * a.ndim)

    params = [prm['conv_w'], prm['conv_b'], prm['dtb'], prm['alog'], prm['dtbT'], prm['alogT'],
              prm['dskip'], prm['ng'], prm['expP'], prm['expT']]
    in_specs = [rows(DI, cols['z'] // DI), rows(DI, cols['xs'] // DI), rows(BCW, cols['bc'] // BCW),
                rows(US_W, 0),
                pl.BlockSpec((1,) + dt_t.shape[1:], lambda b, c: (b * nc + c, 0, 0)),
                pl.BlockSpec((1,) + conv_prev.shape[1:], lambda b, c: (b, 0, 0)),
                pl.BlockSpec((1,) + h0.shape[1:], lambda b, c: (b, 0, 0))] + [full(a) for a in params]
    out_specs = [rows(DI, 0),
                 pl.BlockSpec((1,) + conv_prev.shape[1:], lambda b, c: (b, 0, 0)),
                 pl.BlockSpec((1,) + h0.shape[1:], lambda b, c: (b, 0, 0))]
    call = _stacked_call(
        functools.partial(_ssd_kernel, T=T, n_heads=n_heads),
        grid=(n_batch, nc), in_specs=in_specs, out_specs=out_specs,
        out_shape=[jax.ShapeDtypeStruct((M, DI), BF16),
                   jax.ShapeDtypeStruct(conv_prev.shape, F32),
                   jax.ShapeDtypeStruct(h0.shape, F32)],
        stacked=stacked, n_stacked=1, sem=("parallel", "arbitrary"),
        scratch_shapes=[pltpu.VMEM((8, DI), F32), pltpu.VMEM((8, BCW), F32)])
    return call(u_ssd, u_ssd, u_ssd, u_small, dt_t, conv_prev, h0, *params)


def _sortable_key(x):
    b = lax.bitcast_convert_type(x, I32)
    return b ^ ((b >> 31) & 0x7FFFFFFF)


def _head_weights(us_ref):
    return us_ref[:, US_WI:US_WI + IDX_HEADS] * ((IDX_HEADS * IDX_DIM) ** -0.5)


def _indexer_scores(qi_heads, wi, ki_blk):
    Q = qi_heads[0].shape[0]
    acc = None
    for h0 in range(0, IDX_HEADS, IDX_STACK):
        l = _dot_nt(jnp.concatenate(qi_heads[h0:h0 + IDX_STACK], axis=0), ki_blk)
        for n in range(IDX_STACK):
            t = jnp.maximum(l[n * Q:(n + 1) * Q], 0.0) * wi[:, h0 + n:h0 + n + 1]
            acc = t if acc is None else acc + t
    return acc


def _kth_largest(count_ge, shape, k):
    c0 = count_ge(jnp.zeros(shape, I32))
    t0 = jnp.where(c0 >= k, 0, INT_MIN).astype(I32)

    def body(i, t):
        cand = t | (jnp.int32(1) << (30 - i))
        return jnp.where(count_ge(cand) >= k, cand, t)

    return lax.fori_loop(0, 31, body, t0)


def _tie_cutoff(count_ge, count_tie_below, thr, k, n_keys, cut_s):
    cut_s[...] = jnp.full(thr.shape, 2147483647, I32)
    n_ge = count_ge(thr)

    @pl.when(jnp.max(n_ge) > k)
    def _():
        n_gt = jnp.where(thr == 2147483647, 0, count_ge(thr + 1))
        need = k - n_gt

        def body(i, v):
            cand = v | (jnp.int32(1) << (n_keys.bit_length() - 1 - i))
            return jnp.where(count_tie_below(cand) < need, cand, v)

        v = lax.fori_loop(0, n_keys.bit_length(), body, jnp.zeros(thr.shape, I32))
        cut_s[...] = jnp.where(n_ge > k, v + 1, 2147483647)


def _selected(key, idx, thr, cut):
    return (key > INT_MIN) & ((key > thr) | ((key == thr) & (idx < cut)))


def _limits(pos0, n_rows):
    pos = pos0 + lax.broadcasted_iota(I32, (n_rows, 1), 0)
    return (pos // CHUNK + 1) * CHUNK


def _stack_q_heads(q_ref, g):
    parts = [q_ref[:, (g * Q_PER_KV + i) * HEAD_DIM:(g * Q_PER_KV + i + 1) * HEAD_DIM] for i in range(Q_PER_KV)]
    return (jnp.concatenate(parts, axis=0) * (HEAD_DIM ** -0.5 * LOG2E)).astype(BF16)


def _dsa_prompt_kernel(q_ref, qi_ref, us_ref, ki_ref, k_ref, vt_ref, o_ref, key_s, bias_s, cut_s, *,
                       n_sel, kv_heads):
    QB, KB = DSA_QB, DSA_KB
    nkb_max = key_s.shape[0]
    j = pl.program_id(1)
    nkb = ((j + 1) * QB + KB - 1) // KB
    wi = _head_weights(us_ref)
    qi_heads = [qi_ref[:, h * IDX_DIM:(h + 1) * IDX_DIM].astype(BF16) for h in range(IDX_HEADS)]
    pos = j * QB + lax.broadcasted_iota(I32, (1, QB), 1)
    lim = (pos // CHUNK + 1) * CHUNK

    def score_blk(kb, carry):
        off = pl.multiple_of(kb * KB, KB)
        sc = _indexer_scores(qi_heads, wi, ki_ref[pl.ds(off, KB), :])
        s_idx = off + lax.broadcasted_iota(I32, (KB, QB), 0)
        key_s[kb] = jnp.where(s_idx < lim, _sortable_key(sc.T), INT_MIN)
        return carry

    lax.fori_loop(0, nkb, score_blk, 0)

    def count_ge(cand):
        def body(kb, acc):
            m = (key_s[kb] >= cand).astype(I32)
            return acc + jnp.sum(m.reshape(KB // 8, 8, QB), axis=0)
        acc = lax.fori_loop(0, nkb, body, jnp.zeros((8, QB), I32))
        return jnp.sum(acc, axis=0, keepdims=True)

    thr = _kth_largest(count_ge, (1, QB), n_sel)

    def key_index(kb):
        return kb * KB + lax.broadcasted_iota(I32, (KB, QB), 0)

    def count_tie_below(c):
        def body(kb, acc):
            m = ((key_s[kb] == thr) & (key_index(kb) < c)).astype(I32)
            return acc + jnp.sum(m.reshape(KB // 8, 8, QB), axis=0)
        acc = lax.fori_loop(0, nkb, body, jnp.zeros((8, QB), I32))
        return jnp.sum(acc, axis=0, keepdims=True)

    _tie_cutoff(count_ge, count_tie_below, thr, n_sel, nkb_max * KB, cut_s)
    cut = cut_s[...]

    def bias_blk(kb, carry):
        bias_s[kb] = jnp.where(_selected(key_s[kb], key_index(kb), thr, cut), 0.0, NEG)
        return carry

    lax.fori_loop(0, nkb, bias_blk, 0)

    cols = Q_PER_KV * QB
    GPL = DSA_GROUPS_PER_LOOP
    for g0 in range(0, kv_heads, GPL):
        groups = range(g0, g0 + GPL)
        qs = [_stack_q_heads(q_ref, g) for g in groups]

        def body(kb, carry):
            off = pl.multiple_of(kb * KB, KB)
            b = bias_s[kb]
            b4 = jnp.concatenate([b] * Q_PER_KV, axis=1)
            heads = [slice(g * HEAD_DIM, (g + 1) * HEAD_DIM) for g in groups]
            vrows = [slice(g * VROWS, (g + 1) * VROWS) for g in groups]

            def logits(n):
                return _dot_nt(k_ref[pl.ds(off, KB), heads[n]], qs[n]) + b4

            t_next = logits(0)
            new = []
            for n in range(GPL):
                t = t_next
                if n + 1 < GPL:
                    t_next = logits(n + 1)
                m, acc = carry[n]
                m_new = jnp.maximum(m, jnp.max(t, axis=0, keepdims=True))
                alpha = jnp.exp2(m - m_new)
                p = jnp.exp2(t - m_new)
                acc = alpha * acc + _dot(vt_ref[0, kb, vrows[n], :], p.astype(BF16))
                new.append((m_new, acc))
            return tuple(new)

        init = tuple((jnp.full((1, cols), NEG, F32), jnp.zeros((VROWS, cols), F32)) for _ in groups)
        res = lax.fori_loop(0, nkb, body, init)
        for n, g in enumerate(groups):
            acc = res[n][1]
            out = acc[:HEAD_DIM] / acc[HEAD_DIM:HEAD_DIM + 1]
            for i in range(Q_PER_KV):
                o_ref[:, (g * Q_PER_KV + i) * HEAD_DIM:(g * Q_PER_KV + i + 1) * HEAD_DIM] = (
                    out[:, i * QB:(i + 1) * QB].T.astype(o_ref.dtype))


def _dsa_prompt(u_att, u_small, ki_b, k_b, vt_b, cols, *, n_batch, seq, kv_heads):
    M = u_att.shape[0]
    QB, KB = DSA_QB, DSA_KB
    nq = seq // QB
    nkb_max = seq // KB
    DQ = kv_heads * Q_PER_KV * HEAD_DIM
    DK = kv_heads * HEAD_DIM
    DQI = IDX_HEADS * IDX_DIM
    n_sel = min(TOPK_MAX, seq // 4)
    return pl.pallas_call(
        functools.partial(_dsa_prompt_kernel, n_sel=n_sel, kv_heads=kv_heads),
        grid=(n_batch, nq),
        in_specs=[pl.BlockSpec((QB, DQ), lambda b, j: (b * nq + j, cols['q'] // DQ)),
                  pl.BlockSpec((QB, DQI), lambda b, j: (b * nq + j, cols['qi'] // DQI)),
                  pl.BlockSpec((QB, US_W), lambda b, j: (b * nq + j, 0)),
                  pl.BlockSpec((seq, IDX_DIM), lambda b, j: (b, 0)),
                  pl.BlockSpec((seq, DK), lambda b, j: (b, 0)),
                  pl.BlockSpec((1, nkb_max, kv_heads * VROWS, KB), lambda b, j: (b, 0, 0, 0))],
        out_specs=pl.BlockSpec((QB, DQ), lambda b, j: (b * nq + j, 0)),
        out_shape=jax.ShapeDtypeStruct((M, DQ), BF16),
        scratch_shapes=[pltpu.VMEM((nkb_max, KB, QB), I32), pltpu.VMEM((nkb_max, KB, QB), F32),
                        pltpu.VMEM((1, QB), I32)],
        compiler_params=_cparams(("parallel", "arbitrary")),
    )(u_att, u_att, u_small, ki_b, k_b, vt_b)


def _dsa_sample_kernel(q_ref, qi_ref, us_ref, kn_ref, vn_ref, ck_ref, cv_ref, cki_ref, o_ref, cut_s, *,
                       n_sel, kv_heads, past):
    Q = q_ref.shape[0]
    wi = _head_weights(us_ref)
    qi_heads = [qi_ref[:, h * IDX_DIM:(h + 1) * IDX_DIM].astype(BF16) for h in range(IDX_HEADS)]
    lim = _limits(past, Q)
    ki_new = us_ref[:, US_KI:US_KI + IDX_DIM].astype(BF16)
    sc_p = _indexer_scores(qi_heads, wi, cki_ref[0].astype(BF16))
    sc_n = _indexer_scores(qi_heads, wi, ki_new)
    idx_p = lax.broadcasted_iota(I32, (Q, past), 1)
    idx_n = past + lax.broadcasted_iota(I32, (Q, Q), 1)
    key_p = jnp.where(idx_p < lim, _sortable_key(sc_p), INT_MIN)
    key_n = jnp.where(idx_n < lim, _sortable_key(sc_n), INT_MIN)

    def count_ge(cand):
        return (jnp.sum((key_p >= cand).astype(I32), axis=1, keepdims=True)
                + jnp.sum((key_n >= cand).astype(I32), axis=1, keepdims=True))

    thr = _kth_largest(count_ge, (Q, 1), n_sel)

    def count_tie_below(c):
        return (jnp.sum(((key_p == thr) & (idx_p < c)).astype(I32), axis=1, keepdims=True)
                + jnp.sum(((key_n == thr) & (idx_n < c)).astype(I32), axis=1, keepdims=True))

    _tie_cutoff(count_ge, count_tie_below, thr, n_sel, past + Q, cut_s)
    cut = cut_s[...]
    bias_p = jnp.where(_selected(key_p, idx_p, thr, cut), 0.0, NEG)
    bias_n = jnp.where(_selected(key_n, idx_n, thr, cut), 0.0, NEG)
    bias_p = jnp.concatenate([bias_p] * Q_PER_KV, axis=0)
    bias_n = jnp.concatenate([bias_n] * Q_PER_KV, axis=0)

    for g in range(kv_heads):
        hs = slice(g * HEAD_DIM, (g + 1) * HEAD_DIM)
        qs = _stack_q_heads(q_ref, g)
        ck = ck_ref[0, pl.ds(g, past, stride=kv_heads), :].astype(BF16)
        cv = cv_ref[0, pl.ds(g, past, stride=kv_heads), :].astype(BF16)
        t_p = _dot_nt(qs, ck) + bias_p
        t_n = _dot_nt(qs, kn_ref[:, hs].astype(BF16)) + bias_n
        m = jnp.maximum(jnp.max(t_p, axis=-1, keepdims=True), jnp.max(t_n, axis=-1, keepdims=True))
        p_p = jnp.exp2(t_p - m)
        p_n = jnp.exp2(t_n - m)
        l = jnp.sum(p_p, axis=-1, keepdims=True) + jnp.sum(p_n, axis=-1, keepdims=True)
        acc = _dot(p_p.astype(BF16), cv) + _dot(p_n.astype(BF16), vn_ref[:, hs].astype(BF16))
        out = acc / l
        for i in range(Q_PER_KV):
            o_ref[:, (g * Q_PER_KV + i) * HEAD_DIM:(g * Q_PER_KV + i + 1) * HEAD_DIM] = (
                out[i * Q:(i + 1) * Q].astype(o_ref.dtype))


def _dsa_sample(u_att, u_small, cache_k, cache_v, cache_ki, cols, *, n_batch, seq, row0, kv_heads, stacked):
    past = cache_ki.shape[1]
    DQ = kv_heads * Q_PER_KV * HEAD_DIM
    DK = kv_heads * HEAD_DIM
    DQI = IDX_HEADS * IDX_DIM
    n_sel = min(TOPK_MAX, (past + seq) // 4)
    blk0 = row0 // seq
    call = _stacked_call(
        functools.partial(_dsa_sample_kernel, n_sel=n_sel, kv_heads=kv_heads, past=past),
        grid=(n_batch,),
        in_specs=[pl.BlockSpec((seq, DQ), lambda b: (blk0 + b, cols['q'] // DQ)),
                  pl.BlockSpec((seq, DQI), lambda b: (blk0 + b, cols['qi'] // DQI)),
                  pl.BlockSpec((seq, US_W), lambda b: (blk0 + b, 0)),
                  pl.BlockSpec((seq, DK), lambda b: (blk0 + b, cols['k'] // DK)),
                  pl.BlockSpec((seq, DK), lambda b: (blk0 + b, cols['v'] // DK)),
                  pl.BlockSpec((1, past * kv_heads, HEAD_DIM), lambda b: (b, 0, 0)),
                  pl.BlockSpec((1, past * kv_heads, HEAD_DIM), lambda b: (b, 0, 0)),
                  pl.BlockSpec((1, past, IDX_DIM), lambda b: (b, 0, 0))],
        out_specs=[pl.BlockSpec((seq, DQ), lambda b: (blk0 + b, 0))],
        out_shape=[jax.ShapeDtypeStruct(stacked[0].shape, BF16)],
        stacked=stacked, n_stacked=1, sem=("parallel",), scratch_shapes=[pltpu.VMEM((seq, 1), I32)])
    return call(u_att, u_att, u_small, u_att, u_att, cache_k, cache_v, cache_ki)[0]


def _pair_layout(v, T):
    return jnp.repeat(v.reshape(-1, 2), T, axis=1)


def _dt_transposed(dt_raw, T):
    n, H = dt_raw.shape
    return dt_raw.reshape(n // T, T, H // 2, 2).transpose(0, 2, 3, 1).reshape(n // T, H // 2, 2 * T)


def _ssd_params(conv_w, conv_b, dt_bias, a_log, d_skip, norm_g, T):
    H = dt_bias.shape[0]
    P = SSD_HEAD_DIM
    eye = jnp.eye(H, dtype=F32)
    return dict(conv_w=conv_w, conv_b=conv_b.reshape(1, -1),
                dtb=dt_bias.reshape(1, H), alog=a_log.reshape(1, H),
                dtbT=_pair_layout(dt_bias, T), alogT=_pair_layout(a_log, T),
                dskip=jnp.repeat(d_skip, P).reshape(1, H * P), ng=norm_g.reshape(1, -1),
                expP=jnp.repeat(eye, P, axis=1).astype(BF16), expT=jnp.repeat(eye, T, axis=1).astype(BF16))


def kernel(x_prompt, x_sample, mem_prompt, cache_k, cache_v, cache_idx_k, cache_mem_k, cache_mem_v,
           state_ssm, state_conv, ln1_g, ln1_b, ffn1_w1, ffn1_w3, ffn1_w2, w_in, conv_w, conv_b,
           dt_bias, a_log, d_skip, ssd_norm_g, w_ssd_br, w_att_br, w_out, ln2_g, ln2_b,
           w_mq, w_mk, w_mv, w_mo, ln3_g, ln3_b, ffn2_w1, ffn2_w3, ffn2_w2, ln4_g, ln4_b):
    assert x_prompt.ndim == 3 and ln1_g.shape[0] == DEPTH == 1
    l = 0
    BP, LP, D = x_prompt.shape
    BS, LS, _ = x_sample.shape
    MP, MS = BP * LP, BS * LS
    M = MP + MS
    H = dt_bias.shape[1]
    assert H == SSD_HEADS
    DI = H * SSD_HEAD_DIM
    BCW = conv_w.shape[2] - DI
    KVH = cache_k.shape[3]
    DQ = KVH * Q_PER_KV * HEAD_DIM
    DK = KVH * HEAD_DIM
    DQI = IDX_HEADS * IDX_DIM
    MT = mem_prompt.shape[1]
    PAST = cache_k.shape[2]
    tm_big = M // 8
    tm_half = M // 16

    w0t = w_in[l].T
    c_dt = 2 * DI + BCW
    c_q = c_dt + H
    c_wi = c_q + DQ + 2 * DK + DQI
    c_ki = c_wi + IDX_HEADS
    c_g = c_ki + IDX_DIM
    assert c_g + 2 * D == w0t.shape[0]
    cols_ssd = dict(z=0, xs=DI, bc=2 * DI)
    cols_att = dict(q=0, k=DQ, v=DQ + DK, qi=DQ + 2 * DK)
    n_small = H + IDX_HEADS + IDX_DIM
    assert n_small <= US_W
    w_small_t = jnp.concatenate([w0t[c_dt:c_q], w0t[c_wi:c_g], jnp.zeros((US_W - n_small, D), F32)], axis=0)

    x_p2, x_s2 = x_prompt.reshape(MP, D), x_sample.reshape(MS, D)
    x0b = _cast_rows(x_p2, out_rows=M, out_row0=0)
    x0b, = _cast_rows(x_s2, out_rows=M, out_row0=MP, stacked=x0b)

    def ffn_delta(xb, w1, w3, w2, residual=None):
        h = _ffn_up(xb, w1[l], w3[l], tm=M // 4, tn=256)
        return _matmul(h, w2[l].astype(BF16), tm=tm_big, tn=256, out_dtype=F32, single_buffer_x=True,
                       residual=residual, scale=0.5)

    d1 = ffn_delta(x0b, ffn1_w1, ffn1_w3, ffn1_w2)
    ln1 = _res_layer_norm(x_p2, d1, ln1_g[l], ln1_b[l], scale=0.5, n_rows=MP, out_rows=M)
    x1, x1b = _res_layer_norm(x_s2, d1, ln1_g[l], ln1_b[l], scale=0.5, n_rows=MS, d_row0=MP, out_row0=MP,
                              stacked=ln1)

    u_ssd = _matmul_nt(x1b, w0t, row0=0, n_rows=c_dt, tm=M // 4, tn=512, out_dtype=F32)
    u_att = _matmul_nt(x1b, w0t, row0=c_q, n_rows=c_wi - c_q, tm=M // 4, tn=512, out_dtype=F32)
    u_gate = _matmul_nt(x1b, w0t, row0=c_g, n_rows=2 * D, tm=M // 4, tn=512, out_dtype=F32)
    u_small = _matmul_nt(x1b, w_small_t, row0=0, n_rows=US_W, tm=tm_big, tn=US_W, out_dtype=F32)

    dt_raw = u_small[:, US_DT:US_DT + H]
    zeros_conv = jnp.zeros((BP,) + state_conv.shape[2:], F32)
    zeros_h = jnp.zeros((BP, DI, SSD_STATE), F32)
    prm_p = _ssd_params(conv_w[l], conv_b[l], dt_bias[l], a_log[l], d_skip[l], ssd_norm_g[l], CHUNK)
    y_ssd, conv_p, h_p = _ssd_mixer(u_ssd, u_small, _dt_transposed(dt_raw[:MP], CHUNK), zeros_conv, zeros_h,
                                    prm_p, cols_ssd, n_batch=BP, seq=LP, row0=0, T=CHUNK)
    TS = min(CHUNK, LS)
    prm_s = _ssd_params(conv_w[l], conv_b[l], dt_bias[l], a_log[l], d_skip[l], ssd_norm_g[l], TS)
    y_ssd, conv_s, h_s = _ssd_mixer(u_ssd, u_small, _dt_transposed(dt_raw[MP:], TS), state_conv[l],
                                    state_ssm[l].reshape(BS, DI, SSD_STATE),
                                    prm_s, cols_ssd, n_batch=BS, seq=LS, row0=MP, T=TS, stacked=(y_ssd,))

    k_all = u_att[:, cols_att['k']:cols_att['k'] + DK]
    v_all = u_att[:, cols_att['v']:cols_att['v'] + DK]
    ki_all = u_small[:, US_KI:US_KI + IDX_DIM]
    nkb = LP // DSA_KB
    vt5 = v_all[:MP].astype(BF16).reshape(BP, nkb, DSA_KB, KVH, HEAD_DIM).transpose(0, 1, 3, 4, 2)
    ones_row = (lax.broadcasted_iota(I32, (BP, nkb, KVH, VPAD, DSA_KB), 3) == 0).astype(BF16)
    vt_b = jnp.concatenate([vt5, ones_row], axis=3).reshape(BP, nkb, KVH * VROWS, DSA_KB)
    y_att = _dsa_prompt(u_att, u_small, ki_all[:MP].astype(BF16), k_all[:MP].astype(BF16),
                        vt_b, cols_att, n_batch=BP, seq=LP, kv_heads=KVH)
    y_att = _dsa_sample(u_att, u_small, cache_k[l].reshape(BS, PAST * KVH, HEAD_DIM),
                        cache_v[l].reshape(BS, PAST * KVH, HEAD_DIM), cache_idx_k[l], cols_att,
                        n_batch=BS, seq=LS, row0=MP, kv_heads=KVH, stacked=(y_att,))

    merged = _gated_merge(y_ssd, y_att, w_ssd_br[l], w_att_br[l], u_gate, 0, D, tm=tm_big, tn=256)
    s2 = _matmul(merged, w_out[l], tm=tm_big, tn=512, out_dtype=F32, residual=x1)
    x2, x2b = _res_layer_norm(s2, None, ln2_g[l], ln2_b[l], n_rows=M)

    memb = mem_prompt.reshape(BP * MT, D).astype(BF16)
    mk_p = _matmul(memb, w_mk[l], tm=BP * MT, tn=512, out_dtype=F32)
    mv_p = _matmul(memb, w_mv[l], tm=BP * MT, tn=512, out_dtype=F32)
    qm = _matmul(x2b, w_mq[l], tm=tm_big, tn=512, out_dtype=BF16)
    o_m = _memory_attention(qm, mk_p, mv_p, n_batch=BP, rows_per_batch=LP, row0=0, tq=min(512, LP))
    o_m = _memory_attention(qm, cache_mem_k[l].reshape(BS * MT, D), cache_mem_v[l].reshape(BS * MT, D),
                            n_batch=BS, rows_per_batch=LS, row0=MP, tq=LS, stacked=(o_m,))
    s3 = _matmul(o_m, w_mo[l], tm=tm_big, tn=512, out_dtype=F32, residual=x2)
    x3, x3b = _res_layer_norm(s3, None, ln3_g[l], ln3_b[l], n_rows=M)

    s4 = ffn_delta(x3b, ffn2_w1, ffn2_w3, ffn2_w2, residual=x3)
    y_p, = _res_layer_norm(s4, None, ln4_g[l], ln4_b[l], n_rows=MP, emit_bf16=False)
    y_s, = _res_layer_norm(s4, None, ln4_g[l], ln4_b[l], n_rows=MS, x_row0=MP, emit_bf16=False)

    mh = D // MEM_HEADS
    return (y_p.reshape(BP, LP, D), y_s.reshape(BS, LS, D),
            h_p.reshape(1, BP, H, SSD_HEAD_DIM, SSD_STATE), conv_p[None],
            k_all[:MP].reshape(1, BP, LP, KVH, HEAD_DIM), v_all[:MP].reshape(1, BP, LP, KVH, HEAD_DIM),
            ki_all[:MP].reshape(1, BP, LP, IDX_DIM),
            mk_p.reshape(1, BP, MT, MEM_HEADS, mh), mv_p.reshape(1, BP, MT, MEM_HEADS, mh),
            h_s.reshape(1, BS, H, SSD_HEAD_DIM, SSD_STATE), conv_s[None],
            k_all[MP:].reshape(1, BS, LS, KVH, HEAD_DIM), v_all[MP:].reshape(1, BS, LS, KVH, HEAD_DIM),
            ki_all[MP:].reshape(1, BS, LS, IDX_DIM))
```

```python
import functools
import math

import jax
import jax.numpy as jnp
from jax import lax
from jax.experimental import pallas as pl
from jax.experimental.pallas import tpu as pltpu

F32 = jnp.float32
BF16 = jnp.bfloat16
I32 = jnp.int32

DEPTH = 1
CHUNK = 64
SSD_HEADS = 64
SSD_HEAD_DIM = 64
SSD_HEADS_PER_GROUP = 8
SSD_STATE = 128
SSD_CONV_W = 4
HEAD_DIM = 128
Q_PER_KV = 4
IDX_HEADS = 16
IDX_DIM = 64
TOPK_MAX = 256
MEM_HEADS = 4
ALPHA = (2.0 * DEPTH) ** 0.25
LN_EPS = 1e-5
RMS_EPS = 1e-5

LANES = 128
VMEM_LIMIT = 56 * 1024 * 1024

NEG = -1e30
INT_MIN = -2147483648
LOG2E = math.log2(math.e)

DSA_QB = 128
DSA_KB = 512
DSA_GROUPS_PER_LOOP = 8
IDX_STACK = 4
VPAD = 16
VROWS = HEAD_DIM + VPAD
LN_ROWS = 256

US_DT = 0
US_WI = US_DT + SSD_HEADS
US_KI = US_WI + IDX_HEADS
US_W = 256


def _cparams(sem):
    return pltpu.CompilerParams(dimension_semantics=sem, vmem_limit_bytes=VMEM_LIMIT)


def _dot(a, b):
    return jnp.dot(a, b, preferred_element_type=F32)


def _dot_nt(a, b):
    return lax.dot_general(a, b, (((1,), (1,)), ((), ())), preferred_element_type=F32)


def _dot_tn(a, b, precision=None):
    return lax.dot_general(a, b, (((0,), (0,)), ((), ())), preferred_element_type=F32,
                           precision=precision)


def _dot_hi(a, b):
    return jnp.dot(a, b, preferred_element_type=F32, precision=lax.Precision.HIGHEST)


def _sigmoid(x):
    return 1.0 / (1.0 + jnp.exp(-x))


def _silu(x):
    return x * _sigmoid(x)


def _softplus(x):
    return jnp.maximum(x, 0.0) + jnp.log1p(jnp.exp(-jnp.abs(x)))


def _stacked_call(kernel, *, grid, in_specs, out_specs, out_shape, stacked, n_stacked, sem, scratch_shapes=()):
    if stacked is None:
        return pl.pallas_call(kernel, grid=grid, in_specs=in_specs, out_specs=out_specs, out_shape=out_shape,
                              scratch_shapes=scratch_shapes, compiler_params=_cparams(sem))
    n_in = len(in_specs)

    def body(*refs):
        kernel(*refs[:n_in], *refs[n_in + n_stacked:])

    call = pl.pallas_call(
        body, grid=grid,
        in_specs=list(in_specs) + [pl.BlockSpec(memory_space=pl.ANY)] * n_stacked,
        out_specs=out_specs, out_shape=out_shape, scratch_shapes=scratch_shapes,
        input_output_aliases={n_in + i: i for i in range(n_stacked)},
        compiler_params=_cparams(sem))
    return lambda *args: call(*args, *stacked)


def _mm_kernel(x_ref, w_ref, o_ref):
    o_ref[...] = _dot(x_ref[...], w_ref[...].astype(BF16)).astype(o_ref.dtype)


def _mm_res_kernel(x_ref, w_ref, r_ref, o_ref, *, scale):
    o_ref[...] = ALPHA * r_ref[...] + scale * _dot(x_ref[...], w_ref[...].astype(BF16))


def _matmul(x, w, *, tm, tn, out_dtype, single_buffer_x=False, residual=None, scale=1.0):
    M, K = x.shape
    N = w.shape[1]
    assert N % tn == 0 and M % tm == 0
    xmode = dict(pipeline_mode=pl.Buffered(1)) if single_buffer_x else {}
    tile = pl.BlockSpec((tm, tn), lambda i, j: (i, j))
    in_specs = [pl.BlockSpec((tm, K), lambda i, j: (i, 0), **xmode),
                pl.BlockSpec((K, tn), lambda i, j: (0, j))]
    if residual is None:
        body, args = _mm_kernel, (x, w)
    else:
        assert out_dtype == F32
        body, args = functools.partial(_mm_res_kernel, scale=scale), (x, w, residual)
        in_specs.append(tile)
    return pl.pallas_call(
        body,
        grid=(M // tm, N // tn),
        in_specs=in_specs,
        out_specs=tile,
        out_shape=jax.ShapeDtypeStruct((M, N), out_dtype),
        compiler_params=_cparams(("parallel", "arbitrary")),
    )(*args)


def _mm_nt_kernel(x_ref, wt_ref, o_ref):
    o_ref[...] = _dot_nt(x_ref[...], wt_ref[...].astype(BF16)).astype(o_ref.dtype)


def _matmul_nt(x, wt, *, row0, n_rows, tm, tn, out_dtype):
    M, K = x.shape
    assert n_rows % tn == 0 and M % tm == 0 and row0 % 8 == 0
    return pl.pallas_call(
        _mm_nt_kernel,
        grid=(M // tm, n_rows // tn),
        in_specs=[pl.BlockSpec((tm, K), lambda i, j: (i, 0), pipeline_mode=pl.Buffered(1)),
                  pl.BlockSpec((pl.Element(tn), pl.Element(K)), lambda i, j: (pl.multiple_of(row0 + j * tn, 8), 0))],
        out_specs=pl.BlockSpec((tm, tn), lambda i, j: (i, j)),
        out_shape=jax.ShapeDtypeStruct((M, n_rows), out_dtype),
        compiler_params=_cparams(("parallel", "arbitrary")),
    )(x, wt)


def _cast_kernel(x_ref, o_ref):
    o_ref[...] = x_ref[...].astype(o_ref.dtype)


def _cast_rows(x, *, out_rows, out_row0, stacked=None):
    n, D = x.shape
    tr = max(r for r in range(16, LN_ROWS + 1, 16) if n % r == 0 and out_row0 % r == 0)
    ob = out_row0 // tr
    call = _stacked_call(
        _cast_kernel, grid=(n // tr,),
        in_specs=[pl.BlockSpec((tr, D), lambda i: (i, 0))],
        out_specs=[pl.BlockSpec((tr, D), lambda i: (ob + i, 0))],
        out_shape=[jax.ShapeDtypeStruct((out_rows, D), BF16)],
        stacked=stacked, n_stacked=1, sem=("parallel",))
    return call(x)


def _ffn_up_kernel(x_ref, w1_ref, w3_ref, o_ref):
    x = x_ref[...]
    a = _dot(x, w1_ref[...].astype(BF16))
    b = _dot(x, w3_ref[...].astype(BF16))
    o_ref[...] = (_silu(a) * b).astype(o_ref.dtype)


def _ffn_up(x, w1, w3, *, tm, tn):
    M, K = x.shape
    N = w1.shape[1]
    return pl.pallas_call(
        _ffn_up_kernel,
        grid=(M // tm, N // tn),
        in_specs=[pl.BlockSpec((tm, K), lambda i, j: (i, 0), pipeline_mode=pl.Buffered(1)),
                  pl.BlockSpec((K, tn), lambda i, j: (0, j)),
                  pl.BlockSpec((K, tn), lambda i, j: (0, j))],
        out_specs=pl.BlockSpec((tm, tn), lambda i, j: (i, j)),
        out_shape=jax.ShapeDtypeStruct((M, N), BF16),
        compiler_params=_cparams(("parallel", "arbitrary")),
    )(x, w1, w3)


def _gate_kernel(ys_ref, ya_ref, ws_ref, wa_ref, gs_ref, ga_ref, o_ref):
    s = _dot(ys_ref[...], ws_ref[...].astype(BF16))
    a = _dot(ya_ref[...], wa_ref[...].astype(BF16))
    o_ref[...] = (_sigmoid(gs_ref[...]) * s + _sigmoid(ga_ref[...]) * a).astype(o_ref.dtype)


def _gated_merge(y_ssd, y_att, w_ssd, w_att, u_gate, gs_col, ga_col, *, tm, tn):
    M, K = y_ssd.shape
    N = w_ssd.shape[1]
    gs_blk, ga_blk = gs_col // tn, ga_col // tn
    once = dict(pipeline_mode=pl.Buffered(1))
    return pl.pallas_call(
        _gate_kernel,
        grid=(M // tm, N // tn),
        in_specs=[pl.BlockSpec((tm, K), lambda i, j: (i, 0), **once),
                  pl.BlockSpec((tm, K), lambda i, j: (i, 0), **once),
                  pl.BlockSpec((K, tn), lambda i, j: (0, j)),
                  pl.BlockSpec((K, tn), lambda i, j: (0, j)),
                  pl.BlockSpec((tm, tn), lambda i, j: (i, gs_blk + j)),
                  pl.BlockSpec((tm, tn), lambda i, j: (i, ga_blk + j))],
        out_specs=pl.BlockSpec((tm, tn), lambda i, j: (i, j)),
        out_shape=jax.ShapeDtypeStruct((M, N), BF16),
        compiler_params=_cparams(("parallel", "arbitrary")),
    )(y_ssd, y_att, w_ssd, w_att, u_gate, u_gate)


def _layer_norm_rows(y, g, b):
    mu = jnp.mean(y, axis=-1, keepdims=True)
    yc = y - mu
    var = jnp.mean(yc * yc, axis=-1, keepdims=True)
    return yc * lax.rsqrt(var + LN_EPS) * g + b


def _ln_kernel(x_ref, d_ref, g_ref, b_ref, *o_refs, scale):
    o = _layer_norm_rows(ALPHA * x_ref[...] + scale * d_ref[...], g_ref[...], b_ref[...])
    for o_ref in o_refs:
        o_ref[...] = o.astype(o_ref.dtype)


def _ln_presummed_kernel(y_ref, g_ref, b_ref, *o_refs):
    o = _layer_norm_rows(y_ref[...], g_ref[...], b_ref[...])
    for o_ref in o_refs:
        o_ref[...] = o.astype(o_ref.dtype)


def _res_layer_norm(x, delta, g, b, *, scale=None, n_rows, x_row0=0, d_row0=0, out_rows=None, out_row0=0,
                    stacked=None, emit_bf16=True):
    D = x.shape[1]
    out_rows = n_rows if out_rows is None else out_rows
    if stacked is not None:
        out_rows = stacked[0].shape[0]
    tr = max(r for r in range(8, LN_ROWS + 1, 8)
             if all(v % r == 0 for v in (n_rows, x_row0, d_row0, out_row0)))
    xb, db, ob = x_row0 // tr, d_row0 // tr, out_row0 // tr
    vec = pl.BlockSpec((1, D), lambda i: (0, 0))
    dts = (F32, BF16) if emit_bf16 else (F32,)
    if delta is None:
        body, args = _ln_presummed_kernel, (x,)
        in_specs = [pl.BlockSpec((tr, D), lambda i: (xb + i, 0)), vec, vec]
    else:
        body, args = functools.partial(_ln_kernel, scale=scale), (x, delta)
        in_specs = [pl.BlockSpec((tr, D), lambda i: (xb + i, 0)), pl.BlockSpec((tr, D), lambda i: (db + i, 0)), vec, vec]
    call = _stacked_call(
        body, grid=(n_rows // tr,), in_specs=in_specs,
        out_specs=[pl.BlockSpec((tr, D), lambda i: (ob + i, 0)) for _ in dts],
        out_shape=[jax.ShapeDtypeStruct((out_rows, D), dt) for dt in dts],
        stacked=stacked, n_stacked=len(dts), sem=("parallel",))
    return call(*args, g.reshape(1, D), b.reshape(1, D))


def _memattn_kernel(q_ref, k_ref, v_ref, o_ref):
    dh = q_ref.shape[1] // MEM_HEADS
    scale = dh ** -0.5
    for h in range(MEM_HEADS):
        sl = slice(h * dh, (h + 1) * dh)
        q = q_ref[:, sl]
        k = k_ref[:, sl].astype(BF16)
        v = v_ref[:, sl].astype(BF16)
        s = _dot_nt(q, k) * scale
        m = jnp.max(s, axis=-1, keepdims=True)
        p = jnp.exp(s - m)
        p = p / jnp.sum(p, axis=-1, keepdims=True)
        o_ref[:, sl] = _dot(p.astype(BF16), v).astype(o_ref.dtype)


def _memory_attention(q, mem_k, mem_v, *, n_batch, rows_per_batch, row0, tq, stacked=None):
    M, D = q.shape
    mt = mem_k.shape[0] // n_batch
    nq = rows_per_batch // tq
    blk0 = row0 // tq
    qspec = pl.BlockSpec((tq, D), lambda b, j: (blk0 + b * nq + j, 0))
    mspec = pl.BlockSpec((mt, D), lambda b, j: (b, 0))
    call = _stacked_call(
        _memattn_kernel, grid=(n_batch, nq), in_specs=[qspec, mspec, mspec], out_specs=[qspec],
        out_shape=[jax.ShapeDtypeStruct((M, D), BF16)], stacked=stacked, n_stacked=1,
        sem=("parallel", "arbitrary"))
    return call(q, mem_k, mem_v)[0]


def _ssd_kernel(z_ref, xs_ref, bc_ref, us_ref, dtT_ref, cprev_ref, h0_ref,
                convw_ref, convb_ref, dtb_ref, alog_ref, dtbT_ref, alogT_ref, dskip_ref, ng_ref,
                expP_ref, expT_ref,
                y_ref, cnew_ref, h_ref, extx_s, extbc_s, *, T, n_heads):
    P, R, N = SSD_HEAD_DIM, SSD_HEADS_PER_GROUP, SSD_STATE
    G = n_heads // R
    DI = n_heads * P
    GN = G * N
    c = pl.program_id(1)
    W1 = SSD_CONV_W - 1
    base = 8 - W1

    @pl.when(c == 0)
    def _():
        extx_s[0:base, :] = jnp.zeros((base, DI), F32)
        extbc_s[0:base, :] = jnp.zeros((base, 2 * GN), F32)
        extx_s[base:8, :] = cprev_ref[0, :, :DI]
        extbc_s[base:8, :] = cprev_ref[0, :, DI:]
        h_ref[0] = h0_ref[0]

    def conv(hist_s, x, lo, hi):
        xx = jnp.concatenate([hist_s[...], x], axis=0)
        out = convb_ref[:, lo:hi]
        for j in range(SSD_CONV_W):
            tap = xx if j == W1 else pltpu.roll(xx, W1 - j, axis=0)
            out = out + convw_ref[j:j + 1, lo:hi] * tap[8:8 + T]
        return _silu(out)

    x_in = xs_ref[...]
    bc_in = bc_ref[...]
    xs = conv(extx_s, x_in, 0, DI)
    bc = conv(extbc_s, bc_in, DI, DI + 2 * GN)
    newx = x_in[T - W1:T]
    newbc = bc_in[T - W1:T]
    cnew_ref[0, :, :DI] = newx
    cnew_ref[0, :, DI:] = newbc
    extx_s[base:8, :] = newx
    extbc_s[base:8, :] = newbc

    def expand(vals, e_ref):
        pieces = []
        for v in vals:
            for _ in range(3):
                p = v.astype(BF16)
                pieces.append(p)
                v = v - p.astype(F32)
        y = _dot(jnp.concatenate(pieces, axis=0), e_ref[...])
        return [y[(3 * n) * T:(3 * n + 1) * T] + y[(3 * n + 1) * T:(3 * n + 2) * T] + y[(3 * n + 2) * T:(3 * n + 3) * T]
                for n in range(len(vals))]

    dt = _softplus(us_ref[:, US_DT:US_DT + n_heads] + dtb_ref[...])
    d_a = dt * (-jnp.exp(alog_ref[...]))
    row = lax.broadcasted_iota(I32, (T, T), 0)
    col = lax.broadcasted_iota(I32, (T, T), 1)
    tri = (row >= col).astype(F32)
    a_cum = _dot_hi(tri, d_a)
    a_exp, dt_exp = expand([a_cum, dt], expP_ref)
    a_expt = a_exp if T == P else expand([a_cum], expT_ref)[0]
    d_a_t = _softplus(dtT_ref[0] + dtbT_ref[...]) * (-jnp.exp(alogT_ref[...]))
    r2 = lax.broadcasted_iota(I32, (2 * T, 2 * T), 0)
    c2 = lax.broadcasted_iota(I32, (2 * T, 2 * T), 1)
    tri2 = ((r2 // T == c2 // T) & (r2 <= c2)).astype(F32)
    a_cum_t = _dot_hi(d_a_t, tri2)

    l_idx = lax.broadcasted_iota(I32, (T, 2 * T), 0)
    j_idx = lax.broadcasted_iota(I32, (T, 2 * T), 1)
    causal2 = l_idx >= (j_idx % T)
    rr = lax.broadcasted_iota(I32, (2 * T, 2 * P), 0)
    cc = lax.broadcasted_iota(I32, (2 * T, 2 * P), 1)
    blockdiag = (rr // T) == (cc // P)
    chunk_decay = jnp.exp(a_cum_t)

    xdt = xs * dt_exp
    for g in range(G):
        gs = slice(g * R * P, (g + 1) * R * P)
        b_g = bc[:, g * N:(g + 1) * N].astype(BF16)
        c_g = bc[:, GN + g * N:GN + (g + 1) * N].astype(BF16)
        cb2 = _dot_nt(c_g, jnp.concatenate([b_g, b_g], axis=0))
        h_g = h_ref[0, gs, :]
        y_off = _dot_nt(c_g, h_g.astype(BF16)) * jnp.exp(a_exp[:, gs])
        pairs = []
        for pr in range(R // 2):
            i = g * (R // 2) + pr
            seg = a_expt[:, i * 2 * T:(i + 1) * 2 * T] - a_cum_t[i:i + 1, :]
            decay = jnp.exp(jnp.where(causal2, seg, -jnp.inf))
            m_pair = (cb2 * decay).astype(BF16)
            x_pair = xdt[:, i * 2 * P:(i + 1) * 2 * P]
            rhs = jnp.where(blockdiag, jnp.concatenate([x_pair, x_pair], axis=0), 0.0).astype(BF16)
            pairs.append(_dot(m_pair, rhs))
        y_g = jnp.concatenate(pairs, axis=1) + y_off + dskip_ref[:, gs] * xs[:, gs]
        a_g = a_exp[:, gs]
        dte = jnp.exp(a_g[T - 1:T, :] - a_g)
        st = _dot_tn((xdt[:, gs] * dte).astype(BF16), b_g)
        dec = jnp.concatenate(
            [jnp.broadcast_to(chunk_decay[(g * R + r) // 2:(g * R + r) // 2 + 1,
                                          ((g * R + r) % 2) * T + T - 1:((g * R + r) % 2) * T + T], (P, N))
             for r in range(R)], axis=0)
        h_ref[0, gs, :] = dec * h_g + st
        yg = y_g * _silu(z_ref[:, gs])
        ms = jnp.mean(yg * yg, axis=-1, keepdims=True)
        y_ref[:, gs] = (yg * lax.rsqrt(ms + RMS_EPS) * ng_ref[:, gs]).astype(y_ref.dtype)


def _ssd_mixer(u_ssd, u_small, dt_t, conv_prev, h0, prm, cols, *, n_batch, seq, row0, T, stacked=None):
    M = u_ssd.shape[0]
    n_heads = prm['dtb'].shape[1]
    DI = n_heads * SSD_HEAD_DIM
    BCW = conv_prev.shape[2] - DI
    nc = seq // T
    blk0 = row0 // T

    def rows(width, colblk):
        return pl.BlockSpec((T, width), lambda b, c: (blk0 + b * nc + c, colblk))

    def full(a):
        return pl.BlockSpec(a.shape, lambda b, c: (0,) * a.ndim)

    params = [prm['conv_w'], prm['conv_b'], prm['dtb'], prm['alog'], prm['dtbT'], prm['alogT'],
              prm['dskip'], prm['ng'], prm['expP'], prm['expT']]
    in_specs = [rows(DI, cols['z'] // DI), rows(DI, cols['xs'] // DI), rows(BCW, cols['bc'] // BCW),
                rows(US_W, 0),
                pl.BlockSpec((1,) + dt_t.shape[1:], lambda b, c: (b * nc + c, 0, 0)),
                pl.BlockSpec((1,) + conv_prev.shape[1:], lambda b, c: (b, 0, 0)),
                pl.BlockSpec((1,) + h0.shape[1:], lambda b, c: (b, 0, 0))] + [full(a) for a in params]
    out_specs = [rows(DI, 0),
                 pl.BlockSpec((1,) + conv_prev.shape[1:], lambda b, c: (b, 0, 0)),
                 pl.BlockSpec((1,) + h0.shape[1:], lambda b, c: (b, 0, 0))]
    call = _stacked_call(
        functools.partial(_ssd_kernel, T=T, n_heads=n_heads),
        grid=(n_batch, nc), in_specs=in_specs, out_specs=out_specs,
        out_shape=[jax.ShapeDtypeStruct((M, DI), BF16),
                   jax.ShapeDtypeStruct(conv_prev.shape, F32),
                   jax.ShapeDtypeStruct(h0.shape, F32)],
        stacked=stacked, n_stacked=1, sem=("parallel", "arbitrary"),
        scratch_shapes=[pltpu.VMEM((8, DI), F32), pltpu.VMEM((8, BCW), F32)])
    return call(u_ssd, u_ssd, u_ssd, u_small, dt_t, conv_prev, h0, *params)


def _sortable_key(x):
    b = lax.bitcast_convert_type(x, I32)
    return b ^ ((b >> 31) & 0x7FFFFFFF)


def _head_weights(us_ref):
    return us_ref[:, US_WI:US_WI + IDX_HEADS] * ((IDX_HEADS * IDX_DIM) ** -0.5)


def _indexer_scores(qi_heads, wi, ki_blk):
    Q = qi_heads[0].shape[0]
    acc = None
    for h0 in range(0, IDX_HEADS, IDX_STACK):
        l = _dot_nt(jnp.concatenate(qi_heads[h0:h0 + IDX_STACK], axis=0), ki_blk)
        for n in range(IDX_STACK):
            t = jnp.maximum(l[n * Q:(n + 1) * Q], 0.0) * wi[:, h0 + n:h0 + n + 1]
            acc = t if acc is None else acc + t
    return acc


def _kth_largest(count_ge, shape, k):
    c0 = count_ge(jnp.zeros(shape, I32))
    t0 = jnp.where(c0 >= k, 0, INT_MIN).astype(I32)

    def body(i, t):
        cand = t | (jnp.int32(1) << (30 - i))
        return jnp.where(count_ge(cand) >= k, cand, t)

    return lax.fori_loop(0, 31, body, t0)


def _tie_cutoff(count_ge, count_tie_below, thr, k, n_keys, cut_s):
    cut_s[...] = jnp.full(thr.shape, 2147483647, I32)
    n_ge = count_ge(thr)

    @pl.when(jnp.max(n_ge) > k)
    def _():
        n_gt = jnp.where(thr == 2147483647, 0, count_ge(thr + 1))
        need = k - n_gt

        def body(i, v):
            cand = v | (jnp.int32(1) << (n_keys.bit_length() - 1 - i))
            return jnp.where(count_tie_below(cand) < need, cand, v)

        v = lax.fori_loop(0, n_keys.bit_length(), body, jnp.zeros(thr.shape, I32))
        cut_s[...] = jnp.where(n_ge > k, v + 1, 2147483647)


def _selected(key, idx, thr, cut):
    return (key > INT_MIN) & ((key > thr) | ((key == thr) & (idx < cut)))


def _limits(pos0, n_rows):
    pos = pos0 + lax.broadcasted_iota(I32, (n_rows, 1), 0)
    return (pos // CHUNK + 1) * CHUNK


def _stack_q_heads(q_ref, g):
    parts = [q_ref[:, (g * Q_PER_KV + i) * HEAD_DIM:(g * Q_PER_KV + i + 1) * HEAD_DIM] for i in range(Q_PER_KV)]
    return (jnp.concatenate(parts, axis=0) * (HEAD_DIM ** -0.5 * LOG2E)).astype(BF16)


def _dsa_prompt_kernel(q_ref, qi_ref, us_ref, ki_ref, k_ref, vt_ref, o_ref, key_s, bias_s, cut_s, *,
                       n_sel, kv_heads):
    QB, KB = DSA_QB, DSA_KB
    nkb_max = key_s.shape[0]
    j = pl.program_id(1)
    nkb = ((j + 1) * QB + KB - 1) // KB
    wi = _head_weights(us_ref)
    qi_heads = [qi_ref[:, h * IDX_DIM:(h + 1) * IDX_DIM].astype(BF16) for h in range(IDX_HEADS)]
    pos = j * QB + lax.broadcasted_iota(I32, (1, QB), 1)
    lim = (pos // CHUNK + 1) * CHUNK

    def score_blk(kb, carry):
        off = pl.multiple_of(kb * KB, KB)
        sc = _indexer_scores(qi_heads, wi, ki_ref[pl.ds(off, KB), :])
        s_idx = off + lax.broadcasted_iota(I32, (KB, QB), 0)
        key_s[kb] = jnp.where(s_idx < lim, _sortable_key(sc.T), INT_MIN)
        return carry

    lax.fori_loop(0, nkb, score_blk, 0)

    def count_ge(cand):
        def body(kb, acc):
            m = (key_s[kb] >= cand).astype(I32)
            return acc + jnp.sum(m.reshape(KB // 8, 8, QB), axis=0)
        acc = lax.fori_loop(0, nkb, body, jnp.zeros((8, QB), I32))
        return jnp.sum(acc, axis=0, keepdims=True)

    thr = _kth_largest(count_ge, (1, QB), n_sel)

    def key_index(kb):
        return kb * KB + lax.broadcasted_iota(I32, (KB, QB), 0)

    def count_tie_below(c):
        def body(kb, acc):
            m = ((key_s[kb] == thr) & (key_index(kb) < c)).astype(I32)
            return acc + jnp.sum(m.reshape(KB // 8, 8, QB), axis=0)
        acc = lax.fori_loop(0, nkb, body, jnp.zeros((8, QB), I32))
        return jnp.sum(acc, axis=0, keepdims=True)

    _tie_cutoff(count_ge, count_tie_below, thr, n_sel, nkb_max * KB, cut_s)
    cut = cut_s[...]

    def bias_blk(kb, carry):
        bias_s[kb] = jnp.where(_selected(key_s[kb], key_index(kb), thr, cut), 0.0, NEG)
        return carry

    lax.fori_loop(0, nkb, bias_blk, 0)

    cols = Q_PER_KV * QB
    GPL = DSA_GROUPS_PER_LOOP
    for g0 in range(0, kv_heads, GPL):
        groups = range(g0, g0 + GPL)
        qs = [_stack_q_heads(q_ref, g) for g in groups]

        def body(kb, carry):
            off = pl.multiple_of(kb * KB, KB)
            b = bias_s[kb]
            b4 = jnp.concatenate([b] * Q_PER_KV, axis=1)
            heads = [slice(g * HEAD_DIM, (g + 1) * HEAD_DIM) for g in groups]
            vrows = [slice(g * VROWS, (g + 1) * VROWS) for g in groups]

            def logits(n):
                return _dot_nt(k_ref[pl.ds(off, KB), heads[n]], qs[n]) + b4

            t_next = logits(0)
            new = []
            for n in range(GPL):
                t = t_next
                if n + 1 < GPL:
                    t_next = logits(n + 1)
                m, acc = carry[n]
                m_new = jnp.maximum(m, jnp.max(t, axis=0, keepdims=True))
                alpha = jnp.exp2(m - m_new)
                p = jnp.exp2(t - m_new)
                acc = alpha * acc + _dot(vt_ref[0, kb, vrows[n], :], p.astype(BF16))
                new.append((m_new, acc))
            return tuple(new)

        init = tuple((jnp.full((1, cols), NEG, F32), jnp.zeros((VROWS, cols), F32)) for _ in groups)
        res = lax.fori_loop(0, nkb, body, init)
        for n, g in enumerate(groups):
            acc = res[n][1]
            out = acc[:HEAD_DIM] / acc[HEAD_DIM:HEAD_DIM + 1]
            for i in range(Q_PER_KV):
                o_ref[:, (g * Q_PER_KV + i) * HEAD_DIM:(g * Q_PER_KV + i + 1) * HEAD_DIM] = (
                    out[:, i * QB:(i + 1) * QB].T.astype(o_ref.dtype))


def _dsa_prompt(u_att, u_small, ki_b, k_b, vt_b, cols, *, n_batch, seq, kv_heads):
    M = u_att.shape[0]
    QB, KB = DSA_QB, DSA_KB
    nq = seq // QB
    nkb_max = seq // KB
    DQ = kv_heads * Q_PER_KV * HEAD_DIM
    DK = kv_heads * HEAD_DIM
    DQI = IDX_HEADS * IDX_DIM
    n_sel = min(TOPK_MAX, seq // 4)
    return pl.pallas_call(
        functools.partial(_dsa_prompt_kernel, n_sel=n_sel, kv_heads=kv_heads),
        grid=(n_batch, nq),
        in_specs=[pl.BlockSpec((QB, DQ), lambda b, j: (b * nq + j, cols['q'] // DQ)),
                  pl.BlockSpec((QB, DQI), lambda b, j: (b * nq + j, cols['qi'] // DQI)),
                  pl.BlockSpec((QB, US_W), lambda b, j: (b * nq + j, 0)),
                  pl.BlockSpec((seq, IDX_DIM), lambda b, j: (b, 0)),
                  pl.BlockSpec((seq, DK), lambda b, j: (b, 0)),
                  pl.BlockSpec((1, nkb_max, kv_heads * VROWS, KB), lambda b, j: (b, 0, 0, 0))],
        out_specs=pl.BlockSpec((QB, DQ), lambda b, j: (b * nq + j, 0)),
        out_shape=jax.ShapeDtypeStruct((M, DQ), BF16),
        scratch_shapes=[pltpu.VMEM((nkb_max, KB, QB), I32), pltpu.VMEM((nkb_max, KB, QB), F32),
                        pltpu.VMEM((1, QB), I32)],
        compiler_params=_cparams(("parallel", "arbitrary")),
    )(u_att, u_att, u_small, ki_b, k_b, vt_b)


def _dsa_sample_kernel(q_ref, qi_ref, us_ref, kn_ref, vn_ref, ck_ref, cv_ref, cki_ref, o_ref, cut_s, *,
                       n_sel, kv_heads, past):
    Q = q_ref.shape[0]
    wi = _head_weights(us_ref)
    qi_heads = [qi_ref[:, h * IDX_DIM:(h + 1) * IDX_DIM].astype(BF16) for h in range(IDX_HEADS)]
    lim = _limits(past, Q)
    ki_new = us_ref[:, US_KI:US_KI + IDX_DIM].astype(BF16)
    sc_p = _indexer_scores(qi_heads, wi, cki_ref[0].astype(BF16))
    sc_n = _indexer_scores(qi_heads, wi, ki_new)
    idx_p = lax.broadcasted_iota(I32, (Q, past), 1)
    idx_n = past + lax.broadcasted_iota(I32, (Q, Q), 1)
    key_p = jnp.where(idx_p < lim, _sortable_key(sc_p), INT_MIN)
    key_n = jnp.where(idx_n < lim, _sortable_key(sc_n), INT_MIN)

    def count_ge(cand):
        return (jnp.sum((key_p >= cand).astype(I32), axis=1, keepdims=True)
                + jnp.sum((key_n >= cand).astype(I32), axis=1, keepdims=True))

    thr = _kth_largest(count_ge, (Q, 1), n_sel)

    def count_tie_below(c):
        return (jnp.sum(((key_p == thr) & (idx_p < c)).astype(I32), axis=1, keepdims=True)
                + jnp.sum(((key_n == thr) & (idx_n < c)).astype(I32), axis=1, keepdims=True))

    _tie_cutoff(count_ge, count_tie_below, thr, n_sel, past + Q, cut_s)
    cut = cut_s[...]
    bias_p = jnp.where(_selected(key_p, idx_p, thr, cut), 0.0, NEG)
    bias_n = jnp.where(_selected(key_n, idx_n, thr, cut), 0.0, NEG)
    bias_p = jnp.concatenate([bias_p] * Q_PER_KV, axis=0)
    bias_n = jnp.concatenate([bias_n] * Q_PER_KV, axis=0)

    for g in range(kv_heads):
        hs = slice(g * HEAD_DIM, (g + 1) * HEAD_DIM)
        qs = _stack_q_heads(q_ref, g)
        ck = ck_ref[0, pl.ds(g, past, stride=kv_heads), :].astype(BF16)
        cv = cv_ref[0, pl.ds(g, past, stride=kv_heads), :].astype(BF16)
        t_p = _dot_nt(qs, ck) + bias_p
        t_n = _dot_nt(qs, kn_ref[:, hs].astype(BF16)) + bias_n
        m = jnp.maximum(jnp.max(t_p, axis=-1, keepdims=True), jnp.max(t_n, axis=-1, keepdims=True))
        p_p = jnp.exp2(t_p - m)
        p_n = jnp.exp2(t_n - m)
        l = jnp.sum(p_p, axis=-1, keepdims=True) + jnp.sum(p_n, axis=-1, keepdims=True)
        acc = _dot(p_p.astype(BF16), cv) + _dot(p_n.astype(BF16), vn_ref[:, hs].astype(BF16))
        out = acc / l
        for i in range(Q_PER_KV):
            o_ref[:, (g * Q_PER_KV + i) * HEAD_DIM:(g * Q_PER_KV + i + 1) * HEAD_DIM] = (
                out[i * Q:(i + 1) * Q].astype(o_ref.dtype))


def _dsa_sample(u_att, u_small, cache_k, cache_v, cache_ki, cols, *, n_batch, seq, row0, kv_heads, stacked):
    past = cache_ki.shape[1]
    DQ = kv_heads * Q_PER_KV * HEAD_DIM
    DK = kv_heads * HEAD_DIM
    DQI = IDX_HEADS * IDX_DIM
    n_sel = min(TOPK_MAX, (past + seq) // 4)
    blk0 = row0 // seq
    call = _stacked_call(
        functools.partial(_dsa_sample_kernel, n_sel=n_sel, kv_heads=kv_heads, past=past),
        grid=(n_batch,),
        in_specs=[pl.BlockSpec((seq, DQ), lambda b: (blk0 + b, cols['q'] // DQ)),
                  pl.BlockSpec((seq, DQI), lambda b: (blk0 + b, cols['qi'] // DQI)),
                  pl.BlockSpec((seq, US_W), lambda b: (blk0 + b, 0)),
                  pl.BlockSpec((seq, DK), lambda b: (blk0 + b, cols['k'] // DK)),
                  pl.BlockSpec((seq, DK), lambda b: (blk0 + b, cols['v'] // DK)),
                  pl.BlockSpec((1, past * kv_heads, HEAD_DIM), lambda b: (b, 0, 0)),
                  pl.BlockSpec((1, past * kv_heads, HEAD_DIM), lambda b: (b, 0, 0)),
                  pl.BlockSpec((1, past, IDX_DIM), lambda b: (b, 0, 0))],
        out_specs=[pl.BlockSpec((seq, DQ), lambda b: (blk0 + b, 0))],
        out_shape=[jax.ShapeDtypeStruct(stacked[0].shape, BF16)],
        stacked=stacked, n_stacked=1, sem=("parallel",), scratch_shapes=[pltpu.VMEM((seq, 1), I32)])
    return call(u_att, u_att, u_small, u_att, u_att, cache_k, cache_v, cache_ki)[0]


def _pair_layout(v, T):
    return jnp.repeat(v.reshape(-1, 2), T, axis=1)


def _dt_transposed(dt_raw, T):
    n, H = dt_raw.shape
    return dt_raw.reshape(n // T, T, H // 2, 2).transpose(0, 2, 3, 1).reshape(n // T, H // 2, 2 * T)


def _ssd_params(conv_w, conv_b, dt_bias, a_log, d_skip, norm_g, T):
    H = dt_bias.shape[0]
    P = SSD_HEAD_DIM
    eye = jnp.eye(H, dtype=F32)
    return dict(conv_w=conv_w, conv_b=conv_b.reshape(1, -1),
                dtb=dt_bias.reshape(1, H), alog=a_log.reshape(1, H),
                dtbT=_pair_layout(dt_bias, T), alogT=_pair_layout(a_log, T),
                dskip=jnp.repeat(d_skip, P).reshape(1, H * P), ng=norm_g.reshape(1, -1),
                expP=jnp.repeat(eye, P, axis=1).astype(BF16), expT=jnp.repeat(eye, T, axis=1).astype(BF16))


def kernel(x_prompt, x_sample, mem_prompt, cache_k, cache_v, cache_idx_k, cache_mem_k, cache_mem_v,
           state_ssm, state_conv, ln1_g, ln1_b, ffn1_w1, ffn1_w3, ffn1_w2, w_in, conv_w, conv_b,
           dt_bias, a_log, d_skip, ssd_norm_g, w_ssd_br, w_att_br, w_out, ln2_g, ln2_b,
           w_mq, w_mk, w_mv, w_mo, ln3_g, ln3_b, ffn2_w1, ffn2_w3, ffn2_w2, ln4_g, ln4_b):
    assert x_prompt.ndim == 3 and ln1_g.shape[0] == DEPTH == 1
    l = 0
    BP, LP, D = x_prompt.shape
    BS, LS, _ = x_sample.shape
    MP, MS = BP * LP, BS * LS
    M = MP + MS
    H = dt_bias.shape[1]
    assert H == SSD_HEADS
    DI = H * SSD_HEAD_DIM
    BCW = conv_w.shape[2] - DI
    KVH = cache_k.shape[3]
    DQ = KVH * Q_PER_KV * HEAD_DIM
    DK = KVH * HEAD_DIM
    DQI = IDX_HEADS * IDX_DIM
    MT = mem_prompt.shape[1]
    PAST = cache_k.shape[2]
    tm_big = M // 8
    tm_half = M // 16

    w0t = w_in[l].T
    c_dt = 2 * DI + BCW
    c_q = c_dt + H
    c_wi = c_q + DQ + 2 * DK + DQI
    c_ki = c_wi + IDX_HEADS
    c_g = c_ki + IDX_DIM
    assert c_g + 2 * D == w0t.shape[0]
    cols_ssd = dict(z=0, xs=DI, bc=2 * DI)
    cols_att = dict(q=0, k=DQ, v=DQ + DK, qi=DQ + 2 * DK)
    n_small = H + IDX_HEADS + IDX_DIM
    assert n_small <= US_W
    w_small_t = jnp.concatenate([w0t[c_dt:c_q], w0t[c_wi:c_g], jnp.zeros((US_W - n_small, D), F32)], axis=0)

    x_p2, x_s2 = x_prompt.reshape(MP, D), x_sample.reshape(MS, D)
    x0b = _cast_rows(x_p2, out_rows=M, out_row0=0)
    x0b, = _cast_rows(x_s2, out_rows=M, out_row0=MP, stacked=x0b)

    def ffn_delta(xb, w1, w3, w2, residual=None):
        h = _ffn_up(xb, w1[l], w3[l], tm=M // 4, tn=256)
        return _matmul(h, w2[l], tm=tm_big, tn=256, out_dtype=F32, single_buffer_x=True,
                       residual=residual, scale=0.5)

    d1 = ffn_delta(x0b, ffn1_w1, ffn1_w3, ffn1_w2)
    ln1 = _res_layer_norm(x_p2, d1, ln1_g[l], ln1_b[l], scale=0.5, n_rows=MP, out_rows=M)
    x1, x1b = _res_layer_norm(x_s2, d1, ln1_g[l], ln1_b[l], scale=0.5, n_rows=MS, d_row0=MP, out_row0=MP,
                              stacked=ln1)

    u_ssd = _matmul_nt(x1b, w0t, row0=0, n_rows=c_dt, tm=M // 4, tn=512, out_dtype=F32)
    u_att = _matmul_nt(x1b, w0t, row0=c_q, n_rows=c_wi - c_q, tm=M // 4, tn=512, out_dtype=F32)
    u_gate = _matmul_nt(x1b, w0t, row0=c_g, n_rows=2 * D, tm=M // 4, tn=512, out_dtype=F32)
    u_small = _matmul_nt(x1b, w_small_t, row0=0, n_rows=US_W, tm=tm_big, tn=US_W, out_dtype=F32)

    dt_raw = u_small[:, US_DT:US_DT + H]
    zeros_conv = jnp.zeros((BP,) + state_conv.shape[2:], F32)
    zeros_h = jnp.zeros((BP, DI, SSD_STATE), F32)
    prm_p = _ssd_params(conv_w[l], conv_b[l], dt_bias[l], a_log[l], d_skip[l], ssd_norm_g[l], CHUNK)
    y_ssd, conv_p, h_p = _ssd_mixer(u_ssd, u_small, _dt_transposed(dt_raw[:MP], CHUNK), zeros_conv, zeros_h,
                                    prm_p, cols_ssd, n_batch=BP, seq=LP, row0=0, T=CHUNK)
    TS = min(CHUNK, LS)
    prm_s = _ssd_params(conv_w[l], conv_b[l], dt_bias[l], a_log[l], d_skip[l], ssd_norm_g[l], TS)
    y_ssd, conv_s, h_s = _ssd_mixer(u_ssd, u_small, _dt_transposed(dt_raw[MP:], TS), state_conv[l],
                                    state_ssm[l].reshape(BS, DI, SSD_STATE),
                                    prm_s, cols_ssd, n_batch=BS, seq=LS, row0=MP, T=TS, stacked=(y_ssd,))

    k_all = u_att[:, cols_att['k']:cols_att['k'] + DK]
    v_all = u_att[:, cols_att['v']:cols_att['v'] + DK]
    ki_all = u_small[:, US_KI:US_KI + IDX_DIM]
    nkb = LP // DSA_KB
    vt5 = v_all[:MP].astype(BF16).reshape(BP, nkb, DSA_KB, KVH, HEAD_DIM).transpose(0, 1, 3, 4, 2)
    ones_row = (lax.broadcasted_iota(I32, (BP, nkb, KVH, VPAD, DSA_KB), 3) == 0).astype(BF16)
    vt_b = jnp.concatenate([vt5, ones_row], axis=3).reshape(BP, nkb, KVH * VROWS, DSA_KB)
    y_att = _dsa_prompt(u_att, u_small, ki_all[:MP].astype(BF16), k_all[:MP].astype(BF16),
                        vt_b, cols_att, n_batch=BP, seq=LP, kv_heads=KVH)
    y_att = _dsa_sample(u_att, u_small, cache_k[l].reshape(BS, PAST * KVH, HEAD_DIM),
                        cache_v[l].reshape(BS, PAST * KVH, HEAD_DIM), cache_idx_k[l], cols_att,
                        n_batch=BS, seq=LS, row0=MP, kv_heads=KVH, stacked=(y_att,))

    merged = _gated_merge(y_ssd, y_att, w_ssd_br[l], w_att_br[l], u_gate, 0, D, tm=tm_big, tn=256)
    s2 = _matmul(merged, w_out[l], tm=tm_big, tn=512, out_dtype=F32, residual=x1)
    x2, x2b = _res_layer_norm(s2, None, ln2_g[l], ln2_b[l], n_rows=M)

    memb = mem_prompt.reshape(BP * MT, D).astype(BF16)
    mk_p = _matmul(memb, w_mk[l], tm=BP * MT, tn=512, out_dtype=F32)
    mv_p = _matmul(memb, w_mv[l], tm=BP * MT, tn=512, out_dtype=F32)
    qm = _matmul(x2b, w_mq[l], tm=tm_big, tn=512, out_dtype=BF16)
    o_m = _memory_attention(qm, mk_p, mv_p, n_batch=BP, rows_per_batch=LP, row0=0, tq=min(512, LP))
    o_m = _memory_attention(qm, cache_mem_k[l].reshape(BS * MT, D), cache_mem_v[l].reshape(BS * MT, D),
                            n_batch=BS, rows_per_batch=LS, row0=MP, tq=LS, stacked=(o_m,))
    s3 = _matmul(o_m, w_mo[l], tm=tm_big, tn=512, out_dtype=F32, residual=x2)
    x3, x3b = _res_layer_norm(s3, None, ln3_g[l], ln3_b[l], n_rows=M)

    s4 = ffn_delta(x3b, ffn2_w1, ffn2_w3, ffn2_w2, residual=x3)
    y_p, = _res_layer_norm(s4, None, ln4_g[l], ln4_b[l], n_rows=MP, emit_bf16=False)
    y_s, = _res_layer_norm(s4, None, ln4_g[l], ln4_b[l], n_rows=MS, x_row0=MP, emit_bf16=False)

    mh = D // MEM_HEADS
    return (y_p.reshape(BP, LP, D), y_s.reshape(BS, LS, D),
            h_p.reshape(1, BP, H, SSD_HEAD_DIM, SSD_STATE), conv_p[None],
            k_all[:MP].reshape(1, BP, LP, KVH, HEAD_DIM), v_all[:MP].reshape(1, BP, LP, KVH, HEAD_DIM),
            ki_all[:MP].reshape(1, BP, LP, IDX_DIM),
            mk_p.reshape(1, BP, MT, MEM_HEADS, mh), mv_p.reshape(1, BP, MT, MEM_HEADS, mh),
            h_s.reshape(1, BS, H, SSD_HEAD_DIM, SSD_STATE), conv_s[None],
            k_all[MP:].reshape(1, BS, LS, KVH, HEAD_DIM), v_all[MP:].reshape(1, BS, LS, KVH, HEAD_DIM),
            ki_all[MP:].reshape(1, BS, LS, IDX_DIM))
```

```python
import functools
import math

import jax
import jax.numpy as jnp
from jax import lax
from jax.experimental import pallas as pl
from jax.experimental.pallas import tpu as pltpu

F32 = jnp.float32
BF16 = jnp.bfloat16
I32 = jnp.int32

DEPTH = 1
CHUNK = 64
SSD_HEADS = 64
SSD_HEAD_DIM = 64
SSD_HEADS_PER_GROUP = 8
SSD_STATE = 128
SSD_CONV_W = 4
HEAD_DIM = 128
Q_PER_KV = 4
IDX_HEADS = 16
IDX_DIM = 64
TOPK_MAX = 256
MEM_HEADS = 4
ALPHA = (2.0 * DEPTH) ** 0.25
LN_EPS = 1e-5
RMS_EPS = 1e-5

LANES = 128
VMEM_LIMIT = 56 * 1024 * 1024

NEG = -1e30
INT_MIN = -2147483648
LOG2E = math.log2(math.e)

DSA_QB = 128
DSA_KB = 512
DSA_GROUPS_PER_LOOP = 8
DSA_SAMPLE_STREAMS = 2
QK_AHEAD = 2
IDX_STACK = 4
VPAD = 16
VROWS = HEAD_DIM + VPAD
LN_ROWS = 256

US_DT = 0
US_WI = US_DT + SSD_HEADS
US_KI = US_WI + IDX_HEADS
US_W = 256


def _cparams(sem):
    return pltpu.CompilerParams(dimension_semantics=sem, vmem_limit_bytes=VMEM_LIMIT)


def _dot(a, b):
    return jnp.dot(a, b, preferred_element_type=F32)


def _dot_nt(a, b):
    return lax.dot_general(a, b, (((1,), (1,)), ((), ())), preferred_element_type=F32)


def _dot_tn(a, b, precision=None):
    return lax.dot_general(a, b, (((0,), (0,)), ((), ())), preferred_element_type=F32,
                           precision=precision)


def _dot_hi(a, b):
    return jnp.dot(a, b, preferred_element_type=F32, precision=lax.Precision.HIGHEST)


def _sigmoid(x):
    return 1.0 / (1.0 + jnp.exp(-x))


def _silu(x):
    return x * _sigmoid(x)


def _softplus(x):
    return jnp.maximum(x, 0.0) + jnp.log1p(jnp.exp(-jnp.abs(x)))


def _stacked_call(kernel, *, grid, in_specs, out_specs, out_shape, stacked, n_stacked, sem, scratch_shapes=()):
    if stacked is None:
        return pl.pallas_call(kernel, grid=grid, in_specs=in_specs, out_specs=out_specs, out_shape=out_shape,
                              scratch_shapes=scratch_shapes, compiler_params=_cparams(sem))
    n_in = len(in_specs)

    def body(*refs):
        kernel(*refs[:n_in], *refs[n_in + n_stacked:])

    call = pl.pallas_call(
        body, grid=grid,
        in_specs=list(in_specs) + [pl.BlockSpec(memory_space=pl.ANY)] * n_stacked,
        out_specs=out_specs, out_shape=out_shape, scratch_shapes=scratch_shapes,
        input_output_aliases={n_in + i: i for i in range(n_stacked)},
        compiler_params=_cparams(sem))
    return lambda *args: call(*args, *stacked)


def _mm_kernel(x_ref, w_ref, o_ref):
    o_ref[...] = _dot(x_ref[...], w_ref[...].astype(BF16)).astype(o_ref.dtype)


def _mm_res_kernel(x_ref, w_ref, r_ref, o_ref, *, scale):
    o_ref[...] = ALPHA * r_ref[...] + scale * _dot(x_ref[...], w_ref[...].astype(BF16))


def _matmul(x, w, *, tm, tn, out_dtype, single_buffer_x=False, residual=None, scale=1.0):
    M, K = x.shape
    N = w.shape[1]
    assert N % tn == 0 and M % tm == 0
    xmode = dict(pipeline_mode=pl.Buffered(1)) if single_buffer_x else {}
    tile = pl.BlockSpec((tm, tn), lambda i, j: (i, j))
    in_specs = [pl.BlockSpec((tm, K), lambda i, j: (i, 0), **xmode),
                pl.BlockSpec((K, tn), lambda i, j: (0, j))]
    if residual is None:
        body, args = _mm_kernel, (x, w)
    else:
        assert out_dtype == F32
        body, args = functools.partial(_mm_res_kernel, scale=scale), (x, w, residual)
        in_specs.append(tile)
    return pl.pallas_call(
        body,
        grid=(M // tm, N // tn),
        in_specs=in_specs,
        out_specs=tile,
        out_shape=jax.ShapeDtypeStruct((M, N), out_dtype),
        compiler_params=_cparams(("parallel", "arbitrary")),
    )(*args)


def _mm_nt_kernel(x_ref, wt_ref, o_ref):
    o_ref[...] = _dot_nt(x_ref[...], wt_ref[...].astype(BF16)).astype(o_ref.dtype)


def _matmul_nt(x, wt, *, row0, n_rows, tm, tn, out_dtype):
    M, K = x.shape
    assert n_rows % tn == 0 and M % tm == 0 and row0 % 8 == 0
    return pl.pallas_call(
        _mm_nt_kernel,
        grid=(M // tm, n_rows // tn),
        in_specs=[pl.BlockSpec((tm, K), lambda i, j: (i, 0), pipeline_mode=pl.Buffered(1)),
                  pl.BlockSpec((pl.Element(tn), pl.Element(K)), lambda i, j: (pl.multiple_of(row0 + j * tn, 8), 0))],
        out_specs=pl.BlockSpec((tm, tn), lambda i, j: (i, j)),
        out_shape=jax.ShapeDtypeStruct((M, n_rows), out_dtype),
        compiler_params=_cparams(("parallel", "arbitrary")),
    )(x, wt)


def _cast_kernel(x_ref, o_ref):
    o_ref[...] = x_ref[...].astype(o_ref.dtype)


def _cast_rows(x, *, out_rows, out_row0, stacked=None):
    n, D = x.shape
    tr = max(r for r in range(16, LN_ROWS + 1, 16) if n % r == 0 and out_row0 % r == 0)
    ob = out_row0 // tr
    call = _stacked_call(
        _cast_kernel, grid=(n // tr,),
        in_specs=[pl.BlockSpec((tr, D), lambda i: (i, 0))],
        out_specs=[pl.BlockSpec((tr, D), lambda i: (ob + i, 0))],
        out_shape=[jax.ShapeDtypeStruct((out_rows, D), BF16)],
        stacked=stacked, n_stacked=1, sem=("parallel",))
    return call(x)


def _ffn_up_kernel(x_ref, w1_ref, w3_ref, o_ref):
    x = x_ref[...]
    a = _dot(x, w1_ref[...].astype(BF16))
    b = _dot(x, w3_ref[...].astype(BF16))
    o_ref[...] = (_silu(a) * b).astype(o_ref.dtype)


def _ffn_up(x, w1, w3, *, tm, tn):
    M, K = x.shape
    N = w1.shape[1]
    return pl.pallas_call(
        _ffn_up_kernel,
        grid=(M // tm, N // tn),
        in_specs=[pl.BlockSpec((tm, K), lambda i, j: (i, 0), pipeline_mode=pl.Buffered(1)),
                  pl.BlockSpec((K, tn), lambda i, j: (0, j)),
                  pl.BlockSpec((K, tn), lambda i, j: (0, j))],
        out_specs=pl.BlockSpec((tm, tn), lambda i, j: (i, j)),
        out_shape=jax.ShapeDtypeStruct((M, N), BF16),
        compiler_params=_cparams(("parallel", "arbitrary")),
    )(x, w1, w3)


def _gate_kernel(ys_ref, ya_ref, ws_ref, wa_ref, gs_ref, ga_ref, o_ref):
    s = _dot(ys_ref[...], ws_ref[...].astype(BF16))
    a = _dot(ya_ref[...], wa_ref[...].astype(BF16))
    o_ref[...] = (_sigmoid(gs_ref[...]) * s + _sigmoid(ga_ref[...]) * a).astype(o_ref.dtype)


def _gated_merge(y_ssd, y_att, w_ssd, w_att, u_gate, gs_col, ga_col, *, tm, tn):
    M, K = y_ssd.shape
    N = w_ssd.shape[1]
    gs_blk, ga_blk = gs_col // tn, ga_col // tn
    once = dict(pipeline_mode=pl.Buffered(1))
    return pl.pallas_call(
        _gate_kernel,
        grid=(M // tm, N // tn),
        in_specs=[pl.BlockSpec((tm, K), lambda i, j: (i, 0), **once),
                  pl.BlockSpec((tm, K), lambda i, j: (i, 0), **once),
                  pl.BlockSpec((K, tn), lambda i, j: (0, j)),
                  pl.BlockSpec((K, tn), lambda i, j: (0, j)),
                  pl.BlockSpec((tm, tn), lambda i, j: (i, gs_blk + j)),
                  pl.BlockSpec((tm, tn), lambda i, j: (i, ga_blk + j))],
        out_specs=pl.BlockSpec((tm, tn), lambda i, j: (i, j)),
        out_shape=jax.ShapeDtypeStruct((M, N), BF16),
        compiler_params=_cparams(("parallel", "arbitrary")),
    )(y_ssd, y_att, w_ssd, w_att, u_gate, u_gate)


def _layer_norm_rows(y, g, b):
    mu = jnp.mean(y, axis=-1, keepdims=True)
    yc = y - mu
    var = jnp.mean(yc * yc, axis=-1, keepdims=True)
    return yc * lax.rsqrt(var + LN_EPS) * g + b


def _ln_kernel(x_ref, d_ref, g_ref, b_ref, *o_refs, scale):
    o = _layer_norm_rows(ALPHA * x_ref[...] + scale * d_ref[...], g_ref[...], b_ref[...])
    for o_ref in o_refs:
        o_ref[...] = o.astype(o_ref.dtype)


def _ln_presummed_kernel(y_ref, g_ref, b_ref, *o_refs):
    o = _layer_norm_rows(y_ref[...], g_ref[...], b_ref[...])
    for o_ref in o_refs:
        o_ref[...] = o.astype(o_ref.dtype)


def _res_layer_norm(x, delta, g, b, *, scale=None, n_rows, x_row0=0, d_row0=0, out_rows=None, out_row0=0,
                    stacked=None, emit_bf16=True):
    D = x.shape[1]
    out_rows = n_rows if out_rows is None else out_rows
    if stacked is not None:
        out_rows = stacked[0].shape[0]
    tr = max(r for r in range(8, LN_ROWS + 1, 8)
             if all(v % r == 0 for v in (n_rows, x_row0, d_row0, out_row0)))
    xb, db, ob = x_row0 // tr, d_row0 // tr, out_row0 // tr
    vec = pl.BlockSpec((1, D), lambda i: (0, 0))
    dts = (F32, BF16) if emit_bf16 else (F32,)
    if delta is None:
        body, args = _ln_presummed_kernel, (x,)
        in_specs = [pl.BlockSpec((tr, D), lambda i: (xb + i, 0)), vec, vec]
    else:
        body, args = functools.partial(_ln_kernel, scale=scale), (x, delta)
        in_specs = [pl.BlockSpec((tr, D), lambda i: (xb + i, 0)), pl.BlockSpec((tr, D), lambda i: (db + i, 0)), vec, vec]
    call = _stacked_call(
        body, grid=(n_rows // tr,), in_specs=in_specs,
        out_specs=[pl.BlockSpec((tr, D), lambda i: (ob + i, 0)) for _ in dts],
        out_shape=[jax.ShapeDtypeStruct((out_rows, D), dt) for dt in dts],
        stacked=stacked, n_stacked=len(dts), sem=("parallel",))
    return call(*args, g.reshape(1, D), b.reshape(1, D))


def _memattn_kernel(q_ref, k_ref, v_ref, o_ref):
    dh = q_ref.shape[1] // MEM_HEADS
    scale = dh ** -0.5
    for h in range(MEM_HEADS):
        sl = slice(h * dh, (h + 1) * dh)
        q = q_ref[:, sl]
        k = k_ref[:, sl].astype(BF16)
        v = v_ref[:, sl].astype(BF16)
        s = _dot_nt(q, k) * scale
        m = jnp.max(s, axis=-1, keepdims=True)
        p = jnp.exp(s - m)
        p = p / jnp.sum(p, axis=-1, keepdims=True)
        o_ref[:, sl] = _dot(p.astype(BF16), v).astype(o_ref.dtype)


def _memory_attention(q, mem_k, mem_v, *, n_batch, rows_per_batch, row0, tq, stacked=None):
    M, D = q.shape
    mt = mem_k.shape[0] // n_batch
    nq = rows_per_batch // tq
    blk0 = row0 // tq
    qspec = pl.BlockSpec((tq, D), lambda b, j: (blk0 + b * nq + j, 0))
    mspec = pl.BlockSpec((mt, D), lambda b, j: (b, 0))
    call = _stacked_call(
        _memattn_kernel, grid=(n_batch, nq), in_specs=[qspec, mspec, mspec], out_specs=[qspec],
        out_shape=[jax.ShapeDtypeStruct((M, D), BF16)], stacked=stacked, n_stacked=1,
        sem=("parallel", "arbitrary"))
    return call(q, mem_k, mem_v)[0]


def _ssd_kernel(z_ref, xs_ref, bc_ref, us_ref, dtT_ref, cprev_ref, h0_ref,
                convw_ref, convb_ref, dtb_ref, alog_ref, dtbT_ref, alogT_ref, dskip_ref, ng_ref,
                expP_ref, expT_ref,
                y_ref, cnew_ref, h_ref, extx_s, extbc_s, *, T, n_heads):
    P, R, N = SSD_HEAD_DIM, SSD_HEADS_PER_GROUP, SSD_STATE
    G = n_heads // R
    DI = n_heads * P
    GN = G * N
    c = pl.program_id(1)
    W1 = SSD_CONV_W - 1
    base = 8 - W1

    @pl.when(c == 0)
    def _():
        extx_s[0:base, :] = jnp.zeros((base, DI), F32)
        extbc_s[0:base, :] = jnp.zeros((base, 2 * GN), F32)
        extx_s[base:8, :] = cprev_ref[0, :, :DI]
        extbc_s[base:8, :] = cprev_ref[0, :, DI:]
        h_ref[0] = h0_ref[0]

    def conv(hist_s, x, lo, hi):
        xx = jnp.concatenate([hist_s[...], x], axis=0)
        out = convb_ref[:, lo:hi]
        for j in range(SSD_CONV_W):
            tap = xx if j == W1 else pltpu.roll(xx, W1 - j, axis=0)
            out = out + convw_ref[j:j + 1, lo:hi] * tap[8:8 + T]
        return _silu(out)

    x_in = xs_ref[...]
    bc_in = bc_ref[...]
    xs = conv(extx_s, x_in, 0, DI)
    bc = conv(extbc_s, bc_in, DI, DI + 2 * GN)
    newx = x_in[T - W1:T]
    newbc = bc_in[T - W1:T]
    cnew_ref[0, :, :DI] = newx
    cnew_ref[0, :, DI:] = newbc
    extx_s[base:8, :] = newx
    extbc_s[base:8, :] = newbc

    def expand(vals, e_ref):
        pieces = []
        for v in vals:
            for _ in range(3):
                p = v.astype(BF16)
                pieces.append(p)
                v = v - p.astype(F32)
        y = _dot(jnp.concatenate(pieces, axis=0), e_ref[...])
        return [y[(3 * n) * T:(3 * n + 1) * T] + y[(3 * n + 1) * T:(3 * n + 2) * T] + y[(3 * n + 2) * T:(3 * n + 3) * T]
                for n in range(len(vals))]

    dt = _softplus(us_ref[:, US_DT:US_DT + n_heads] + dtb_ref[...])
    d_a = dt * (-jnp.exp(alog_ref[...]))
    row = lax.broadcasted_iota(I32, (T, T), 0)
    col = lax.broadcasted_iota(I32, (T, T), 1)
    tri = (row >= col).astype(F32)
    a_cum = _dot_hi(tri, d_a)
    a_exp, dt_exp = expand([a_cum, dt], expP_ref)
    a_expt = a_exp if T == P else expand([a_cum], expT_ref)[0]
    d_a_t = _softplus(dtT_ref[0] + dtbT_ref[...]) * (-jnp.exp(alogT_ref[...]))
    r2 = lax.broadcasted_iota(I32, (2 * T, 2 * T), 0)
    c2 = lax.broadcasted_iota(I32, (2 * T, 2 * T), 1)
    tri2 = ((r2 // T == c2 // T) & (r2 <= c2)).astype(F32)
    a_cum_t = _dot_hi(d_a_t, tri2)

    l_idx = lax.broadcasted_iota(I32, (T, 2 * T), 0)
    j_idx = lax.broadcasted_iota(I32, (T, 2 * T), 1)
    causal2 = l_idx >= (j_idx % T)
    rr = lax.broadcasted_iota(I32, (2 * T, 2 * P), 0)
    cc = lax.broadcasted_iota(I32, (2 * T, 2 * P), 1)
    blockdiag = (rr // T) == (cc // P)
    chunk_decay = jnp.exp(a_cum_t)

    xdt = xs * dt_exp
    for g in range(G):
        gs = slice(g * R * P, (g + 1) * R * P)
        b_g = bc[:, g * N:(g + 1) * N].astype(BF16)
        c_g = bc[:, GN + g * N:GN + (g + 1) * N].astype(BF16)
        cb2 = _dot_nt(c_g, jnp.concatenate([b_g, b_g], axis=0))
        h_g = h_ref[0, gs, :]
        y_off = _dot_nt(c_g, h_g.astype(BF16)) * jnp.exp(a_exp[:, gs])
        pairs = []
        for pr in range(R // 2):
            i = g * (R // 2) + pr
            seg = a_expt[:, i * 2 * T:(i + 1) * 2 * T] - a_cum_t[i:i + 1, :]
            decay = jnp.exp(jnp.where(causal2, seg, -jnp.inf))
            m_pair = (cb2 * decay).astype(BF16)
            x_pair = xdt[:, i * 2 * P:(i + 1) * 2 * P]
            rhs = jnp.where(blockdiag, jnp.concatenate([x_pair, x_pair], axis=0), 0.0).astype(BF16)
            pairs.append(_dot(m_pair, rhs))
        y_g = jnp.concatenate(pairs, axis=1) + y_off + dskip_ref[:, gs] * xs[:, gs]
        a_g = a_exp[:, gs]
        dte = jnp.exp(a_g[T - 1:T, :] - a_g)
        st = _dot_tn((xdt[:, gs] * dte).astype(BF16), b_g)
        dec = jnp.concatenate(
            [jnp.broadcast_to(chunk_decay[(g * R + r) // 2:(g * R + r) // 2 + 1,
                                          ((g * R + r) % 2) * T + T - 1:((g * R + r) % 2) * T + T], (P, N))
             for r in range(R)], axis=0)
        h_ref[0, gs, :] = dec * h_g + st
        yg = y_g * _silu(z_ref[:, gs])
        ms = jnp.mean(yg * yg, axis=-1, keepdims=True)
        y_ref[:, gs] = (yg * lax.rsqrt(ms + RMS_EPS) * ng_ref[:, gs]).astype(y_ref.dtype)


def _ssd_mixer(u_ssd, u_small, dt_t, conv_prev, h0, prm, cols, *, n_batch, seq, row0, T, stacked=None):
    M = u_ssd.shape[0]
    n_heads = prm['dtb'].shape[1]
    DI = n_heads * SSD_HEAD_DIM
    BCW = conv_prev.shape[2] - DI
    nc = seq // T
    blk0 = row0 // T

    def rows(width, colblk):
        return pl.BlockSpec((T, width), lambda b, c: (blk0 + b * nc + c, colblk))

    def full(a):
        return pl.BlockSpec(a.shape, lambda b, c: (0,) * a.ndim)

    params = [prm['conv_w'], prm['conv_b'], prm['dtb'], prm['alog'], prm['dtbT'], prm['alogT'],
              prm['dskip'], prm['ng'], prm['expP'], prm['expT']]
    in_specs = [rows(DI, cols['z'] // DI), rows(DI, cols['xs'] // DI), rows(BCW, cols['bc'] // BCW),
                rows(US_W, 0),
                pl.BlockSpec((1,) + dt_t.shape[1:], lambda b, c: (b * nc + c, 0, 0)),
                pl.BlockSpec((1,) + conv_prev.shape[1:], lambda b, c: (b, 0, 0)),
                pl.BlockSpec((1,) + h0.shape[1:], lambda b, c: (b, 0, 0))] + [full(a) for a in params]
    out_specs = [rows(DI, 0),
                 pl.BlockSpec((1,) + conv_prev.shape[1:], lambda b, c: (b, 0, 0)),
                 pl.BlockSpec((1,) + h0.shape[1:], lambda b, c: (b, 0, 0))]
    call = _stacked_call(
        functools.partial(_ssd_kernel, T=T, n_heads=n_heads),
        grid=(n_batch, nc), in_specs=in_specs, out_specs=out_specs,
        out_shape=[jax.ShapeDtypeStruct((M, DI), BF16),
                   jax.ShapeDtypeStruct(conv_prev.shape, F32),
                   jax.ShapeDtypeStruct(h0.shape, F32)],
        stacked=stacked, n_stacked=1, sem=("parallel", "arbitrary"),
        scratch_shapes=[pltpu.VMEM((8, DI), F32), pltpu.VMEM((8, BCW), F32)])
    return call(u_ssd, u_ssd, u_ssd, u_small, dt_t, conv_prev, h0, *params)


def _sortable_key(x):
    b = lax.bitcast_convert_type(x, I32)
    return b ^ ((b >> 31) & 0x7FFFFFFF)


def _head_weights(us_ref):
    return us_ref[:, US_WI:US_WI + IDX_HEADS] * ((IDX_HEADS * IDX_DIM) ** -0.5)


def _indexer_scores(qi_heads, wi, ki_blk):
    Q = qi_heads[0].shape[0]
    acc = None
    for h0 in range(0, IDX_HEADS, IDX_STACK):
        l = _dot_nt(jnp.concatenate(qi_heads[h0:h0 + IDX_STACK], axis=0), ki_blk)
        for n in range(IDX_STACK):
            t = jnp.maximum(l[n * Q:(n + 1) * Q], 0.0) * wi[:, h0 + n:h0 + n + 1]
            acc = t if acc is None else acc + t
    return acc


def _kth_largest(count_ge, shape, k):
    c0 = count_ge(jnp.zeros(shape, I32))
    t0 = jnp.where(c0 >= k, 0, INT_MIN).astype(I32)

    def body(i, t):
        cand = t | (jnp.int32(1) << (30 - i))
        return jnp.where(count_ge(cand) >= k, cand, t)

    return lax.fori_loop(0, 31, body, t0)


def _tie_cutoff(count_ge, count_tie_below, thr, k, n_keys, cut_s):
    cut_s[...] = jnp.full(thr.shape, 2147483647, I32)
    n_ge = count_ge(thr)

    @pl.when(jnp.max(n_ge) > k)
    def _():
        n_gt = jnp.where(thr == 2147483647, 0, count_ge(thr + 1))
        need = k - n_gt

        def body(i, v):
            cand = v | (jnp.int32(1) << (n_keys.bit_length() - 1 - i))
            return jnp.where(count_tie_below(cand) < need, cand, v)

        v = lax.fori_loop(0, n_keys.bit_length(), body, jnp.zeros(thr.shape, I32))
        cut_s[...] = jnp.where(n_ge > k, v + 1, 2147483647)


def _selected(key, idx, thr, cut):
    return (key > INT_MIN) & ((key > thr) | ((key == thr) & (idx < cut)))


def _limits(pos0, n_rows):
    pos = pos0 + lax.broadcasted_iota(I32, (n_rows, 1), 0)
    return (pos // CHUNK + 1) * CHUNK


def _stack_q_heads(q_ref, g, rows=slice(None)):
    parts = [q_ref[rows, (g * Q_PER_KV + i) * HEAD_DIM:(g * Q_PER_KV + i + 1) * HEAD_DIM] for i in range(Q_PER_KV)]
    return (jnp.concatenate(parts, axis=0) * (HEAD_DIM ** -0.5 * LOG2E)).astype(BF16)


def _dsa_prompt_kernel(q_ref, qi_ref, us_ref, ki_ref, k_ref, vt_ref, o_ref, key_s, bias_s, cut_s, *,
                       n_sel, kv_heads):
    QB, KB = DSA_QB, DSA_KB
    nkb_max = key_s.shape[0]
    j = pl.program_id(1)
    nkb = ((j + 1) * QB + KB - 1) // KB
    wi = _head_weights(us_ref)
    qi_heads = [qi_ref[:, h * IDX_DIM:(h + 1) * IDX_DIM].astype(BF16) for h in range(IDX_HEADS)]
    pos = j * QB + lax.broadcasted_iota(I32, (1, QB), 1)
    lim = (pos // CHUNK + 1) * CHUNK

    def score_blk(kb, carry):
        off = pl.multiple_of(kb * KB, KB)
        sc = _indexer_scores(qi_heads, wi, ki_ref[pl.ds(off, KB), :])
        s_idx = off + lax.broadcasted_iota(I32, (KB, QB), 0)
        key_s[kb] = jnp.where(s_idx < lim, _sortable_key(sc.T), INT_MIN)
        return carry

    lax.fori_loop(0, nkb, score_blk, 0)

    def count_ge(cand):
        def body(kb, acc):
            m = (key_s[kb] >= cand).astype(I32)
            return acc + jnp.sum(m.reshape(KB // 8, 8, QB), axis=0)
        acc = lax.fori_loop(0, nkb, body, jnp.zeros((8, QB), I32))
        return jnp.sum(acc, axis=0, keepdims=True)

    thr = _kth_largest(count_ge, (1, QB), n_sel)

    def key_index(kb):
        return kb * KB + lax.broadcasted_iota(I32, (KB, QB), 0)

    def count_tie_below(c):
        def body(kb, acc):
            m = ((key_s[kb] == thr) & (key_index(kb) < c)).astype(I32)
            return acc + jnp.sum(m.reshape(KB // 8, 8, QB), axis=0)
        acc = lax.fori_loop(0, nkb, body, jnp.zeros((8, QB), I32))
        return jnp.sum(acc, axis=0, keepdims=True)

    _tie_cutoff(count_ge, count_tie_below, thr, n_sel, nkb_max * KB, cut_s)
    cut = cut_s[...]

    def bias_blk(kb, carry):
        bias_s[kb] = jnp.where(_selected(key_s[kb], key_index(kb), thr, cut), 0.0, NEG)
        return carry

    lax.fori_loop(0, nkb, bias_blk, 0)

    cols = Q_PER_KV * QB
    GPL = DSA_GROUPS_PER_LOOP
    for g0 in range(0, kv_heads, GPL):
        groups = range(g0, g0 + GPL)
        qs = [_stack_q_heads(q_ref, g) for g in groups]

        def body(kb, carry):
            off = pl.multiple_of(kb * KB, KB)
            b = bias_s[kb]
            b4 = jnp.concatenate([b] * Q_PER_KV, axis=1)
            heads = [slice(g * HEAD_DIM, (g + 1) * HEAD_DIM) for g in groups]
            vrows = [slice(g * VROWS, (g + 1) * VROWS) for g in groups]

            def logits(n):
                return _dot_nt(k_ref[pl.ds(off, KB), heads[n]], qs[n]) + b4

            pending = [logits(n) for n in range(min(QK_AHEAD, GPL))]
            new = []
            for n in range(GPL):
                t = pending.pop(0)
                if n + QK_AHEAD < GPL:
                    pending.append(logits(n + QK_AHEAD))
                m, acc = carry[n]
                m_new = jnp.maximum(m, jnp.max(t, axis=0, keepdims=True))
                alpha = jnp.exp2(m - m_new)
                p = jnp.exp2(t - m_new)
                acc = alpha * acc + _dot(vt_ref[0, kb, vrows[n], :], p.astype(BF16))
                new.append((m_new, acc))
            return tuple(new)

        init = tuple((jnp.full((1, cols), NEG, F32), jnp.zeros((VROWS, cols), F32)) for _ in groups)
        res = lax.fori_loop(0, nkb, body, init)
        for n, g in enumerate(groups):
            acc = res[n][1]
            out = acc[:HEAD_DIM] / acc[HEAD_DIM:HEAD_DIM + 1]
            for i in range(Q_PER_KV):
                o_ref[:, (g * Q_PER_KV + i) * HEAD_DIM:(g * Q_PER_KV + i + 1) * HEAD_DIM] = (
                    out[:, i * QB:(i + 1) * QB].T.astype(o_ref.dtype))


def _dsa_prompt(u_att, u_small, ki_b, k_b, vt_b, cols, *, n_batch, seq, kv_heads):
    M = u_att.shape[0]
    QB, KB = DSA_QB, DSA_KB
    nq = seq // QB
    nkb_max = seq // KB
    DQ = kv_heads * Q_PER_KV * HEAD_DIM
    DK = kv_heads * HEAD_DIM
    DQI = IDX_HEADS * IDX_DIM
    n_sel = min(TOPK_MAX, seq // 4)
    return pl.pallas_call(
        functools.partial(_dsa_prompt_kernel, n_sel=n_sel, kv_heads=kv_heads),
        grid=(n_batch, nq),
        in_specs=[pl.BlockSpec((QB, DQ), lambda b, j: (b * nq + j, cols['q'] // DQ)),
                  pl.BlockSpec((QB, DQI), lambda b, j: (b * nq + j, cols['qi'] // DQI)),
                  pl.BlockSpec((QB, US_W), lambda b, j: (b * nq + j, 0)),
                  pl.BlockSpec((seq, IDX_DIM), lambda b, j: (b, 0), pipeline_mode=pl.Buffered(1)),
                  pl.BlockSpec((seq, DK), lambda b, j: (b, 0), pipeline_mode=pl.Buffered(1)),
                  pl.BlockSpec((1, nkb_max, kv_heads * VROWS, KB), lambda b, j: (b, 0, 0, 0),
                               pipeline_mode=pl.Buffered(1))],
        out_specs=pl.BlockSpec((QB, DQ), lambda b, j: (b * nq + j, 0)),
        out_shape=jax.ShapeDtypeStruct((M, DQ), BF16),
        scratch_shapes=[pltpu.VMEM((nkb_max, KB, QB), I32), pltpu.VMEM((nkb_max, KB, QB), F32),
                        pltpu.VMEM((1, QB), I32)],
        compiler_params=_cparams(("parallel", "arbitrary")),
    )(u_att, u_att, u_small, ki_b, k_b, vt_b)


def _dsa_sample_kernel(q_ref, qi_ref, us_ref, kn_ref, vn_ref, ck_ref, cv_ref, cki_ref, o_ref, cut_s, *,
                       n_sel, kv_heads, past, n_streams):
    R = q_ref.shape[0]
    Q = R // n_streams
    wi = _head_weights(us_ref)
    ki_new = us_ref[:, US_KI:US_KI + IDX_DIM].astype(BF16)
    sc_p, sc_n = [], []
    for s in range(n_streams):
        rs = slice(s * Q, (s + 1) * Q)
        qi_heads = [qi_ref[rs, h * IDX_DIM:(h + 1) * IDX_DIM].astype(BF16) for h in range(IDX_HEADS)]
        sc_p.append(_indexer_scores(qi_heads, wi[rs], cki_ref[s].astype(BF16)))
        sc_n.append(_indexer_scores(qi_heads, wi[rs], ki_new[rs]))
    sc_p = jnp.concatenate(sc_p, axis=0)
    sc_n = jnp.concatenate(sc_n, axis=0)
    lim = jnp.concatenate([_limits(past, Q)] * n_streams, axis=0)
    idx_p = lax.broadcasted_iota(I32, (R, past), 1)
    idx_n = past + lax.broadcasted_iota(I32, (R, Q), 1)
    key_p = jnp.where(idx_p < lim, _sortable_key(sc_p), INT_MIN)
    key_n = jnp.where(idx_n < lim, _sortable_key(sc_n), INT_MIN)

    def count_ge(cand):
        return (jnp.sum((key_p >= cand).astype(I32), axis=1, keepdims=True)
                + jnp.sum((key_n >= cand).astype(I32), axis=1, keepdims=True))

    thr = _kth_largest(count_ge, (R, 1), n_sel)

    def count_tie_below(c):
        return (jnp.sum(((key_p == thr) & (idx_p < c)).astype(I32), axis=1, keepdims=True)
                + jnp.sum(((key_n == thr) & (idx_n < c)).astype(I32), axis=1, keepdims=True))

    _tie_cutoff(count_ge, count_tie_below, thr, n_sel, past + Q, cut_s)
    cut = cut_s[...]
    bias_p = jnp.where(_selected(key_p, idx_p, thr, cut), 0.0, NEG)
    bias_n = jnp.where(_selected(key_n, idx_n, thr, cut), 0.0, NEG)

    for s in range(n_streams):
        rs = slice(s * Q, (s + 1) * Q)
        b_p = jnp.concatenate([bias_p[rs]] * Q_PER_KV, axis=0)
        b_n = jnp.concatenate([bias_n[rs]] * Q_PER_KV, axis=0)
        for g in range(kv_heads):
            hs = slice(g * HEAD_DIM, (g + 1) * HEAD_DIM)
            qs = _stack_q_heads(q_ref, g, rs)
            ck = ck_ref[s, pl.ds(g, past, stride=kv_heads), :].astype(BF16)
            cv = cv_ref[s, pl.ds(g, past, stride=kv_heads), :].astype(BF16)
            t_p = _dot_nt(qs, ck) + b_p
            t_n = _dot_nt(qs, kn_ref[rs, hs].astype(BF16)) + b_n
            m = jnp.maximum(jnp.max(t_p, axis=-1, keepdims=True), jnp.max(t_n, axis=-1, keepdims=True))
            p_p = jnp.exp2(t_p - m)
            p_n = jnp.exp2(t_n - m)
            l = jnp.sum(p_p, axis=-1, keepdims=True) + jnp.sum(p_n, axis=-1, keepdims=True)
            acc = _dot(p_p.astype(BF16), cv) + _dot(p_n.astype(BF16), vn_ref[rs, hs].astype(BF16))
            out = acc / l
            for i in range(Q_PER_KV):
                o_ref[rs, (g * Q_PER_KV + i) * HEAD_DIM:(g * Q_PER_KV + i + 1) * HEAD_DIM] = (
                    out[i * Q:(i + 1) * Q].astype(o_ref.dtype))


def _dsa_sample(u_att, u_small, cache_k, cache_v, cache_ki, cols, *, n_batch, seq, row0, kv_heads, stacked):
    past = cache_ki.shape[1]
    DQ = kv_heads * Q_PER_KV * HEAD_DIM
    DK = kv_heads * HEAD_DIM
    DQI = IDX_HEADS * IDX_DIM
    n_sel = min(TOPK_MAX, (past + seq) // 4)
    ns = DSA_SAMPLE_STREAMS
    rows = ns * seq
    assert n_batch % ns == 0 and row0 % rows == 0
    blk0 = row0 // rows
    call = _stacked_call(
        functools.partial(_dsa_sample_kernel, n_sel=n_sel, kv_heads=kv_heads, past=past, n_streams=ns),
        grid=(n_batch // ns,),
        in_specs=[pl.BlockSpec((rows, DQ), lambda b: (blk0 + b, cols['q'] // DQ)),
                  pl.BlockSpec((rows, DQI), lambda b: (blk0 + b, cols['qi'] // DQI)),
                  pl.BlockSpec((rows, US_W), lambda b: (blk0 + b, 0)),
                  pl.BlockSpec((rows, DK), lambda b: (blk0 + b, cols['k'] // DK)),
                  pl.BlockSpec((rows, DK), lambda b: (blk0 + b, cols['v'] // DK)),
                  pl.BlockSpec((ns, past * kv_heads, HEAD_DIM), lambda b: (b, 0, 0)),
                  pl.BlockSpec((ns, past * kv_heads, HEAD_DIM), lambda b: (b, 0, 0)),
                  pl.BlockSpec((ns, past, IDX_DIM), lambda b: (b, 0, 0))],
        out_specs=[pl.BlockSpec((rows, DQ), lambda b: (blk0 + b, 0))],
        out_shape=[jax.ShapeDtypeStruct(stacked[0].shape, BF16)],
        stacked=stacked, n_stacked=1, sem=("parallel",), scratch_shapes=[pltpu.VMEM((rows, 1), I32)])
    return call(u_att, u_att, u_small, u_att, u_att, cache_k, cache_v, cache_ki)[0]


def _pair_layout(v, T):
    return jnp.repeat(v.reshape(-1, 2), T, axis=1)


def _dt_transposed(dt_raw, T):
    n, H = dt_raw.shape
    return dt_raw.reshape(n // T, T, H // 2, 2).transpose(0, 2, 3, 1).reshape(n // T, H // 2, 2 * T)


def _ssd_params(conv_w, conv_b, dt_bias, a_log, d_skip, norm_g, T):
    H = dt_bias.shape[0]
    P = SSD_HEAD_DIM
    eye = jnp.eye(H, dtype=F32)
    return dict(conv_w=conv_w, conv_b=conv_b.reshape(1, -1),
                dtb=dt_bias.reshape(1, H), alog=a_log.reshape(1, H),
                dtbT=_pair_layout(dt_bias, T), alogT=_pair_layout(a_log, T),
                dskip=jnp.repeat(d_skip, P).reshape(1, H * P), ng=norm_g.reshape(1, -1),
                expP=jnp.repeat(eye, P, axis=1).astype(BF16), expT=jnp.repeat(eye, T, axis=1).astype(BF16))


def kernel(x_prompt, x_sample, mem_prompt, cache_k, cache_v, cache_idx_k, cache_mem_k, cache_mem_v,
           state_ssm, state_conv, ln1_g, ln1_b, ffn1_w1, ffn1_w3, ffn1_w2, w_in, conv_w, conv_b,
           dt_bias, a_log, d_skip, ssd_norm_g, w_ssd_br, w_att_br, w_out, ln2_g, ln2_b,
           w_mq, w_mk, w_mv, w_mo, ln3_g, ln3_b, ffn2_w1, ffn2_w3, ffn2_w2, ln4_g, ln4_b):
    assert x_prompt.ndim == 3 and ln1_g.shape[0] == DEPTH == 1
    l = 0
    BP, LP, D = x_prompt.shape
    BS, LS, _ = x_sample.shape
    MP, MS = BP * LP, BS * LS
    M = MP + MS
    H = dt_bias.shape[1]
    assert H == SSD_HEADS
    DI = H * SSD_HEAD_DIM
    BCW = conv_w.shape[2] - DI
    KVH = cache_k.shape[3]
    DQ = KVH * Q_PER_KV * HEAD_DIM
    DK = KVH * HEAD_DIM
    DQI = IDX_HEADS * IDX_DIM
    MT = mem_prompt.shape[1]
    PAST = cache_k.shape[2]
    tm_big = M // 8
    tm_half = M // 16

    w0t = w_in[l].T
    c_dt = 2 * DI + BCW
    c_q = c_dt + H
    c_wi = c_q + DQ + 2 * DK + DQI
    c_ki = c_wi + IDX_HEADS
    c_g = c_ki + IDX_DIM
    assert c_g + 2 * D == w0t.shape[0]
    cols_ssd = dict(z=0, xs=DI, bc=2 * DI)
    cols_att = dict(q=0, k=DQ, v=DQ + DK, qi=DQ + 2 * DK)
    n_small = H + IDX_HEADS + IDX_DIM
    assert n_small <= US_W
    w_small_t = jnp.concatenate([w0t[c_dt:c_q], w0t[c_wi:c_g], jnp.zeros((US_W - n_small, D), F32)], axis=0)

    x_p2, x_s2 = x_prompt.reshape(MP, D), x_sample.reshape(MS, D)
    x0b = _cast_rows(x_p2, out_rows=M, out_row0=0)
    x0b, = _cast_rows(x_s2, out_rows=M, out_row0=MP, stacked=x0b)

    def ffn_delta(xb, w1, w3, w2, residual=None):
        h = _ffn_up(xb, w1[l], w3[l], tm=M // 4, tn=256)
        return _matmul(h, w2[l], tm=tm_big, tn=256, out_dtype=F32, single_buffer_x=True,
                       residual=residual, scale=0.5)

    d1 = ffn_delta(x0b, ffn1_w1, ffn1_w3, ffn1_w2)
    ln1 = _res_layer_norm(x_p2, d1, ln1_g[l], ln1_b[l], scale=0.5, n_rows=MP, out_rows=M)
    x1, x1b = _res_layer_norm(x_s2, d1, ln1_g[l], ln1_b[l], scale=0.5, n_rows=MS, d_row0=MP, out_row0=MP,
                              stacked=ln1)

    u_ssd = _matmul_nt(x1b, w0t, row0=0, n_rows=c_dt, tm=M // 4, tn=512, out_dtype=F32)
    u_att = _matmul_nt(x1b, w0t, row0=c_q, n_rows=c_wi - c_q, tm=M // 4, tn=512, out_dtype=F32)
    u_gate = _matmul_nt(x1b, w0t, row0=c_g, n_rows=2 * D, tm=M // 4, tn=512, out_dtype=F32)
    u_small = _matmul_nt(x1b, w_small_t, row0=0, n_rows=US_W, tm=tm_big, tn=US_W, out_dtype=F32)

    dt_raw = u_small[:, US_DT:US_DT + H]
    zeros_conv = jnp.zeros((BP,) + state_conv.shape[2:], F32)
    zeros_h = jnp.zeros((BP, DI, SSD_STATE), F32)
    prm_p = _ssd_params(conv_w[l], conv_b[l], dt_bias[l], a_log[l], d_skip[l], ssd_norm_g[l], CHUNK)
    y_ssd, conv_p, h_p = _ssd_mixer(u_ssd, u_small, _dt_transposed(dt_raw[:MP], CHUNK), zeros_conv, zeros_h,
                                    prm_p, cols_ssd, n_batch=BP, seq=LP, row0=0, T=CHUNK)
    TS = min(CHUNK, LS)
    prm_s = _ssd_params(conv_w[l], conv_b[l], dt_bias[l], a_log[l], d_skip[l], ssd_norm_g[l], TS)
    y_ssd, conv_s, h_s = _ssd_mixer(u_ssd, u_small, _dt_transposed(dt_raw[MP:], TS), state_conv[l],
                                    state_ssm[l].reshape(BS, DI, SSD_STATE),
                                    prm_s, cols_ssd, n_batch=BS, seq=LS, row0=MP, T=TS, stacked=(y_ssd,))

    k_all = u_att[:, cols_att['k']:cols_att['k'] + DK]
    v_all = u_att[:, cols_att['v']:cols_att['v'] + DK]
    ki_all = u_small[:, US_KI:US_KI + IDX_DIM]
    nkb = LP // DSA_KB
    vt5 = v_all[:MP].astype(BF16).reshape(BP, nkb, DSA_KB, KVH, HEAD_DIM).transpose(0, 1, 3, 4, 2)
    ones_row = (lax.broadcasted_iota(I32, (BP, nkb, KVH, VPAD, DSA_KB), 3) == 0).astype(BF16)
    vt_b = jnp.concatenate([vt5, ones_row], axis=3).reshape(BP, nkb, KVH * VROWS, DSA_KB)
    y_att = _dsa_prompt(u_att, u_small, ki_all[:MP].astype(BF16), k_all[:MP].astype(BF16),
                        vt_b, cols_att, n_batch=BP, seq=LP, kv_heads=KVH)
    y_att = _dsa_sample(u_att, u_small, cache_k[l].reshape(BS, PAST * KVH, HEAD_DIM),
                        cache_v[l].reshape(BS, PAST * KVH, HEAD_DIM), cache_idx_k[l], cols_att,
                        n_batch=BS, seq=LS, row0=MP, kv_heads=KVH, stacked=(y_att,))

    merged = _gated_merge(y_ssd, y_att, w_ssd_br[l], w_att_br[l], u_gate, 0, D, tm=tm_big, tn=256)
    s2 = _matmul(merged, w_out[l], tm=tm_big, tn=512, out_dtype=F32, residual=x1)
    x2, x2b = _res_layer_norm(s2, None, ln2_g[l], ln2_b[l], n_rows=M)

    memb = mem_prompt.reshape(BP * MT, D).astype(BF16)
    mk_p = _matmul(memb, w_mk[l], tm=BP * MT, tn=512, out_dtype=F32)
    mv_p = _matmul(memb, w_mv[l], tm=BP * MT, tn=512, out_dtype=F32)
    qm = _matmul(x2b, w_mq[l], tm=tm_big, tn=512, out_dtype=BF16)
    o_m = _memory_attention(qm, mk_p, mv_p, n_batch=BP, rows_per_batch=LP, row0=0, tq=min(512, LP))
    o_m = _memory_attention(qm, cache_mem_k[l].reshape(BS * MT, D), cache_mem_v[l].reshape(BS * MT, D),
                            n_batch=BS, rows_per_batch=LS, row0=MP, tq=LS, stacked=(o_m,))
    s3 = _matmul(o_m, w_mo[l], tm=tm_big, tn=512, out_dtype=F32, residual=x2)
    x3, x3b = _res_layer_norm(s3, None, ln3_g[l], ln3_b[l], n_rows=M)

    s4 = ffn_delta(x3b, ffn2_w1, ffn2_w3, ffn2_w2, residual=x3)
    y_p, = _res_layer_norm(s4, None, ln4_g[l], ln4_b[l], n_rows=MP, emit_bf16=False)
    y_s, = _res_layer_norm(s4, None, ln4_g[l], ln4_b[l], n_rows=MS, x_row0=MP, emit_bf16=False)

    mh = D // MEM_HEADS
    return (y_p.reshape(BP, LP, D), y_s.reshape(BS, LS, D),
            h_p.reshape(1, BP, H, SSD_HEAD_DIM, SSD_STATE), conv_p[None],
            k_all[:MP].reshape(1, BP, LP, KVH, HEAD_DIM), v_all[:MP].reshape(1, BP, LP, KVH, HEAD_DIM),
            ki_all[:MP].reshape(1, BP, LP, IDX_DIM),
            mk_p.reshape(1, BP, MT, MEM_HEADS, mh), mv_p.reshape(1, BP, MT, MEM_HEADS, mh),
            h_s.reshape(1, BS, H, SSD_HEAD_DIM, SSD_STATE), conv_s[None],
            k_all[MP:].reshape(1, BS, LS, KVH, HEAD_DIM), v_all[MP:].reshape(1, BS, LS, KVH, HEAD_DIM),
            ki_all[MP:].reshape(1, BS, LS, IDX_DIM))
```

```python
import functools
import math

import jax
import jax.numpy as jnp
from jax import lax
from jax.experimental import pallas as pl
from jax.experimental.pallas import tpu as pltpu

F32 = jnp.float32
BF16 = jnp.bfloat16
I32 = jnp.int32

DEPTH = 1
CHUNK = 64
SSD_HEADS = 64
SSD_HEAD_DIM = 64
SSD_HEADS_PER_GROUP = 8
SSD_STATE = 128
SSD_CONV_W = 4
HEAD_DIM = 128
Q_PER_KV = 4
IDX_HEADS = 16
IDX_DIM = 64
TOPK_MAX = 256
MEM_HEADS = 4
ALPHA = (2.0 * DEPTH) ** 0.25
LN_EPS = 1e-5
RMS_EPS = 1e-5

VMEM_LIMIT = 56 * 1024 * 1024

NEG = -1e30
INT_MIN = -2147483648
INT_MAX = 2147483647
LOG2E = math.log2(math.e)

DSA_QB = 128
DSA_KB = 512
DSA_GROUPS_PER_LOOP = 8
DSA_SAMPLE_STREAMS = 2
QK_AHEAD = 2
IDX_STACK = 4
VPAD = 16
VROWS = HEAD_DIM + VPAD
LN_ROWS = 256

US_DT = 0
US_WI = US_DT + SSD_HEADS
US_KI = US_WI + IDX_HEADS
US_W = 256


def _cparams(sem):
    return pltpu.CompilerParams(dimension_semantics=sem, vmem_limit_bytes=VMEM_LIMIT)


def _dot(a, b):
    return jnp.dot(a, b, preferred_element_type=F32)


def _dot_nt(a, b):
    return lax.dot_general(a, b, (((1,), (1,)), ((), ())), preferred_element_type=F32)


def _dot_tn(a, b):
    return lax.dot_general(a, b, (((0,), (0,)), ((), ())), preferred_element_type=F32)


def _dot_hi(a, b):
    return jnp.dot(a, b, preferred_element_type=F32, precision=lax.Precision.HIGHEST)


def _sigmoid(x):
    return 1.0 / (1.0 + jnp.exp(-x))


def _silu(x):
    return x * _sigmoid(x)


def _softplus(x):
    return jnp.maximum(x, 0.0) + jnp.log1p(jnp.exp(-jnp.abs(x)))


def _stacked_call(kernel, *, grid, in_specs, out_specs, out_shape, stacked, n_stacked, sem, scratch_shapes=()):
    if stacked is None:
        return pl.pallas_call(kernel, grid=grid, in_specs=in_specs, out_specs=out_specs, out_shape=out_shape,
                              scratch_shapes=scratch_shapes, compiler_params=_cparams(sem))
    n_in = len(in_specs)

    def body(*refs):
        kernel(*refs[:n_in], *refs[n_in + n_stacked:])

    call = pl.pallas_call(
        body, grid=grid,
        in_specs=list(in_specs) + [pl.BlockSpec(memory_space=pl.ANY)] * n_stacked,
        out_specs=out_specs, out_shape=out_shape, scratch_shapes=scratch_shapes,
        input_output_aliases={n_in + i: i for i in range(n_stacked)},
        compiler_params=_cparams(sem))
    return lambda *args: call(*args, *stacked)


def _mm_kernel(x_ref, w_ref, o_ref):
    o_ref[...] = _dot(x_ref[...], w_ref[...].astype(BF16)).astype(o_ref.dtype)


def _mm_res_kernel(x_ref, w_ref, r_ref, o_ref, *, scale):
    o_ref[...] = ALPHA * r_ref[...] + scale * _dot(x_ref[...], w_ref[...].astype(BF16))


def _matmul(x, w, *, tm, tn, out_dtype, single_buffer_x=False, residual=None, scale=1.0):
    M, K = x.shape
    N = w.shape[1]
    assert N % tn == 0 and M % tm == 0
    xmode = dict(pipeline_mode=pl.Buffered(1)) if single_buffer_x else {}
    tile = pl.BlockSpec((tm, tn), lambda i, j: (i, j))
    in_specs = [pl.BlockSpec((tm, K), lambda i, j: (i, 0), **xmode),
                pl.BlockSpec((K, tn), lambda i, j: (0, j))]
    if residual is None:
        body, args = _mm_kernel, (x, w)
    else:
        assert out_dtype == F32
        body, args = functools.partial(_mm_res_kernel, scale=scale), (x, w, residual)
        in_specs.append(tile)
    return pl.pallas_call(
        body,
        grid=(M // tm, N // tn),
        in_specs=in_specs,
        out_specs=tile,
        out_shape=jax.ShapeDtypeStruct((M, N), out_dtype),
        compiler_params=_cparams(("parallel", "arbitrary")),
    )(*args)


def _mm_nt_kernel(x_ref, wt_ref, o_ref):
    o_ref[...] = _dot_nt(x_ref[...], wt_ref[...].astype(BF16)).astype(o_ref.dtype)


def _matmul_nt(x, wt, *, row0, n_rows, tm, tn, out_dtype):
    M, K = x.shape
    assert n_rows % tn == 0 and M % tm == 0 and row0 % 8 == 0
    return pl.pallas_call(
        _mm_nt_kernel,
        grid=(M // tm, n_rows // tn),
        in_specs=[pl.BlockSpec((tm, K), lambda i, j: (i, 0), pipeline_mode=pl.Buffered(1)),
                  pl.BlockSpec((pl.Element(tn), pl.Element(K)), lambda i, j: (pl.multiple_of(row0 + j * tn, 8), 0))],
        out_specs=pl.BlockSpec((tm, tn), lambda i, j: (i, j)),
        out_shape=jax.ShapeDtypeStruct((M, n_rows), out_dtype),
        compiler_params=_cparams(("parallel", "arbitrary")),
    )(x, wt)


def _two_source_specs(xa, xb, tr):
    D = xa.shape[1]
    na, nb = xa.shape[0] // tr, xb.shape[0] // tr
    return (na, na + nb,
            pl.BlockSpec((tr, D), lambda i: (jnp.minimum(i, na - 1), 0)),
            pl.BlockSpec((tr, D), lambda i: (jnp.maximum(i - na, 0), 0)))


def _stacked_rows(xa_ref, xb_ref, n_first):
    return jnp.where(pl.program_id(0) < n_first, xa_ref[...], xb_ref[...])


def _stack_cast_kernel(xa_ref, xb_ref, o_ref, *, n_first):
    o_ref[...] = _stacked_rows(xa_ref, xb_ref, n_first).astype(o_ref.dtype)


def _stack_cast(xa, xb):
    D = xa.shape[1]
    tr = max(r for r in range(16, LN_ROWS + 1, 16) if xa.shape[0] % r == 0 and xb.shape[0] % r == 0)
    na, n, spec_a, spec_b = _two_source_specs(xa, xb, tr)
    return pl.pallas_call(
        functools.partial(_stack_cast_kernel, n_first=na), grid=(n,),
        in_specs=[spec_a, spec_b],
        out_specs=pl.BlockSpec((tr, D), lambda i: (i, 0)),
        out_shape=jax.ShapeDtypeStruct((n * tr, D), BF16),
        compiler_params=_cparams(("parallel",)),
    )(xa, xb)


def _ffn_up_kernel(x_ref, w1_ref, w3_ref, o_ref):
    x = x_ref[...]
    a = _dot(x, w1_ref[...].astype(BF16))
    b = _dot(x, w3_ref[...].astype(BF16))
    o_ref[...] = (_silu(a) * b).astype(o_ref.dtype)


def _ffn_up(x, w1, w3, *, tm, tn):
    M, K = x.shape
    N = w1.shape[1]
    return pl.pallas_call(
        _ffn_up_kernel,
        grid=(M // tm, N // tn),
        in_specs=[pl.BlockSpec((tm, K), lambda i, j: (i, 0), pipeline_mode=pl.Buffered(1)),
                  pl.BlockSpec((K, tn), lambda i, j: (0, j)),
                  pl.BlockSpec((K, tn), lambda i, j: (0, j))],
        out_specs=pl.BlockSpec((tm, tn), lambda i, j: (i, j)),
        out_shape=jax.ShapeDtypeStruct((M, N), BF16),
        compiler_params=_cparams(("parallel", "arbitrary")),
    )(x, w1, w3)


def _gate_kernel(ys_ref, ya_ref, ws_ref, wa_ref, gs_ref, ga_ref, o_ref):
    s = _dot(ys_ref[...], ws_ref[...].astype(BF16))
    a = _dot(ya_ref[...], wa_ref[...].astype(BF16))
    o_ref[...] = (_sigmoid(gs_ref[...]) * s + _sigmoid(ga_ref[...]) * a).astype(o_ref.dtype)


def _gated_merge(y_ssd, y_att, w_ssd, w_att, u_gate, gs_col, ga_col, *, tm, tn):
    M, K = y_ssd.shape
    N = w_ssd.shape[1]
    gs_blk, ga_blk = gs_col // tn, ga_col // tn
    once = dict(pipeline_mode=pl.Buffered(1))
    return pl.pallas_call(
        _gate_kernel,
        grid=(M // tm, N // tn),
        in_specs=[pl.BlockSpec((tm, K), lambda i, j: (i, 0), **once),
                  pl.BlockSpec((tm, K), lambda i, j: (i, 0), **once),
                  pl.BlockSpec((K, tn), lambda i, j: (0, j)),
                  pl.BlockSpec((K, tn), lambda i, j: (0, j)),
                  pl.BlockSpec((tm, tn), lambda i, j: (i, gs_blk + j)),
                  pl.BlockSpec((tm, tn), lambda i, j: (i, ga_blk + j))],
        out_specs=pl.BlockSpec((tm, tn), lambda i, j: (i, j)),
        out_shape=jax.ShapeDtypeStruct((M, N), BF16),
        compiler_params=_cparams(("parallel", "arbitrary")),
    )(y_ssd, y_att, w_ssd, w_att, u_gate, u_gate)


def _layer_norm_rows(y, g, b):
    mu = jnp.mean(y, axis=-1, keepdims=True)
    yc = y - mu
    var = jnp.mean(yc * yc, axis=-1, keepdims=True)
    return yc * lax.rsqrt(var + LN_EPS) * g + b


def _ln_two_source_kernel(xa_ref, xb_ref, d_ref, g_ref, b_ref, o_ref, ob_ref, *, scale, n_first):
    x = _stacked_rows(xa_ref, xb_ref, n_first)
    o = _layer_norm_rows(ALPHA * x + scale * d_ref[...], g_ref[...], b_ref[...])
    o_ref[...] = o
    ob_ref[...] = o.astype(BF16)


def _res_layer_norm_stacking(xa, xb, delta, g, b, *, scale):
    D = xa.shape[1]
    tr = max(r for r in range(16, LN_ROWS + 1, 16) if xa.shape[0] % r == 0 and xb.shape[0] % r == 0)
    na, n, spec_a, spec_b = _two_source_specs(xa, xb, tr)
    row = pl.BlockSpec((tr, D), lambda i: (i, 0))
    vec = pl.BlockSpec((1, D), lambda i: (0, 0))
    return pl.pallas_call(
        functools.partial(_ln_two_source_kernel, scale=scale, n_first=na), grid=(n,),
        in_specs=[spec_a, spec_b, row, vec, vec],
        out_specs=[row, row],
        out_shape=[jax.ShapeDtypeStruct((n * tr, D), F32), jax.ShapeDtypeStruct((n * tr, D), BF16)],
        compiler_params=_cparams(("parallel",)),
    )(xa, xb, delta, g.reshape(1, D), b.reshape(1, D))


def _ln_presummed_kernel(y_ref, g_ref, b_ref, *o_refs):
    o = _layer_norm_rows(y_ref[...], g_ref[...], b_ref[...])
    for o_ref in o_refs:
        o_ref[...] = o.astype(o_ref.dtype)


def _layer_norm(y, g, b, *, n_rows, row0=0, emit_bf16=True):
    D = y.shape[1]
    tr = max(r for r in range(8, LN_ROWS + 1, 8) if n_rows % r == 0 and row0 % r == 0)
    yb = row0 // tr
    vec = pl.BlockSpec((1, D), lambda i: (0, 0))
    dts = (F32, BF16) if emit_bf16 else (F32,)
    return pl.pallas_call(
        _ln_presummed_kernel, grid=(n_rows // tr,),
        in_specs=[pl.BlockSpec((tr, D), lambda i: (yb + i, 0)), vec, vec],
        out_specs=[pl.BlockSpec((tr, D), lambda i: (i, 0)) for _ in dts],
        out_shape=[jax.ShapeDtypeStruct((n_rows, D), dt) for dt in dts],
        compiler_params=_cparams(("parallel",)),
    )(y, g.reshape(1, D), b.reshape(1, D))


def _memattn_kernel(q_ref, k_ref, v_ref, o_ref):
    dh = q_ref.shape[1] // MEM_HEADS
    scale = dh ** -0.5
    for h in range(MEM_HEADS):
        sl = slice(h * dh, (h + 1) * dh)
        q = q_ref[:, sl]
        k = k_ref[:, sl].astype(BF16)
        v = v_ref[:, sl].astype(BF16)
        s = _dot_nt(q, k) * scale
        m = jnp.max(s, axis=-1, keepdims=True)
        p = jnp.exp(s - m)
        p = p / jnp.sum(p, axis=-1, keepdims=True)
        o_ref[:, sl] = _dot(p.astype(BF16), v).astype(o_ref.dtype)


def _memory_attention(q, mem_k, mem_v, *, n_batch, rows_per_batch, row0, tq, stacked=None):
    M, D = q.shape
    mt = mem_k.shape[0] // n_batch
    nq = rows_per_batch // tq
    blk0 = row0 // tq
    qspec = pl.BlockSpec((tq, D), lambda b, j: (blk0 + b * nq + j, 0))
    mspec = pl.BlockSpec((mt, D), lambda b, j: (b, 0))
    call = _stacked_call(
        _memattn_kernel, grid=(n_batch, nq), in_specs=[qspec, mspec, mspec], out_specs=[qspec],
        out_shape=[jax.ShapeDtypeStruct((M, D), BF16)], stacked=stacked, n_stacked=1,
        sem=("parallel", "arbitrary"))
    return call(q, mem_k, mem_v)[0]


def _ssd_kernel(z_ref, xs_ref, bc_ref, us_ref, dtT_ref, cprev_ref, h0_ref,
                convw_ref, convb_ref, dtb_ref, alog_ref, dtbT_ref, alogT_ref, dskip_ref, ng_ref,
                expP_ref, expT_ref,
                y_ref, cnew_ref, h_ref, extx_s, extbc_s, *, T, n_heads):
    P, R, N = SSD_HEAD_DIM, SSD_HEADS_PER_GROUP, SSD_STATE
    G = n_heads // R
    DI = n_heads * P
    GN = G * N
    c = pl.program_id(1)
    W1 = SSD_CONV_W - 1
    base = 8 - W1

    @pl.when(c == 0)
    def _():
        extx_s[0:base, :] = jnp.zeros((base, DI), F32)
        extbc_s[0:base, :] = jnp.zeros((base, 2 * GN), F32)
        extx_s[base:8, :] = cprev_ref[0, :, :DI]
        extbc_s[base:8, :] = cprev_ref[0, :, DI:]
        h_ref[0] = h0_ref[0]

    def conv(hist_s, x, lo, hi):
        xx = jnp.concatenate([hist_s[...], x], axis=0)
        out = convb_ref[:, lo:hi]
        for j in range(SSD_CONV_W):
            tap = xx if j == W1 else pltpu.roll(xx, W1 - j, axis=0)
            out = out + convw_ref[j:j + 1, lo:hi] * tap[8:8 + T]
        return _silu(out)

    x_in = xs_ref[...]
    bc_in = bc_ref[...]
    xs = conv(extx_s, x_in, 0, DI)
    bc = conv(extbc_s, bc_in, DI, DI + 2 * GN)
    newx = x_in[T - W1:T]
    newbc = bc_in[T - W1:T]
    cnew_ref[0, :, :DI] = newx
    cnew_ref[0, :, DI:] = newbc
    extx_s[base:8, :] = newx
    extbc_s[base:8, :] = newbc

    def expand(vals, e_ref):
        pieces = []
        for v in vals:
            for _ in range(3):
                p = v.astype(BF16)
                pieces.append(p)
                v = v - p.astype(F32)
        y = _dot(jnp.concatenate(pieces, axis=0), e_ref[...])
        return [y[(3 * n) * T:(3 * n + 1) * T] + y[(3 * n + 1) * T:(3 * n + 2) * T] + y[(3 * n + 2) * T:(3 * n + 3) * T]
                for n in range(len(vals))]

    dt = _softplus(us_ref[:, US_DT:US_DT + n_heads] + dtb_ref[...])
    d_a = dt * (-jnp.exp(alog_ref[...]))
    row = lax.broadcasted_iota(I32, (T, T), 0)
    col = lax.broadcasted_iota(I32, (T, T), 1)
    tri = (row >= col).astype(F32)
    a_cum = _dot_hi(tri, d_a)
    a_exp, dt_exp = expand([a_cum, dt], expP_ref)
    a_expt = a_exp if T == P else expand([a_cum], expT_ref)[0]
    d_a_t = _softplus(dtT_ref[0] + dtbT_ref[...]) * (-jnp.exp(alogT_ref[...]))
    r2 = lax.broadcasted_iota(I32, (2 * T, 2 * T), 0)
    c2 = lax.broadcasted_iota(I32, (2 * T, 2 * T), 1)
    tri2 = ((r2 // T == c2 // T) & (r2 <= c2)).astype(F32)
    a_cum_t = _dot_hi(d_a_t, tri2)

    l_idx = lax.broadcasted_iota(I32, (T, 2 * T), 0)
    j_idx = lax.broadcasted_iota(I32, (T, 2 * T), 1)
    causal2 = l_idx >= (j_idx % T)
    rr = lax.broadcasted_iota(I32, (2 * T, 2 * P), 0)
    cc = lax.broadcasted_iota(I32, (2 * T, 2 * P), 1)
    blockdiag = (rr // T) == (cc // P)
    chunk_decay = jnp.exp(a_cum_t)

    xdt = xs * dt_exp
    for g in range(G):
        gs = slice(g * R * P, (g + 1) * R * P)
        b_g = bc[:, g * N:(g + 1) * N].astype(BF16)
        c_g = bc[:, GN + g * N:GN + (g + 1) * N].astype(BF16)
        cb2 = _dot_nt(c_g, jnp.concatenate([b_g, b_g], axis=0))
        h_g = h_ref[0, gs, :]
        y_off = _dot_nt(c_g, h_g.astype(BF16)) * jnp.exp(a_exp[:, gs])
        pairs = []
        for pr in range(R // 2):
            i = g * (R // 2) + pr
            seg = a_expt[:, i * 2 * T:(i + 1) * 2 * T] - a_cum_t[i:i + 1, :]
            decay = jnp.exp(jnp.where(causal2, seg, -jnp.inf))
            m_pair = (cb2 * decay).astype(BF16)
            x_pair = xdt[:, i * 2 * P:(i + 1) * 2 * P]
            rhs = jnp.where(blockdiag, jnp.concatenate([x_pair, x_pair], axis=0), 0.0).astype(BF16)
            pairs.append(_dot(m_pair, rhs))
        y_g = jnp.concatenate(pairs, axis=1) + y_off + dskip_ref[:, gs] * xs[:, gs]
        a_g = a_exp[:, gs]
        dte = jnp.exp(a_g[T - 1:T, :] - a_g)
        st = _dot_tn((xdt[:, gs] * dte).astype(BF16), b_g)
        dec = jnp.concatenate(
            [jnp.broadcast_to(chunk_decay[(g * R + r) // 2:(g * R + r) // 2 + 1,
                                          ((g * R + r) % 2) * T + T - 1:((g * R + r) % 2) * T + T], (P, N))
             for r in range(R)], axis=0)
        h_ref[0, gs, :] = dec * h_g + st
        yg = y_g * _silu(z_ref[:, gs])
        ms = jnp.mean(yg * yg, axis=-1, keepdims=True)
        y_ref[:, gs] = (yg * lax.rsqrt(ms + RMS_EPS) * ng_ref[:, gs]).astype(y_ref.dtype)


def _ssd_mixer(u_ssd, u_small, dt_t, conv_prev, h0, prm, cols, *, n_batch, seq, row0, T, stacked=None):
    M = u_ssd.shape[0]
    n_heads = prm['dtb'].shape[1]
    DI = n_heads * SSD_HEAD_DIM
    BCW = conv_prev.shape[2] - DI
    nc = seq // T
    blk0 = row0 // T

    def rows(width, colblk):
        return pl.BlockSpec((T, width), lambda b, c: (blk0 + b * nc + c, colblk))

    def full(a):
        return pl.BlockSpec(a.shape, lambda b, c: (0,) * a.ndim)

    params = [prm['conv_w'], prm['conv_b'], prm['dtb'], prm['alog'], prm['dtbT'], prm['alogT'],
              prm['dskip'], prm['ng'], prm['expP'], prm['expT']]
    in_specs = [rows(DI, cols['z'] // DI), rows(DI, cols['xs'] // DI), rows(BCW, cols['bc'] // BCW),
                rows(US_W, 0),
                pl.BlockSpec((1,) + dt_t.shape[1:], lambda b, c: (b * nc + c, 0, 0)),
                pl.BlockSpec((1,) + conv_prev.shape[1:], lambda b, c: (b, 0, 0)),
                pl.BlockSpec((1,) + h0.shape[1:], lambda b, c: (b, 0, 0))] + [full(a) for a in params]
    out_specs = [rows(DI, 0),
                 pl.BlockSpec((1,) + conv_prev.shape[1:], lambda b, c: (b, 0, 0)),
                 pl.BlockSpec((1,) + h0.shape[1:], lambda b, c: (b, 0, 0))]
    call = _stacked_call(
        functools.partial(_ssd_kernel, T=T, n_heads=n_heads),
        grid=(n_batch, nc), in_specs=in_specs, out_specs=out_specs,
        out_shape=[jax.ShapeDtypeStruct((M, DI), BF16),
                   jax.ShapeDtypeStruct(conv_prev.shape, F32),
                   jax.ShapeDtypeStruct(h0.shape, F32)],
        stacked=stacked, n_stacked=1, sem=("parallel", "arbitrary"),
        scratch_shapes=[pltpu.VMEM((8, DI), F32), pltpu.VMEM((8, BCW), F32)])
    return call(u_ssd, u_ssd, u_ssd, u_small, dt_t, conv_prev, h0, *params)


def _sortable_key(x):
    b = lax.bitcast_convert_type(x, I32)
    return b ^ ((b >> 31) & 0x7FFFFFFF)


def _head_weights(us_ref):
    return us_ref[:, US_WI:US_WI + IDX_HEADS] * ((IDX_HEADS * IDX_DIM) ** -0.5)


def _indexer_scores(qi_heads, wi, ki_blk):
    Q = qi_heads[0].shape[0]
    acc = None
    for h0 in range(0, IDX_HEADS, IDX_STACK):
        l = _dot_nt(jnp.concatenate(qi_heads[h0:h0 + IDX_STACK], axis=0), ki_blk)
        for n in range(IDX_STACK):
            t = jnp.maximum(l[n * Q:(n + 1) * Q], 0.0) * wi[:, h0 + n:h0 + n + 1]
            acc = t if acc is None else acc + t
    return acc


def _kth_largest(count_ge, shape, k):
    c0 = count_ge(jnp.zeros(shape, I32))
    t0 = jnp.where(c0 >= k, 0, INT_MIN).astype(I32)

    def body(i, t):
        cand = t | (jnp.int32(1) << (30 - i))
        return jnp.where(count_ge(cand) >= k, cand, t)

    return lax.fori_loop(0, 31, body, t0)


def _tie_cutoff(count_ge, count_tie_below, thr, k, n_keys, cut_s):
    cut_s[...] = jnp.full(thr.shape, INT_MAX, I32)
    n_ge = count_ge(thr)

    @pl.when(jnp.max(n_ge) > k)
    def _():
        n_gt = jnp.where(thr == INT_MAX, 0, count_ge(thr + 1))
        need = k - n_gt

        def body(i, v):
            cand = v | (jnp.int32(1) << (n_keys.bit_length() - 1 - i))
            return jnp.where(count_tie_below(cand) < need, cand, v)

        v = lax.fori_loop(0, n_keys.bit_length(), body, jnp.zeros(thr.shape, I32))
        cut_s[...] = jnp.where(n_ge > k, v + 1, INT_MAX)


def _selected(key, idx, thr, cut):
    return (key > INT_MIN) & ((key > thr) | ((key == thr) & (idx < cut)))


def _limits(pos0, n_rows):
    pos = pos0 + lax.broadcasted_iota(I32, (n_rows, 1), 0)
    return (pos // CHUNK + 1) * CHUNK


def _stack_q_heads(q_ref, g, rows=slice(None)):
    parts = [q_ref[rows, (g * Q_PER_KV + i) * HEAD_DIM:(g * Q_PER_KV + i + 1) * HEAD_DIM] for i in range(Q_PER_KV)]
    return (jnp.concatenate(parts, axis=0) * (HEAD_DIM ** -0.5 * LOG2E)).astype(BF16)


def _dsa_prompt_kernel(q_ref, qi_ref, us_ref, ki_ref, k_ref, vt_ref, o_ref, key_s, bias_s, cut_s, *,
                       n_sel, kv_heads):
    QB, KB = DSA_QB, DSA_KB
    nkb_max = key_s.shape[0]
    j = pl.program_id(1)
    nkb = ((j + 1) * QB + KB - 1) // KB
    wi = _head_weights(us_ref)
    qi_heads = [qi_ref[:, h * IDX_DIM:(h + 1) * IDX_DIM].astype(BF16) for h in range(IDX_HEADS)]
    pos = j * QB + lax.broadcasted_iota(I32, (1, QB), 1)
    lim = (pos // CHUNK + 1) * CHUNK

    def score_blk(kb, carry):
        off = pl.multiple_of(kb * KB, KB)
        sc = _indexer_scores(qi_heads, wi, ki_ref[pl.ds(off, KB), :])
        s_idx = off + lax.broadcasted_iota(I32, (KB, QB), 0)
        key_s[kb] = jnp.where(s_idx < lim, _sortable_key(sc.T), INT_MIN)
        return carry

    lax.fori_loop(0, nkb, score_blk, 0)

    def count_ge(cand):
        def body(kb, acc):
            m = (key_s[kb] >= cand).astype(I32)
            return acc + jnp.sum(m.reshape(KB // 8, 8, QB), axis=0)
        acc = lax.fori_loop(0, nkb, body, jnp.zeros((8, QB), I32))
        return jnp.sum(acc, axis=0, keepdims=True)

    thr = _kth_largest(count_ge, (1, QB), n_sel)

    def key_index(kb):
        return kb * KB + lax.broadcasted_iota(I32, (KB, QB), 0)

    def count_tie_below(c):
        def body(kb, acc):
            m = ((key_s[kb] == thr) & (key_index(kb) < c)).astype(I32)
            return acc + jnp.sum(m.reshape(KB // 8, 8, QB), axis=0)
        acc = lax.fori_loop(0, nkb, body, jnp.zeros((8, QB), I32))
        return jnp.sum(acc, axis=0, keepdims=True)

    _tie_cutoff(count_ge, count_tie_below, thr, n_sel, nkb_max * KB, cut_s)
    cut = cut_s[...]

    def bias_blk(kb, carry):
        bias_s[kb] = jnp.where(_selected(key_s[kb], key_index(kb), thr, cut), 0.0, NEG)
        return carry

    lax.fori_loop(0, nkb, bias_blk, 0)

    cols = Q_PER_KV * QB
    GPL = DSA_GROUPS_PER_LOOP
    for g0 in range(0, kv_heads, GPL):
        groups = range(g0, g0 + GPL)
        qs = [_stack_q_heads(q_ref, g) for g in groups]

        def body(kb, carry):
            off = pl.multiple_of(kb * KB, KB)
            b = bias_s[kb]
            b4 = jnp.concatenate([b] * Q_PER_KV, axis=1)
            heads = [slice(g * HEAD_DIM, (g + 1) * HEAD_DIM) for g in groups]
            vrows = [slice(g * VROWS, (g + 1) * VROWS) for g in groups]

            def logits(n):
                return _dot_nt(k_ref[pl.ds(off, KB), heads[n]], qs[n]) + b4

            pending = [logits(n) for n in range(min(QK_AHEAD, GPL))]
            new = []
            for n in range(GPL):
                t = pending.pop(0)
                if n + QK_AHEAD < GPL:
                    pending.append(logits(n + QK_AHEAD))
                m, acc = carry[n]
                m_new = jnp.maximum(m, jnp.max(t, axis=0, keepdims=True))
                alpha = jnp.exp2(m - m_new)
                p = jnp.exp2(t - m_new)
                acc = alpha * acc + _dot(vt_ref[0, kb, vrows[n], :], p.astype(BF16))
                new.append((m_new, acc))
            return tuple(new)

        init = tuple((jnp.full((1, cols), NEG, F32), jnp.zeros((VROWS, cols), F32)) for _ in groups)
        res = lax.fori_loop(0, nkb, body, init)
        for n, g in enumerate(groups):
            acc = res[n][1]
            out = acc[:HEAD_DIM] / acc[HEAD_DIM:HEAD_DIM + 1]
            for i in range(Q_PER_KV):
                o_ref[:, (g * Q_PER_KV + i) * HEAD_DIM:(g * Q_PER_KV + i + 1) * HEAD_DIM] = (
                    out[:, i * QB:(i + 1) * QB].T.astype(o_ref.dtype))


def _dsa_prompt(u_att, u_small, ki_b, k_b, vt_b, cols, *, n_batch, seq, kv_heads):
    M = u_att.shape[0]
    QB, KB = DSA_QB, DSA_KB
    nq = seq // QB
    nkb_max = seq // KB
    DQ = kv_heads * Q_PER_KV * HEAD_DIM
    DK = kv_heads * HEAD_DIM
    DQI = IDX_HEADS * IDX_DIM
    n_sel = min(TOPK_MAX, seq // 4)
    return pl.pallas_call(
        functools.partial(_dsa_prompt_kernel, n_sel=n_sel, kv_heads=kv_heads),
        grid=(n_batch, nq),
        in_specs=[pl.BlockSpec((QB, DQ), lambda b, j: (b * nq + j, cols['q'] // DQ)),
                  pl.BlockSpec((QB, DQI), lambda b, j: (b * nq + j, cols['qi'] // DQI)),
                  pl.BlockSpec((QB, US_W), lambda b, j: (b * nq + j, 0)),
                  pl.BlockSpec((seq, IDX_DIM), lambda b, j: (b, 0), pipeline_mode=pl.Buffered(1)),
                  pl.BlockSpec((seq, DK), lambda b, j: (b, 0), pipeline_mode=pl.Buffered(1)),
                  pl.BlockSpec((1, nkb_max, kv_heads * VROWS, KB), lambda b, j: (b, 0, 0, 0),
                               pipeline_mode=pl.Buffered(1))],
        out_specs=pl.BlockSpec((QB, DQ), lambda b, j: (b * nq + j, 0)),
        out_shape=jax.ShapeDtypeStruct((M, DQ), BF16),
        scratch_shapes=[pltpu.VMEM((nkb_max, KB, QB), I32), pltpu.VMEM((nkb_max, KB, QB), F32),
                        pltpu.VMEM((1, QB), I32)],
        compiler_params=_cparams(("parallel", "arbitrary")),
    )(u_att, u_att, u_small, ki_b, k_b, vt_b)


def _dsa_sample_kernel(q_ref, qi_ref, us_ref, kn_ref, vn_ref, ck_ref, cv_ref, cki_ref, o_ref, cut_s, *,
                       n_sel, kv_heads, past, n_streams):
    R = q_ref.shape[0]
    Q = R // n_streams
    wi = _head_weights(us_ref)
    ki_new = us_ref[:, US_KI:US_KI + IDX_DIM].astype(BF16)
    sc_p, sc_n = [], []
    for s in range(n_streams):
        rs = slice(s * Q, (s + 1) * Q)
        qi_heads = [qi_ref[rs, h * IDX_DIM:(h + 1) * IDX_DIM].astype(BF16) for h in range(IDX_HEADS)]
        sc_p.append(_indexer_scores(qi_heads, wi[rs], cki_ref[s].astype(BF16)))
        sc_n.append(_indexer_scores(qi_heads, wi[rs], ki_new[rs]))
    sc_p = jnp.concatenate(sc_p, axis=0)
    sc_n = jnp.concatenate(sc_n, axis=0)
    lim = jnp.concatenate([_limits(past, Q)] * n_streams, axis=0)
    idx_p = lax.broadcasted_iota(I32, (R, past), 1)
    idx_n = past + lax.broadcasted_iota(I32, (R, Q), 1)
    key_p = jnp.where(idx_p < lim, _sortable_key(sc_p), INT_MIN)
    key_n = jnp.where(idx_n < lim, _sortable_key(sc_n), INT_MIN)

    def count_ge(cand):
        return (jnp.sum((key_p >= cand).astype(I32), axis=1, keepdims=True)
                + jnp.sum((key_n >= cand).astype(I32), axis=1, keepdims=True))

    thr = _kth_largest(count_ge, (R, 1), n_sel)

    def count_tie_below(c):
        return (jnp.sum(((key_p == thr) & (idx_p < c)).astype(I32), axis=1, keepdims=True)
                + jnp.sum(((key_n == thr) & (idx_n < c)).astype(I32), axis=1, keepdims=True))

    _tie_cutoff(count_ge, count_tie_below, thr, n_sel, past + Q, cut_s)
    cut = cut_s[...]
    bias_p = jnp.where(_selected(key_p, idx_p, thr, cut), 0.0, NEG)
    bias_n = jnp.where(_selected(key_n, idx_n, thr, cut), 0.0, NEG)

    for s in range(n_streams):
        rs = slice(s * Q, (s + 1) * Q)
        b_p = jnp.concatenate([bias_p[rs]] * Q_PER_KV, axis=0)
        b_n = jnp.concatenate([bias_n[rs]] * Q_PER_KV, axis=0)
        for g in range(kv_heads):
            hs = slice(g * HEAD_DIM, (g + 1) * HEAD_DIM)
            qs = _stack_q_heads(q_ref, g, rs)
            ck = ck_ref[s, pl.ds(g, past, stride=kv_heads), :].astype(BF16)
            cv = cv_ref[s, pl.ds(g, past, stride=kv_heads), :].astype(BF16)
            t_p = _dot_nt(qs, ck) + b_p
            t_n = _dot_nt(qs, kn_ref[rs, hs].astype(BF16)) + b_n
            m = jnp.maximum(jnp.max(t_p, axis=-1, keepdims=True), jnp.max(t_n, axis=-1, keepdims=True))
            p_p = jnp.exp2(t_p - m)
            p_n = jnp.exp2(t_n - m)
            l = jnp.sum(p_p, axis=-1, keepdims=True) + jnp.sum(p_n, axis=-1, keepdims=True)
            acc = _dot(p_p.astype(BF16), cv) + _dot(p_n.astype(BF16), vn_ref[rs, hs].astype(BF16))
            out = acc / l
            for i in range(Q_PER_KV):
                o_ref[rs, (g * Q_PER_KV + i) * HEAD_DIM:(g * Q_PER_KV + i + 1) * HEAD_DIM] = (
                    out[i * Q:(i + 1) * Q].astype(o_ref.dtype))


def _dsa_sample(u_att, u_small, cache_k, cache_v, cache_ki, cols, *, n_batch, seq, row0, kv_heads, stacked):
    past = cache_ki.shape[1]
    DQ = kv_heads * Q_PER_KV * HEAD_DIM
    DK = kv_heads * HEAD_DIM
    DQI = IDX_HEADS * IDX_DIM
    n_sel = min(TOPK_MAX, (past + seq) // 4)
    ns = DSA_SAMPLE_STREAMS
    rows = ns * seq
    assert n_batch % ns == 0 and row0 % rows == 0
    blk0 = row0 // rows
    call = _stacked_call(
        functools.partial(_dsa_sample_kernel, n_sel=n_sel, kv_heads=kv_heads, past=past, n_streams=ns),
        grid=(n_batch // ns,),
        in_specs=[pl.BlockSpec((rows, DQ), lambda b: (blk0 + b, cols['q'] // DQ)),
                  pl.BlockSpec((rows, DQI), lambda b: (blk0 + b, cols['qi'] // DQI)),
                  pl.BlockSpec((rows, US_W), lambda b: (blk0 + b, 0)),
                  pl.BlockSpec((rows, DK), lambda b: (blk0 + b, cols['k'] // DK)),
                  pl.BlockSpec((rows, DK), lambda b: (blk0 + b, cols['v'] // DK)),
                  pl.BlockSpec((ns, past * kv_heads, HEAD_DIM), lambda b: (b, 0, 0)),
                  pl.BlockSpec((ns, past * kv_heads, HEAD_DIM), lambda b: (b, 0, 0)),
                  pl.BlockSpec((ns, past, IDX_DIM), lambda b: (b, 0, 0))],
        out_specs=[pl.BlockSpec((rows, DQ), lambda b: (blk0 + b, 0))],
        out_shape=[jax.ShapeDtypeStruct(stacked[0].shape, BF16)],
        stacked=stacked, n_stacked=1, sem=("parallel",), scratch_shapes=[pltpu.VMEM((rows, 1), I32)])
    return call(u_att, u_att, u_small, u_att, u_att, cache_k, cache_v, cache_ki)[0]


def _pair_layout(v, T):
    return jnp.repeat(v.reshape(-1, 2), T, axis=1)


def _dt_transposed(dt_raw, T):
    n, H = dt_raw.shape
    return dt_raw.reshape(n // T, T, H // 2, 2).transpose(0, 2, 3, 1).reshape(n // T, H // 2, 2 * T)


def _ssd_params(conv_w, conv_b, dt_bias, a_log, d_skip, norm_g, T):
    H = dt_bias.shape[0]
    P = SSD_HEAD_DIM
    eye = jnp.eye(H, dtype=F32)
    return dict(conv_w=conv_w, conv_b=conv_b.reshape(1, -1),
                dtb=dt_bias.reshape(1, H), alog=a_log.reshape(1, H),
                dtbT=_pair_layout(dt_bias, T), alogT=_pair_layout(a_log, T),
                dskip=jnp.repeat(d_skip, P).reshape(1, H * P), ng=norm_g.reshape(1, -1),
                expP=jnp.repeat(eye, P, axis=1).astype(BF16), expT=jnp.repeat(eye, T, axis=1).astype(BF16))


def kernel(x_prompt, x_sample, mem_prompt, cache_k, cache_v, cache_idx_k, cache_mem_k, cache_mem_v,
           state_ssm, state_conv, ln1_g, ln1_b, ffn1_w1, ffn1_w3, ffn1_w2, w_in, conv_w, conv_b,
           dt_bias, a_log, d_skip, ssd_norm_g, w_ssd_br, w_att_br, w_out, ln2_g, ln2_b,
           w_mq, w_mk, w_mv, w_mo, ln3_g, ln3_b, ffn2_w1, ffn2_w3, ffn2_w2, ln4_g, ln4_b):
    assert x_prompt.ndim == 3 and ln1_g.shape[0] == DEPTH == 1
    l = 0
    BP, LP, D = x_prompt.shape
    BS, LS, _ = x_sample.shape
    MP, MS = BP * LP, BS * LS
    M = MP + MS
    H = dt_bias.shape[1]
    assert H == SSD_HEADS
    DI = H * SSD_HEAD_DIM
    BCW = conv_w.shape[2] - DI
    KVH = cache_k.shape[3]
    DQ = KVH * Q_PER_KV * HEAD_DIM
    DK = KVH * HEAD_DIM
    DQI = IDX_HEADS * IDX_DIM
    MT = mem_prompt.shape[1]
    PAST = cache_k.shape[2]
    tm_big = M // 8
    tm_half = M // 16

    w0t = w_in[l].T
    c_dt = 2 * DI + BCW
    c_q = c_dt + H
    c_wi = c_q + DQ + 2 * DK + DQI
    c_ki = c_wi + IDX_HEADS
    c_g = c_ki + IDX_DIM
    assert c_g + 2 * D == w0t.shape[0]
    cols_ssd = dict(z=0, xs=DI, bc=2 * DI)
    cols_att = dict(q=0, k=DQ, v=DQ + DK, qi=DQ + 2 * DK)
    n_small = H + IDX_HEADS + IDX_DIM
    assert n_small <= US_W
    w_small_t = jnp.concatenate([w0t[c_dt:c_q], w0t[c_wi:c_g], jnp.zeros((US_W - n_small, D), F32)], axis=0)

    x_p2, x_s2 = x_prompt.reshape(MP, D), x_sample.reshape(MS, D)
    x0b = _stack_cast(x_p2, x_s2)

    def ffn_delta(xb, w1, w3, w2, residual=None):
        h = _ffn_up(xb, w1[l], w3[l], tm=M // 4, tn=256)
        return _matmul(h, w2[l], tm=tm_big, tn=256, out_dtype=F32, single_buffer_x=True,
                       residual=residual, scale=0.5)

    d1 = ffn_delta(x0b, ffn1_w1, ffn1_w3, ffn1_w2)
    x1, x1b = _res_layer_norm_stacking(x_p2, x_s2, d1, ln1_g[l], ln1_b[l], scale=0.5)

    u_ssd = _matmul_nt(x1b, w0t, row0=0, n_rows=c_dt, tm=M // 4, tn=512, out_dtype=F32)
    u_att = _matmul_nt(x1b, w0t, row0=c_q, n_rows=c_wi - c_q, tm=M // 4, tn=512, out_dtype=F32)
    u_gate = _matmul_nt(x1b, w0t, row0=c_g, n_rows=2 * D, tm=M // 4, tn=512, out_dtype=F32)
    u_small = _matmul_nt(x1b, w_small_t, row0=0, n_rows=US_W, tm=tm_big, tn=US_W, out_dtype=F32)

    dt_raw = u_small[:, US_DT:US_DT + H]
    zeros_conv = jnp.zeros((BP,) + state_conv.shape[2:], F32)
    zeros_h = jnp.zeros((BP, DI, SSD_STATE), F32)
    prm_p = _ssd_params(conv_w[l], conv_b[l], dt_bias[l], a_log[l], d_skip[l], ssd_norm_g[l], CHUNK)
    y_ssd, conv_p, h_p = _ssd_mixer(u_ssd, u_small, _dt_transposed(dt_raw[:MP], CHUNK), zeros_conv, zeros_h,
                                    prm_p, cols_ssd, n_batch=BP, seq=LP, row0=0, T=CHUNK)
    TS = min(CHUNK, LS)
    prm_s = _ssd_params(conv_w[l], conv_b[l], dt_bias[l], a_log[l], d_skip[l], ssd_norm_g[l], TS)
    y_ssd, conv_s, h_s = _ssd_mixer(u_ssd, u_small, _dt_transposed(dt_raw[MP:], TS), state_conv[l],
                                    state_ssm[l].reshape(BS, DI, SSD_STATE),
                                    prm_s, cols_ssd, n_batch=BS, seq=LS, row0=MP, T=TS, stacked=(y_ssd,))

    k_all = u_att[:, cols_att['k']:cols_att['k'] + DK]
    v_all = u_att[:, cols_att['v']:cols_att['v'] + DK]
    ki_all = u_small[:, US_KI:US_KI + IDX_DIM]
    nkb = LP // DSA_KB
    vt5 = v_all[:MP].astype(BF16).reshape(BP, nkb, DSA_KB, KVH, HEAD_DIM).transpose(0, 1, 3, 4, 2)
    ones_row = (lax.broadcasted_iota(I32, (BP, nkb, KVH, VPAD, DSA_KB), 3) == 0).astype(BF16)
    vt_b = jnp.concatenate([vt5, ones_row], axis=3).reshape(BP, nkb, KVH * VROWS, DSA_KB)
    y_att = _dsa_prompt(u_att, u_small, ki_all[:MP].astype(BF16), k_all[:MP].astype(BF16),
                        vt_b, cols_att, n_batch=BP, seq=LP, kv_heads=KVH)
    y_att = _dsa_sample(u_att, u_small, cache_k[l].reshape(BS, PAST * KVH, HEAD_DIM),
                        cache_v[l].reshape(BS, PAST * KVH, HEAD_DIM), cache_idx_k[l], cols_att,
                        n_batch=BS, seq=LS, row0=MP, kv_heads=KVH, stacked=(y_att,))

    merged = _gated_merge(y_ssd, y_att, w_ssd_br[l], w_att_br[l], u_gate, 0, D, tm=tm_big, tn=256)
    s2 = _matmul(merged, w_out[l], tm=tm_big, tn=512, out_dtype=F32, residual=x1)
    x2, x2b = _layer_norm(s2, ln2_g[l], ln2_b[l], n_rows=M)

    memb = mem_prompt.reshape(BP * MT, D).astype(BF16)
    mk_p = _matmul(memb, w_mk[l], tm=BP * MT, tn=512, out_dtype=F32)
    mv_p = _matmul(memb, w_mv[l], tm=BP * MT, tn=512, out_dtype=F32)
    qm = _matmul(x2b, w_mq[l], tm=tm_big, tn=512, out_dtype=BF16)
    o_m = _memory_attention(qm, mk_p, mv_p, n_batch=BP, rows_per_batch=LP, row0=0, tq=min(512, LP))
    o_m = _memory_attention(qm, cache_mem_k[l].reshape(BS * MT, D), cache_mem_v[l].reshape(BS * MT, D),
                            n_batch=BS, rows_per_batch=LS, row0=MP, tq=LS, stacked=(o_m,))
    s3 = _matmul(o_m, w_mo[l], tm=tm_big, tn=512, out_dtype=F32, residual=x2)
    x3, x3b = _layer_norm(s3, ln3_g[l], ln3_b[l], n_rows=M)

    s4 = ffn_delta(x3b, ffn2_w1, ffn2_w3, ffn2_w2, residual=x3)
    y_p, = _layer_norm(s4, ln4_g[l], ln4_b[l], n_rows=MP, emit_bf16=False)
    y_s, = _layer_norm(s4, ln4_g[l], ln4_b[l], n_rows=MS, row0=MP, emit_bf16=False)

    mh = D // MEM_HEADS
    return (y_p.reshape(BP, LP, D), y_s.reshape(BS, LS, D),
            h_p.reshape(1, BP, H, SSD_HEAD_DIM, SSD_STATE), conv_p[None],
            k_all[:MP].reshape(1, BP, LP, KVH, HEAD_DIM), v_all[:MP].reshape(1, BP, LP, KVH, HEAD_DIM),
            ki_all[:MP].reshape(1, BP, LP, IDX_DIM),
            mk_p.reshape(1, BP, MT, MEM_HEADS, mh), mv_p.reshape(1, BP, MT, MEM_HEADS, mh),
            h_s.reshape(1, BS, H, SSD_HEAD_DIM, SSD_STATE), conv_s[None],
            k_all[MP:].reshape(1, BS, LS, KVH, HEAD_DIM), v_all[MP:].reshape(1, BS, LS, KVH, HEAD_DIM),
            ki_all[MP:].reshape(1, BS, LS, IDX_DIM))
```

```python
import functools
import math

import jax
import jax.numpy as jnp
from jax import lax
from jax.experimental import pallas as pl
from jax.experimental.pallas import tpu as pltpu

F32 = jnp.float32
BF16 = jnp.bfloat16
I32 = jnp.int32

DEPTH = 1
CHUNK = 64
SSD_HEADS = 64
SSD_HEAD_DIM = 64
SSD_HEADS_PER_GROUP = 8
SSD_STATE = 128
SSD_CONV_W = 4
HEAD_DIM = 128
Q_PER_KV = 4
IDX_HEADS = 16
IDX_DIM = 64
TOPK_MAX = 256
MEM_HEADS = 4
ALPHA = (2.0 * DEPTH) ** 0.25
LN_EPS = 1e-5
RMS_EPS = 1e-5

SUBLANES = 8
BF16_ROWS = 2 * SUBLANES
VMEM_LIMIT = 56 * 1024 * 1024

NEG = -1e30
INT_MIN = -2147483648
INT_MAX = 2147483647
LOG2E = math.log2(math.e)

DSA_QB = 128
DSA_KB = 512
DSA_GROUPS_PER_LOOP = 8
DSA_SAMPLE_STREAMS = 2
QK_AHEAD = 2
IDX_STACK = 4
VPAD = BF16_ROWS
VROWS = HEAD_DIM + VPAD
LN_ROWS = 256

US_DT = 0
US_WI = US_DT + SSD_HEADS
US_KI = US_WI + IDX_HEADS
US_W = 256


def _cparams(sem):
    return pltpu.CompilerParams(dimension_semantics=sem, vmem_limit_bytes=VMEM_LIMIT)


def _dot(a, b):
    return jnp.dot(a, b, preferred_element_type=F32)


def _dot_nt(a, b):
    return lax.dot_general(a, b, (((1,), (1,)), ((), ())), preferred_element_type=F32)


def _dot_tn(a, b):
    return lax.dot_general(a, b, (((0,), (0,)), ((), ())), preferred_element_type=F32)


def _dot_hi(a, b):
    return jnp.dot(a, b, preferred_element_type=F32, precision=lax.Precision.HIGHEST)


def _sigmoid(x):
    return 1.0 / (1.0 + jnp.exp(-x))


def _silu(x):
    return x * _sigmoid(x)


def _softplus(x):
    return jnp.maximum(x, 0.0) + jnp.log1p(jnp.exp(-jnp.abs(x)))


def _stacked_call(kernel, *, grid, in_specs, out_specs, out_shape, stacked, n_stacked, sem, scratch_shapes=()):
    if stacked is None:
        return pl.pallas_call(kernel, grid=grid, in_specs=in_specs, out_specs=out_specs, out_shape=out_shape,
                              scratch_shapes=scratch_shapes, compiler_params=_cparams(sem))
    n_in = len(in_specs)

    def body(*refs):
        kernel(*refs[:n_in], *refs[n_in + n_stacked:])

    call = pl.pallas_call(
        body, grid=grid,
        in_specs=list(in_specs) + [pl.BlockSpec(memory_space=pl.ANY)] * n_stacked,
        out_specs=out_specs, out_shape=out_shape, scratch_shapes=scratch_shapes,
        input_output_aliases={n_in + i: i for i in range(n_stacked)},
        compiler_params=_cparams(sem))
    return lambda *args: call(*args, *stacked)


def _mm_kernel(x_ref, w_ref, o_ref):
    o_ref[...] = _dot(x_ref[...], w_ref[...].astype(BF16)).astype(o_ref.dtype)


def _mm_res_kernel(x_ref, w_ref, r_ref, o_ref, *, scale):
    o_ref[...] = ALPHA * r_ref[...] + scale * _dot(x_ref[...], w_ref[...].astype(BF16))


def _matmul(x, w, *, tm, tn, out_dtype, single_buffer_x=False, residual=None, scale=1.0):
    M, K = x.shape
    N = w.shape[1]
    assert N % tn == 0 and M % tm == 0
    xmode = dict(pipeline_mode=pl.Buffered(1)) if single_buffer_x else {}
    tile = pl.BlockSpec((tm, tn), lambda i, j: (i, j))
    in_specs = [pl.BlockSpec((tm, K), lambda i, j: (i, 0), **xmode),
                pl.BlockSpec((K, tn), lambda i, j: (0, j))]
    if residual is None:
        body, args = _mm_kernel, (x, w)
    else:
        assert out_dtype == F32
        body, args = functools.partial(_mm_res_kernel, scale=scale), (x, w, residual)
        in_specs.append(tile)
    return pl.pallas_call(
        body,
        grid=(M // tm, N // tn),
        in_specs=in_specs,
        out_specs=tile,
        out_shape=jax.ShapeDtypeStruct((M, N), out_dtype),
        compiler_params=_cparams(("parallel", "arbitrary")),
    )(*args)


def _mm_nt_kernel(x_ref, wt_ref, o_ref):
    o_ref[...] = _dot_nt(x_ref[...], wt_ref[...].astype(BF16)).astype(o_ref.dtype)


def _matmul_nt(x, wt, *, row0, n_rows, tm, tn, out_dtype):
    M, K = x.shape
    assert n_rows % tn == 0 and M % tm == 0 and row0 % SUBLANES == 0
    return pl.pallas_call(
        _mm_nt_kernel,
        grid=(M // tm, n_rows // tn),
        in_specs=[pl.BlockSpec((tm, K), lambda i, j: (i, 0), pipeline_mode=pl.Buffered(1)),
                  pl.BlockSpec((pl.Element(tn), pl.Element(K)), lambda i, j: (pl.multiple_of(row0 + j * tn, SUBLANES), 0))],
        out_specs=pl.BlockSpec((tm, tn), lambda i, j: (i, j)),
        out_shape=jax.ShapeDtypeStruct((M, n_rows), out_dtype),
        compiler_params=_cparams(("parallel", "arbitrary")),
    )(x, wt)


def _two_source_specs(xa, xb, tr):
    D = xa.shape[1]
    na, nb = xa.shape[0] // tr, xb.shape[0] // tr
    return (na, na + nb,
            pl.BlockSpec((tr, D), lambda i: (jnp.minimum(i, na - 1), 0)),
            pl.BlockSpec((tr, D), lambda i: (jnp.maximum(i - na, 0), 0)))


def _stacked_rows(xa_ref, xb_ref, n_first):
    return jnp.where(pl.program_id(0) < n_first, xa_ref[...], xb_ref[...])


def _stack_cast_kernel(xa_ref, xb_ref, o_ref, *, n_first):
    o_ref[...] = _stacked_rows(xa_ref, xb_ref, n_first).astype(o_ref.dtype)


def _stack_cast(xa, xb):
    D = xa.shape[1]
    tr = max(r for r in range(BF16_ROWS, LN_ROWS + 1, BF16_ROWS) if xa.shape[0] % r == 0 and xb.shape[0] % r == 0)
    na, n, spec_a, spec_b = _two_source_specs(xa, xb, tr)
    return pl.pallas_call(
        functools.partial(_stack_cast_kernel, n_first=na), grid=(n,),
        in_specs=[spec_a, spec_b],
        out_specs=pl.BlockSpec((tr, D), lambda i: (i, 0)),
        out_shape=jax.ShapeDtypeStruct((n * tr, D), BF16),
        compiler_params=_cparams(("parallel",)),
    )(xa, xb)


def _ffn_up_kernel(x_ref, w1_ref, w3_ref, o_ref):
    x = x_ref[...]
    a = _dot(x, w1_ref[...].astype(BF16))
    b = _dot(x, w3_ref[...].astype(BF16))
    o_ref[...] = (_silu(a) * b).astype(o_ref.dtype)


def _ffn_up(x, w1, w3, *, tm, tn):
    M, K = x.shape
    N = w1.shape[1]
    return pl.pallas_call(
        _ffn_up_kernel,
        grid=(M // tm, N // tn),
        in_specs=[pl.BlockSpec((tm, K), lambda i, j: (i, 0), pipeline_mode=pl.Buffered(1)),
                  pl.BlockSpec((K, tn), lambda i, j: (0, j)),
                  pl.BlockSpec((K, tn), lambda i, j: (0, j))],
        out_specs=pl.BlockSpec((tm, tn), lambda i, j: (i, j)),
        out_shape=jax.ShapeDtypeStruct((M, N), BF16),
        compiler_params=_cparams(("parallel", "arbitrary")),
    )(x, w1, w3)


def _gate_kernel(ys_ref, ya_ref, ws_ref, wa_ref, gs_ref, ga_ref, o_ref):
    s = _dot(ys_ref[...], ws_ref[...].astype(BF16))
    a = _dot(ya_ref[...], wa_ref[...].astype(BF16))
    o_ref[...] = (_sigmoid(gs_ref[...]) * s + _sigmoid(ga_ref[...]) * a).astype(o_ref.dtype)


def _gated_merge(y_ssd, y_att, w_ssd, w_att, u_gate, gs_col, ga_col, *, tm, tn):
    M, K = y_ssd.shape
    N = w_ssd.shape[1]
    gs_blk, ga_blk = gs_col // tn, ga_col // tn
    once = dict(pipeline_mode=pl.Buffered(1))
    return pl.pallas_call(
        _gate_kernel,
        grid=(M // tm, N // tn),
        in_specs=[pl.BlockSpec((tm, K), lambda i, j: (i, 0), **once),
                  pl.BlockSpec((tm, K), lambda i, j: (i, 0), **once),
                  pl.BlockSpec((K, tn), lambda i, j: (0, j)),
                  pl.BlockSpec((K, tn), lambda i, j: (0, j)),
                  pl.BlockSpec((tm, tn), lambda i, j: (i, gs_blk + j)),
                  pl.BlockSpec((tm, tn), lambda i, j: (i, ga_blk + j))],
        out_specs=pl.BlockSpec((tm, tn), lambda i, j: (i, j)),
        out_shape=jax.ShapeDtypeStruct((M, N), BF16),
        compiler_params=_cparams(("parallel", "arbitrary")),
    )(y_ssd, y_att, w_ssd, w_att, u_gate, u_gate)


def _layer_norm_rows(y, g, b):
    mu = jnp.mean(y, axis=-1, keepdims=True)
    yc = y - mu
    var = jnp.mean(yc * yc, axis=-1, keepdims=True)
    return yc * lax.rsqrt(var + LN_EPS) * g + b


def _ln_two_source_kernel(xa_ref, xb_ref, d_ref, g_ref, b_ref, o_ref, ob_ref, *, scale, n_first):
    x = _stacked_rows(xa_ref, xb_ref, n_first)
    o = _layer_norm_rows(ALPHA * x + scale * d_ref[...], g_ref[...], b_ref[...])
    o_ref[...] = o
    ob_ref[...] = o.astype(BF16)


def _res_layer_norm_stacking(xa, xb, delta, g, b, *, scale):
    D = xa.shape[1]
    tr = max(r for r in range(BF16_ROWS, LN_ROWS + 1, BF16_ROWS) if xa.shape[0] % r == 0 and xb.shape[0] % r == 0)
    na, n, spec_a, spec_b = _two_source_specs(xa, xb, tr)
    row = pl.BlockSpec((tr, D), lambda i: (i, 0))
    vec = pl.BlockSpec((1, D), lambda i: (0, 0))
    return pl.pallas_call(
        functools.partial(_ln_two_source_kernel, scale=scale, n_first=na), grid=(n,),
        in_specs=[spec_a, spec_b, row, vec, vec],
        out_specs=[row, row],
        out_shape=[jax.ShapeDtypeStruct((n * tr, D), F32), jax.ShapeDtypeStruct((n * tr, D), BF16)],
        compiler_params=_cparams(("parallel",)),
    )(xa, xb, delta, g.reshape(1, D), b.reshape(1, D))


def _ln_presummed_kernel(y_ref, g_ref, b_ref, *o_refs):
    o = _layer_norm_rows(y_ref[...], g_ref[...], b_ref[...])
    for o_ref in o_refs:
        o_ref[...] = o.astype(o_ref.dtype)


def _layer_norm(y, g, b, *, n_rows, row0=0, emit_bf16=True):
    D = y.shape[1]
    tr = max(r for r in range(SUBLANES, LN_ROWS + 1, SUBLANES) if n_rows % r == 0 and row0 % r == 0)
    yb = row0 // tr
    vec = pl.BlockSpec((1, D), lambda i: (0, 0))
    dts = (F32, BF16) if emit_bf16 else (F32,)
    return pl.pallas_call(
        _ln_presummed_kernel, grid=(n_rows // tr,),
        in_specs=[pl.BlockSpec((tr, D), lambda i: (yb + i, 0)), vec, vec],
        out_specs=[pl.BlockSpec((tr, D), lambda i: (i, 0)) for _ in dts],
        out_shape=[jax.ShapeDtypeStruct((n_rows, D), dt) for dt in dts],
        compiler_params=_cparams(("parallel",)),
    )(y, g.reshape(1, D), b.reshape(1, D))


def _memattn_kernel(q_ref, k_ref, v_ref, o_ref):
    dh = q_ref.shape[1] // MEM_HEADS
    scale = dh ** -0.5
    for h in range(MEM_HEADS):
        sl = slice(h * dh, (h + 1) * dh)
        q = q_ref[:, sl]
        k = k_ref[:, sl].astype(BF16)
        v = v_ref[:, sl].astype(BF16)
        s = _dot_nt(q, k) * scale
        m = jnp.max(s, axis=-1, keepdims=True)
        p = jnp.exp(s - m)
        p = p / jnp.sum(p, axis=-1, keepdims=True)
        o_ref[:, sl] = _dot(p.astype(BF16), v).astype(o_ref.dtype)


def _memory_attention(q, mem_k, mem_v, *, n_batch, rows_per_batch, row0, tq, stacked=None):
    M, D = q.shape
    mt = mem_k.shape[0] // n_batch
    nq = rows_per_batch // tq
    blk0 = row0 // tq
    qspec = pl.BlockSpec((tq, D), lambda b, j: (blk0 + b * nq + j, 0))
    mspec = pl.BlockSpec((mt, D), lambda b, j: (b, 0))
    call = _stacked_call(
        _memattn_kernel, grid=(n_batch, nq), in_specs=[qspec, mspec, mspec], out_specs=[qspec],
        out_shape=[jax.ShapeDtypeStruct((M, D), BF16)], stacked=stacked, n_stacked=1,
        sem=("parallel", "arbitrary"))
    return call(q, mem_k, mem_v)[0]


def _ssd_kernel(z_ref, xs_ref, bc_ref, us_ref, dtT_ref, cprev_ref, h0_ref,
                convw_ref, convb_ref, dtb_ref, alog_ref, dtbT_ref, alogT_ref, dskip_ref, ng_ref,
                expP_ref, expT_ref,
                y_ref, cnew_ref, h_ref, extx_s, extbc_s, *, T, n_heads):
    P, R, N = SSD_HEAD_DIM, SSD_HEADS_PER_GROUP, SSD_STATE
    G = n_heads // R
    DI = n_heads * P
    GN = G * N
    c = pl.program_id(1)
    W1 = SSD_CONV_W - 1
    HT = SUBLANES
    base = HT - W1

    @pl.when(c == 0)
    def _():
        extx_s[0:base, :] = jnp.zeros((base, DI), F32)
        extbc_s[0:base, :] = jnp.zeros((base, 2 * GN), F32)
        extx_s[base:HT, :] = cprev_ref[0, :, :DI]
        extbc_s[base:HT, :] = cprev_ref[0, :, DI:]
        h_ref[0] = h0_ref[0]

    def conv(hist_s, x, lo, hi):
        xx = jnp.concatenate([hist_s[...], x], axis=0)
        out = convb_ref[:, lo:hi]
        for j in range(SSD_CONV_W):
            tap = xx if j == W1 else pltpu.roll(xx, W1 - j, axis=0)
            out = out + convw_ref[j:j + 1, lo:hi] * tap[HT:HT + T]
        return _silu(out)

    x_in = xs_ref[...]
    bc_in = bc_ref[...]
    xs = conv(extx_s, x_in, 0, DI)
    bc = conv(extbc_s, bc_in, DI, DI + 2 * GN)
    newx = x_in[T - W1:T]
    newbc = bc_in[T - W1:T]
    cnew_ref[0, :, :DI] = newx
    cnew_ref[0, :, DI:] = newbc
    extx_s[base:HT, :] = newx
    extbc_s[base:HT, :] = newbc

    def expand(vals, e_ref):
        pieces = []
        for v in vals:
            for _ in range(3):
                p = v.astype(BF16)
                pieces.append(p)
                v = v - p.astype(F32)
        y = _dot(jnp.concatenate(pieces, axis=0), e_ref[...])
        return [y[(3 * n) * T:(3 * n + 1) * T] + y[(3 * n + 1) * T:(3 * n + 2) * T] + y[(3 * n + 2) * T:(3 * n + 3) * T]
                for n in range(len(vals))]

    dt = _softplus(us_ref[:, US_DT:US_DT + n_heads] + dtb_ref[...])
    d_a = dt * (-jnp.exp(alog_ref[...]))
    row = lax.broadcasted_iota(I32, (T, T), 0)
    col = lax.broadcasted_iota(I32, (T, T), 1)
    tri = (row >= col).astype(F32)
    a_cum = _dot_hi(tri, d_a)
    a_exp, dt_exp = expand([a_cum, dt], expP_ref)
    a_expt = a_exp if T == P else expand([a_cum], expT_ref)[0]
    d_a_t = _softplus(dtT_ref[0] + dtbT_ref[...]) * (-jnp.exp(alogT_ref[...]))
    r2 = lax.broadcasted_iota(I32, (2 * T, 2 * T), 0)
    c2 = lax.broadcasted_iota(I32, (2 * T, 2 * T), 1)
    tri2 = ((r2 // T == c2 // T) & (r2 <= c2)).astype(F32)
    a_cum_t = _dot_hi(d_a_t, tri2)

    l_idx = lax.broadcasted_iota(I32, (T, 2 * T), 0)
    j_idx = lax.broadcasted_iota(I32, (T, 2 * T), 1)
    causal2 = l_idx >= (j_idx % T)
    rr = lax.broadcasted_iota(I32, (2 * T, 2 * P), 0)
    cc = lax.broadcasted_iota(I32, (2 * T, 2 * P), 1)
    blockdiag = (rr // T) == (cc // P)
    chunk_decay = jnp.exp(a_cum_t)

    xdt = xs * dt_exp
    for g in range(G):
        gs = slice(g * R * P, (g + 1) * R * P)
        b_g = bc[:, g * N:(g + 1) * N].astype(BF16)
        c_g = bc[:, GN + g * N:GN + (g + 1) * N].astype(BF16)
        cb2 = _dot_nt(c_g, jnp.concatenate([b_g, b_g], axis=0))
        h_g = h_ref[0, gs, :]
        y_off = _dot_nt(c_g, h_g.astype(BF16)) * jnp.exp(a_exp[:, gs])
        pairs = []
        for pr in range(R // 2):
            i = g * (R // 2) + pr
            seg = a_expt[:, i * 2 * T:(i + 1) * 2 * T] - a_cum_t[i:i + 1, :]
            decay = jnp.exp(jnp.where(causal2, seg, -jnp.inf))
            m_pair = (cb2 * decay).astype(BF16)
            x_pair = xdt[:, i * 2 * P:(i + 1) * 2 * P]
            rhs = jnp.where(blockdiag, jnp.concatenate([x_pair, x_pair], axis=0), 0.0).astype(BF16)
            pairs.append(_dot(m_pair, rhs))
        y_g = jnp.concatenate(pairs, axis=1) + y_off + dskip_ref[:, gs] * xs[:, gs]
        a_g = a_exp[:, gs]
        dte = jnp.exp(a_g[T - 1:T, :] - a_g)
        st = _dot_tn((xdt[:, gs] * dte).astype(BF16), b_g)
        dec = jnp.concatenate(
            [jnp.broadcast_to(chunk_decay[(g * R + r) // 2:(g * R + r) // 2 + 1,
                                          ((g * R + r) % 2) * T + T - 1:((g * R + r) % 2) * T + T], (P, N))
             for r in range(R)], axis=0)
        h_ref[0, gs, :] = dec * h_g + st
        yg = y_g * _silu(z_ref[:, gs])
        ms = jnp.mean(yg * yg, axis=-1, keepdims=True)
        y_ref[:, gs] = (yg * lax.rsqrt(ms + RMS_EPS) * ng_ref[:, gs]).astype(y_ref.dtype)


def _ssd_mixer(u_ssd, u_small, dt_t, conv_prev, h0, prm, cols, *, n_batch, seq, row0, T, stacked=None):
    M = u_ssd.shape[0]
    n_heads = prm['dtb'].shape[1]
    DI = n_heads * SSD_HEAD_DIM
    BCW = conv_prev.shape[2] - DI
    nc = seq // T
    blk0 = row0 // T

    def rows(width, colblk):
        return pl.BlockSpec((T, width), lambda b, c: (blk0 + b * nc + c, colblk))

    def full(a):
        return pl.BlockSpec(a.shape, lambda b, c: (0,) * a.ndim)

    params = [prm['conv_w'], prm['conv_b'], prm['dtb'], prm['alog'], prm['dtbT'], prm['alogT'],
              prm['dskip'], prm['ng'], prm['expP'], prm['expT']]
    in_specs = [rows(DI, cols['z'] // DI), rows(DI, cols['xs'] // DI), rows(BCW, cols['bc'] // BCW),
                rows(US_W, 0),
                pl.BlockSpec((1,) + dt_t.shape[1:], lambda b, c: (b * nc + c, 0, 0)),
                pl.BlockSpec((1,) + conv_prev.shape[1:], lambda b, c: (b, 0, 0)),
                pl.BlockSpec((1,) + h0.shape[1:], lambda b, c: (b, 0, 0))] + [full(a) for a in params]
    out_specs = [rows(DI, 0),
                 pl.BlockSpec((1,) + conv_prev.shape[1:], lambda b, c: (b, 0, 0)),
                 pl.BlockSpec((1,) + h0.shape[1:], lambda b, c: (b, 0, 0))]
    call = _stacked_call(
        functools.partial(_ssd_kernel, T=T, n_heads=n_heads),
        grid=(n_batch, nc), in_specs=in_specs, out_specs=out_specs,
        out_shape=[jax.ShapeDtypeStruct((M, DI), BF16),
                   jax.ShapeDtypeStruct(conv_prev.shape, F32),
                   jax.ShapeDtypeStruct(h0.shape, F32)],
        stacked=stacked, n_stacked=1, sem=("parallel", "arbitrary"),
        scratch_shapes=[pltpu.VMEM((SUBLANES, DI), F32), pltpu.VMEM((SUBLANES, BCW), F32)])
    return call(u_ssd, u_ssd, u_ssd, u_small, dt_t, conv_prev, h0, *params)


def _sortable_key(x):
    b = lax.bitcast_convert_type(x, I32)
    return b ^ ((b >> 31) & 0x7FFFFFFF)


def _head_weights(us_ref):
    return us_ref[:, US_WI:US_WI + IDX_HEADS] * ((IDX_HEADS * IDX_DIM) ** -0.5)


def _indexer_scores(qi_heads, wi, ki_blk):
    Q = qi_heads[0].shape[0]
    acc = None
    for h0 in range(0, IDX_HEADS, IDX_STACK):
        l = _dot_nt(jnp.concatenate(qi_heads[h0:h0 + IDX_STACK], axis=0), ki_blk)
        for n in range(IDX_STACK):
            t = jnp.maximum(l[n * Q:(n + 1) * Q], 0.0) * wi[:, h0 + n:h0 + n + 1]
            acc = t if acc is None else acc + t
    return acc


def _kth_largest(count_ge, shape, k):
    c0 = count_ge(jnp.zeros(shape, I32))
    t0 = jnp.where(c0 >= k, 0, INT_MIN).astype(I32)

    def body(i, t):
        cand = t | (jnp.int32(1) << (30 - i))
        return jnp.where(count_ge(cand) >= k, cand, t)

    return lax.fori_loop(0, 31, body, t0)


def _tie_cutoff(count_ge, count_tie_below, thr, k, n_keys, cut_s):
    cut_s[...] = jnp.full(thr.shape, INT_MAX, I32)
    n_ge = count_ge(thr)

    @pl.when(jnp.max(n_ge) > k)
    def _():
        n_gt = jnp.where(thr == INT_MAX, 0, count_ge(thr + 1))
        need = k - n_gt

        def body(i, v):
            cand = v | (jnp.int32(1) << (n_keys.bit_length() - 1 - i))
            return jnp.where(count_tie_below(cand) < need, cand, v)

        v = lax.fori_loop(0, n_keys.bit_length(), body, jnp.zeros(thr.shape, I32))
        cut_s[...] = jnp.where(n_ge > k, v + 1, INT_MAX)


def _selected(key, idx, thr, cut):
    return (key > INT_MIN) & ((key > thr) | ((key == thr) & (idx < cut)))


def _limits(pos0, n_rows):
    pos = pos0 + lax.broadcasted_iota(I32, (n_rows, 1), 0)
    return (pos // CHUNK + 1) * CHUNK


def _stack_q_heads(q_ref, g, rows=slice(None)):
    parts = [q_ref[rows, (g * Q_PER_KV + i) * HEAD_DIM:(g * Q_PER_KV + i + 1) * HEAD_DIM] for i in range(Q_PER_KV)]
    return (jnp.concatenate(parts, axis=0) * (HEAD_DIM ** -0.5 * LOG2E)).astype(BF16)


def _dsa_prompt_kernel(q_ref, qi_ref, us_ref, ki_ref, k_ref, vt_ref, o_ref, key_s, bias_s, cut_s, *,
                       n_sel, kv_heads):
    QB, KB = DSA_QB, DSA_KB
    nkb_max = key_s.shape[0]
    j = pl.program_id(1)
    nkb = ((j + 1) * QB + KB - 1) // KB
    wi = _head_weights(us_ref)
    qi_heads = [qi_ref[:, h * IDX_DIM:(h + 1) * IDX_DIM].astype(BF16) for h in range(IDX_HEADS)]
    pos = j * QB + lax.broadcasted_iota(I32, (1, QB), 1)
    lim = (pos // CHUNK + 1) * CHUNK

    def score_blk(kb, carry):
        off = pl.multiple_of(kb * KB, KB)
        sc = _indexer_scores(qi_heads, wi, ki_ref[pl.ds(off, KB), :])
        s_idx = off + lax.broadcasted_iota(I32, (KB, QB), 0)
        key_s[kb] = jnp.where(s_idx < lim, _sortable_key(sc.T), INT_MIN)
        return carry

    lax.fori_loop(0, nkb, score_blk, 0)

    def count_ge(cand):
        def body(kb, acc):
            m = (key_s[kb] >= cand).astype(I32)
            return acc + jnp.sum(m.reshape(KB // SUBLANES, SUBLANES, QB), axis=0)
        acc = lax.fori_loop(0, nkb, body, jnp.zeros((SUBLANES, QB), I32))
        return jnp.sum(acc, axis=0, keepdims=True)

    thr = _kth_largest(count_ge, (1, QB), n_sel)

    def key_index(kb):
        return kb * KB + lax.broadcasted_iota(I32, (KB, QB), 0)

    def count_tie_below(c):
        def body(kb, acc):
            m = ((key_s[kb] == thr) & (key_index(kb) < c)).astype(I32)
            return acc + jnp.sum(m.reshape(KB // SUBLANES, SUBLANES, QB), axis=0)
        acc = lax.fori_loop(0, nkb, body, jnp.zeros((SUBLANES, QB), I32))
        return jnp.sum(acc, axis=0, keepdims=True)

    _tie_cutoff(count_ge, count_tie_below, thr, n_sel, nkb_max * KB, cut_s)
    cut = cut_s[...]

    def bias_blk(kb, carry):
        bias_s[kb] = jnp.where(_selected(key_s[kb], key_index(kb), thr, cut), 0.0, NEG)
        return carry

    lax.fori_loop(0, nkb, bias_blk, 0)

    cols = Q_PER_KV * QB
    GPL = DSA_GROUPS_PER_LOOP
    for g0 in range(0, kv_heads, GPL):
        groups = range(g0, g0 + GPL)
        qs = [_stack_q_heads(q_ref, g) for g in groups]

        def body(kb, carry):
            off = pl.multiple_of(kb * KB, KB)
            b = bias_s[kb]
            b4 = jnp.concatenate([b] * Q_PER_KV, axis=1)
            heads = [slice(g * HEAD_DIM, (g + 1) * HEAD_DIM) for g in groups]
            vrows = [slice(g * VROWS, (g + 1) * VROWS) for g in groups]

            def logits(n):
                return _dot_nt(k_ref[pl.ds(off, KB), heads[n]], qs[n]) + b4

            pending = [logits(n) for n in range(min(QK_AHEAD, GPL))]
            new = []
            for n in range(GPL):
                t = pending.pop(0)
                if n + QK_AHEAD < GPL:
                    pending.append(logits(n + QK_AHEAD))
                m, acc = carry[n]
                m_new = jnp.maximum(m, jnp.max(t, axis=0, keepdims=True))
                alpha = jnp.exp2(m - m_new)
                p = jnp.exp2(t - m_new)
                acc = alpha * acc + _dot(vt_ref[0, kb, vrows[n], :], p.astype(BF16))
                new.append((m_new, acc))
            return tuple(new)

        init = tuple((jnp.full((1, cols), NEG, F32), jnp.zeros((VROWS, cols), F32)) for _ in groups)
        res = lax.fori_loop(0, nkb, body, init)
        for n, g in enumerate(groups):
            acc = res[n][1]
            out = acc[:HEAD_DIM] / acc[HEAD_DIM:HEAD_DIM + 1]
            for i in range(Q_PER_KV):
                o_ref[:, (g * Q_PER_KV + i) * HEAD_DIM:(g * Q_PER_KV + i + 1) * HEAD_DIM] = (
                    out[:, i * QB:(i + 1) * QB].T.astype(o_ref.dtype))


def _dsa_prompt(u_att, u_small, ki_b, k_b, vt_b, cols, *, n_batch, seq, kv_heads):
    M = u_att.shape[0]
    QB, KB = DSA_QB, DSA_KB
    nq = seq // QB
    nkb_max = seq // KB
    DQ = kv_heads * Q_PER_KV * HEAD_DIM
    DK = kv_heads * HEAD_DIM
    DQI = IDX_HEADS * IDX_DIM
    n_sel = min(TOPK_MAX, seq // 4)
    return pl.pallas_call(
        functools.partial(_dsa_prompt_kernel, n_sel=n_sel, kv_heads=kv_heads),
        grid=(n_batch, nq),
        in_specs=[pl.BlockSpec((QB, DQ), lambda b, j: (b * nq + j, cols['q'] // DQ)),
                  pl.BlockSpec((QB, DQI), lambda b, j: (b * nq + j, cols['qi'] // DQI)),
                  pl.BlockSpec((QB, US_W), lambda b, j: (b * nq + j, 0)),
                  pl.BlockSpec((seq, IDX_DIM), lambda b, j: (b, 0), pipeline_mode=pl.Buffered(1)),
                  pl.BlockSpec((seq, DK), lambda b, j: (b, 0), pipeline_mode=pl.Buffered(1)),
                  pl.BlockSpec((1, nkb_max, kv_heads * VROWS, KB), lambda b, j: (b, 0, 0, 0),
                               pipeline_mode=pl.Buffered(1))],
        out_specs=pl.BlockSpec((QB, DQ), lambda b, j: (b * nq + j, 0)),
        out_shape=jax.ShapeDtypeStruct((M, DQ), BF16),
        scratch_shapes=[pltpu.VMEM((nkb_max, KB, QB), I32), pltpu.VMEM((nkb_max, KB, QB), F32),
                        pltpu.VMEM((1, QB), I32)],
        compiler_params=_cparams(("parallel", "arbitrary")),
    )(u_att, u_att, u_small, ki_b, k_b, vt_b)


def _dsa_sample_kernel(q_ref, qi_ref, us_ref, kn_ref, vn_ref, ck_ref, cv_ref, cki_ref, o_ref, cut_s, *,
                       n_sel, kv_heads, past, n_streams):
    R = q_ref.shape[0]
    Q = R // n_streams
    wi = _head_weights(us_ref)
    ki_new = us_ref[:, US_KI:US_KI + IDX_DIM].astype(BF16)
    sc_p, sc_n = [], []
    for s in range(n_streams):
        rs = slice(s * Q, (s + 1) * Q)
        qi_heads = [qi_ref[rs, h * IDX_DIM:(h + 1) * IDX_DIM].astype(BF16) for h in range(IDX_HEADS)]
        sc_p.append(_indexer_scores(qi_heads, wi[rs], cki_ref[s].astype(BF16)))
        sc_n.append(_indexer_scores(qi_heads, wi[rs], ki_new[rs]))
    sc_p = jnp.concatenate(sc_p, axis=0)
    sc_n = jnp.concatenate(sc_n, axis=0)
    lim = jnp.concatenate([_limits(past, Q)] * n_streams, axis=0)
    idx_p = lax.broadcasted_iota(I32, (R, past), 1)
    idx_n = past + lax.broadcasted_iota(I32, (R, Q), 1)
    key_p = jnp.where(idx_p < lim, _sortable_key(sc_p), INT_MIN)
    key_n = jnp.where(idx_n < lim, _sortable_key(sc_n), INT_MIN)

    def count_ge(cand):
        return (jnp.sum((key_p >= cand).astype(I32), axis=1, keepdims=True)
                + jnp.sum((key_n >= cand).astype(I32), axis=1, keepdims=True))

    thr = _kth_largest(count_ge, (R, 1), n_sel)

    def count_tie_below(c):
        return (jnp.sum(((key_p == thr) & (idx_p < c)).astype(I32), axis=1, keepdims=True)
                + jnp.sum(((key_n == thr) & (idx_n < c)).astype(I32), axis=1, keepdims=True))

    _tie_cutoff(count_ge, count_tie_below, thr, n_sel, past + Q, cut_s)
    cut = cut_s[...]
    bias_p = jnp.where(_selected(key_p, idx_p, thr, cut), 0.0, NEG)
    bias_n = jnp.where(_selected(key_n, idx_n, thr, cut), 0.0, NEG)

    for s in range(n_streams):
        rs = slice(s * Q, (s + 1) * Q)
        b_p = jnp.concatenate([bias_p[rs]] * Q_PER_KV, axis=0)
        b_n = jnp.concatenate([bias_n[rs]] * Q_PER_KV, axis=0)
        for g in range(kv_heads):
            hs = slice(g * HEAD_DIM, (g + 1) * HEAD_DIM)
            qs = _stack_q_heads(q_ref, g, rs)
            ck = ck_ref[s, pl.ds(g, past, stride=kv_heads), :].astype(BF16)
            cv = cv_ref[s, pl.ds(g, past, stride=kv_heads), :].astype(BF16)
            t_p = _dot_nt(qs, ck) + b_p
            t_n = _dot_nt(qs, kn_ref[rs, hs].astype(BF16)) + b_n
            m = jnp.maximum(jnp.max(t_p, axis=-1, keepdims=True), jnp.max(t_n, axis=-1, keepdims=True))
            p_p = jnp.exp2(t_p - m)
            p_n = jnp.exp2(t_n - m)
            l = jnp.sum(p_p, axis=-1, keepdims=True) + jnp.sum(p_n, axis=-1, keepdims=True)
            acc = _dot(p_p.astype(BF16), cv) + _dot(p_n.astype(BF16), vn_ref[rs, hs].astype(BF16))
            out = acc / l
            for i in range(Q_PER_KV):
                o_ref[rs, (g * Q_PER_KV + i) * HEAD_DIM:(g * Q_PER_KV + i + 1) * HEAD_DIM] = (
                    out[i * Q:(i + 1) * Q].astype(o_ref.dtype))


def _dsa_sample(u_att, u_small, cache_k, cache_v, cache_ki, cols, *, n_batch, seq, row0, kv_heads, stacked):
    past = cache_ki.shape[1]
    DQ = kv_heads * Q_PER_KV * HEAD_DIM
    DK = kv_heads * HEAD_DIM
    DQI = IDX_HEADS * IDX_DIM
    n_sel = min(TOPK_MAX, (past + seq) // 4)
    ns = DSA_SAMPLE_STREAMS
    rows = ns * seq
    assert n_batch % ns == 0 and row0 % rows == 0
    blk0 = row0 // rows
    call = _stacked_call(
        functools.partial(_dsa_sample_kernel, n_sel=n_sel, kv_heads=kv_heads, past=past, n_streams=ns),
        grid=(n_batch // ns,),
        in_specs=[pl.BlockSpec((rows, DQ), lambda b: (blk0 + b, cols['q'] // DQ)),
                  pl.BlockSpec((rows, DQI), lambda b: (blk0 + b, cols['qi'] // DQI)),
                  pl.BlockSpec((rows, US_W), lambda b: (blk0 + b, 0)),
                  pl.BlockSpec((rows, DK), lambda b: (blk0 + b, cols['k'] // DK)),
                  pl.BlockSpec((rows, DK), lambda b: (blk0 + b, cols['v'] // DK)),
                  pl.BlockSpec((ns, past * kv_heads, HEAD_DIM), lambda b: (b, 0, 0)),
                  pl.BlockSpec((ns, past * kv_heads, HEAD_DIM), lambda b: (b, 0, 0)),
                  pl.BlockSpec((ns, past, IDX_DIM), lambda b: (b, 0, 0))],
        out_specs=[pl.BlockSpec((rows, DQ), lambda b: (blk0 + b, 0))],
        out_shape=[jax.ShapeDtypeStruct(stacked[0].shape, BF16)],
        stacked=stacked, n_stacked=1, sem=("parallel",), scratch_shapes=[pltpu.VMEM((rows, 1), I32)])
    return call(u_att, u_att, u_small, u_att, u_att, cache_k, cache_v, cache_ki)[0]


def _pair_layout(v, T):
    return jnp.repeat(v.reshape(-1, 2), T, axis=1)


def _dt_transposed(dt_raw, T):
    n, H = dt_raw.shape
    return dt_raw.reshape(n // T, T, H // 2, 2).transpose(0, 2, 3, 1).reshape(n // T, H // 2, 2 * T)


def _ssd_params(conv_w, conv_b, dt_bias, a_log, d_skip, norm_g, T):
    H = dt_bias.shape[0]
    P = SSD_HEAD_DIM
    eye = jnp.eye(H, dtype=F32)
    return dict(conv_w=conv_w, conv_b=conv_b.reshape(1, -1),
                dtb=dt_bias.reshape(1, H), alog=a_log.reshape(1, H),
                dtbT=_pair_layout(dt_bias, T), alogT=_pair_layout(a_log, T),
                dskip=jnp.repeat(d_skip, P).reshape(1, H * P), ng=norm_g.reshape(1, -1),
                expP=jnp.repeat(eye, P, axis=1).astype(BF16), expT=jnp.repeat(eye, T, axis=1).astype(BF16))


def kernel(x_prompt, x_sample, mem_prompt, cache_k, cache_v, cache_idx_k, cache_mem_k, cache_mem_v,
           state_ssm, state_conv, ln1_g, ln1_b, ffn1_w1, ffn1_w3, ffn1_w2, w_in, conv_w, conv_b,
           dt_bias, a_log, d_skip, ssd_norm_g, w_ssd_br, w_att_br, w_out, ln2_g, ln2_b,
           w_mq, w_mk, w_mv, w_mo, ln3_g, ln3_b, ffn2_w1, ffn2_w3, ffn2_w2, ln4_g, ln4_b):
    assert x_prompt.ndim == 3 and ln1_g.shape[0] == DEPTH == 1
    l = 0
    BP, LP, D = x_prompt.shape
    BS, LS, _ = x_sample.shape
    MP, MS = BP * LP, BS * LS
    M = MP + MS
    H = dt_bias.shape[1]
    assert H == SSD_HEADS
    DI = H * SSD_HEAD_DIM
    BCW = conv_w.shape[2] - DI
    KVH = cache_k.shape[3]
    DQ = KVH * Q_PER_KV * HEAD_DIM
    DK = KVH * HEAD_DIM
    DQI = IDX_HEADS * IDX_DIM
    MT = mem_prompt.shape[1]
    PAST = cache_k.shape[2]
    tm_big = M // 8
    tm_half = M // 16

    w0t = w_in[l].T
    c_dt = 2 * DI + BCW
    c_q = c_dt + H
    c_wi = c_q + DQ + 2 * DK + DQI
    c_ki = c_wi + IDX_HEADS
    c_g = c_ki + IDX_DIM
    assert c_g + 2 * D == w0t.shape[0]
    cols_ssd = dict(z=0, xs=DI, bc=2 * DI)
    cols_att = dict(q=0, k=DQ, v=DQ + DK, qi=DQ + 2 * DK)
    n_small = H + IDX_HEADS + IDX_DIM
    assert n_small <= US_W
    w_small_t = jnp.concatenate([w0t[c_dt:c_q], w0t[c_wi:c_g], jnp.zeros((US_W - n_small, D), F32)], axis=0)

    x_p2, x_s2 = x_prompt.reshape(MP, D), x_sample.reshape(MS, D)
    x0b = _stack_cast(x_p2, x_s2)

    def ffn_delta(xb, w1, w3, w2, residual=None):
        h = _ffn_up(xb, w1[l], w3[l], tm=M // 4, tn=256)
        return _matmul(h, w2[l], tm=tm_big, tn=256, out_dtype=F32, single_buffer_x=True,
                       residual=residual, scale=0.5)

    d1 = ffn_delta(x0b, ffn1_w1, ffn1_w3, ffn1_w2)
    x1, x1b = _res_layer_norm_stacking(x_p2, x_s2, d1, ln1_g[l], ln1_b[l], scale=0.5)

    u_ssd = _matmul_nt(x1b, w0t, row0=0, n_rows=c_dt, tm=M // 4, tn=512, out_dtype=F32)
    u_att = _matmul_nt(x1b, w0t, row0=c_q, n_rows=c_wi - c_q, tm=M // 4, tn=512, out_dtype=F32)
    u_gate = _matmul_nt(x1b, w0t, row0=c_g, n_rows=2 * D, tm=M // 4, tn=512, out_dtype=F32)
    u_small = _matmul_nt(x1b, w_small_t, row0=0, n_rows=US_W, tm=tm_big, tn=US_W, out_dtype=F32)

    dt_raw = u_small[:, US_DT:US_DT + H]
    zeros_conv = jnp.zeros((BP,) + state_conv.shape[2:], F32)
    zeros_h = jnp.zeros((BP, DI, SSD_STATE), F32)
    prm_p = _ssd_params(conv_w[l], conv_b[l], dt_bias[l], a_log[l], d_skip[l], ssd_norm_g[l], CHUNK)
    y_ssd, conv_p, h_p = _ssd_mixer(u_ssd, u_small, _dt_transposed(dt_raw[:MP], CHUNK), zeros_conv, zeros_h,
                                    prm_p, cols_ssd, n_batch=BP, seq=LP, row0=0, T=CHUNK)
    TS = min(CHUNK, LS)
    prm_s = _ssd_params(conv_w[l], conv_b[l], dt_bias[l], a_log[l], d_skip[l], ssd_norm_g[l], TS)
    y_ssd, conv_s, h_s = _ssd_mixer(u_ssd, u_small, _dt_transposed(dt_raw[MP:], TS), state_conv[l],
                                    state_ssm[l].reshape(BS, DI, SSD_STATE),
                                    prm_s, cols_ssd, n_batch=BS, seq=LS, row0=MP, T=TS, stacked=(y_ssd,))

    k_all = u_att[:, cols_att['k']:cols_att['k'] + DK]
    v_all = u_att[:, cols_att['v']:cols_att['v'] + DK]
    ki_all = u_small[:, US_KI:US_KI + IDX_DIM]
    nkb = LP // DSA_KB
    vt5 = v_all[:MP].astype(BF16).reshape(BP, nkb, DSA_KB, KVH, HEAD_DIM).transpose(0, 1, 3, 4, 2)
    ones_row = (lax.broadcasted_iota(I32, (BP, nkb, KVH, VPAD, DSA_KB), 3) == 0).astype(BF16)
    vt_b = jnp.concatenate([vt5, ones_row], axis=3).reshape(BP, nkb, KVH * VROWS, DSA_KB)
    y_att = _dsa_prompt(u_att, u_small, ki_all[:MP].astype(BF16), k_all[:MP].astype(BF16),
                        vt_b, cols_att, n_batch=BP, seq=LP, kv_heads=KVH)
    y_att = _dsa_sample(u_att, u_small, cache_k[l].reshape(BS, PAST * KVH, HEAD_DIM),
                        cache_v[l].reshape(BS, PAST * KVH, HEAD_DIM), cache_idx_k[l], cols_att,
                        n_batch=BS, seq=LS, row0=MP, kv_heads=KVH, stacked=(y_att,))

    merged = _gated_merge(y_ssd, y_att, w_ssd_br[l], w_att_br[l], u_gate, 0, D, tm=tm_big, tn=256)
    s2 = _matmul(merged, w_out[l], tm=tm_big, tn=512, out_dtype=F32, residual=x1)
    x2, x2b = _layer_norm(s2, ln2_g[l], ln2_b[l], n_rows=M)

    memb = mem_prompt.reshape(BP * MT, D).astype(BF16)
    mk_p = _matmul(memb, w_mk[l], tm=BP * MT, tn=512, out_dtype=F32)
    mv_p = _matmul(memb, w_mv[l], tm=BP * MT, tn=512, out_dtype=F32)
    qm = _matmul(x2b, w_mq[l], tm=tm_big, tn=512, out_dtype=BF16)
    o_m = _memory_attention(qm, mk_p, mv_p, n_batch=BP, rows_per_batch=LP, row0=0, tq=min(512, LP))
    o_m = _memory_attention(qm, cache_mem_k[l].reshape(BS * MT, D), cache_mem_v[l].reshape(BS * MT, D),
                            n_batch=BS, rows_per_batch=LS, row0=MP, tq=LS, stacked=(o_m,))
    s3 = _matmul(o_m, w_mo[l], tm=tm_big, tn=512, out_dtype=F32, residual=x2)
    x3, x3b = _layer_norm(s3, ln3_g[l], ln3_b[l], n_rows=M)

    s4 = ffn_delta(x3b, ffn2_w1, ffn2_w3, ffn2_w2, residual=x3)
    y_p, = _layer_norm(s4, ln4_g[l], ln4_b[l], n_rows=MP, emit_bf16=False)
    y_s, = _layer_norm(s4, ln4_g[l], ln4_b[l], n_rows=MS, row0=MP, emit_bf16=False)

    mh = D // MEM_HEADS
    return (y_p.reshape(BP, LP, D), y_s.reshape(BS, LS, D),
            h_p.reshape(1, BP, H, SSD_HEAD_DIM, SSD_STATE), conv_p[None],
            k_all[:MP].reshape(1, BP, LP, KVH, HEAD_DIM), v_all[:MP].reshape(1, BP, LP, KVH, HEAD_DIM),
            ki_all[:MP].reshape(1, BP, LP, IDX_DIM),
            mk_p.reshape(1, BP, MT, MEM_HEADS, mh), mv_p.reshape(1, BP, MT, MEM_HEADS, mh),
            h_s.reshape(1, BS, H, SSD_HEAD_DIM, SSD_STATE), conv_s[None],
            k_all[MP:].reshape(1, BS, LS, KVH, HEAD_DIM), v_all[MP:].reshape(1, BS, LS, KVH, HEAD_DIM),
            ki_all[MP:].reshape(1, BS, LS, IDX_DIM))
```

```python
import functools
import math

import jax
import jax.numpy as jnp
from jax import lax
from jax.experimental import pallas as pl
from jax.experimental.pallas import tpu as pltpu

F32 = jnp.float32
BF16 = jnp.bfloat16
I32 = jnp.int32

DEPTH = 1
CHUNK = 64
SSD_HEADS = 64
SSD_HEAD_DIM = 64
SSD_HEADS_PER_GROUP = 8
SSD_STATE = 128
SSD_CONV_W = 4
HEAD_DIM = 128
Q_PER_KV = 4
IDX_HEADS = 16
IDX_DIM = 64
TOPK_MAX = 256
MEM_HEADS = 4
ALPHA = (2.0 * DEPTH) ** 0.25
LN_EPS = 1e-5
RMS_EPS = 1e-5

SUBLANES = 8
BF16_ROWS = 2 * SUBLANES
VMEM_LIMIT = 56 * 1024 * 1024

NEG = -1e30
INT_MIN = -2147483648
INT_MAX = 2147483647
LOG2E = math.log2(math.e)

DSA_QB = 128
DSA_KB = 512
DSA_GROUPS_PER_LOOP = 8
DSA_SAMPLE_STREAMS = 2
QK_AHEAD = 2
IDX_STACK = 4
VPAD = BF16_ROWS
VROWS = HEAD_DIM + VPAD
LN_ROWS = 256
MAX_ROW_CHUNKS = 8

US_DT = 0
US_WI = US_DT + SSD_HEADS
US_KI = US_WI + IDX_HEADS
US_W = 256


def _cparams(sem):
    return pltpu.CompilerParams(dimension_semantics=sem, vmem_limit_bytes=VMEM_LIMIT)


def _dot(a, b):
    return jnp.dot(a, b, preferred_element_type=F32)


def _dot_nt(a, b):
    return lax.dot_general(a, b, (((1,), (1,)), ((), ())), preferred_element_type=F32)


def _dot_tn(a, b):
    return lax.dot_general(a, b, (((0,), (0,)), ((), ())), preferred_element_type=F32)


def _dot_hi(a, b):
    return jnp.dot(a, b, preferred_element_type=F32, precision=lax.Precision.HIGHEST)


def _row_chunks(n_rows):
    n = MAX_ROW_CHUNKS
    while n > 1 and n_rows % (n * BF16_ROWS):
        n //= 2
    return [slice(c * (n_rows // n), (c + 1) * (n_rows // n)) for c in range(n)]


def _sigmoid(x):
    return 1.0 / (1.0 + jnp.exp(-x))


def _silu(x):
    return x * _sigmoid(x)


def _softplus(x):
    return jnp.maximum(x, 0.0) + jnp.log1p(jnp.exp(-jnp.abs(x)))


def _stacked_call(kernel, *, grid, in_specs, out_specs, out_shape, stacked, n_stacked, sem, scratch_shapes=()):
    if stacked is None:
        return pl.pallas_call(kernel, grid=grid, in_specs=in_specs, out_specs=out_specs, out_shape=out_shape,
                              scratch_shapes=scratch_shapes, compiler_params=_cparams(sem))
    n_in = len(in_specs)

    def body(*refs):
        kernel(*refs[:n_in], *refs[n_in + n_stacked:])

    call = pl.pallas_call(
        body, grid=grid,
        in_specs=list(in_specs) + [pl.BlockSpec(memory_space=pl.ANY)] * n_stacked,
        out_specs=out_specs, out_shape=out_shape, scratch_shapes=scratch_shapes,
        input_output_aliases={n_in + i: i for i in range(n_stacked)},
        compiler_params=_cparams(sem))
    return lambda *args: call(*args, *stacked)


def _mm_kernel(x_ref, w_ref, o_ref):
    w = w_ref[...].astype(BF16)
    for rs in _row_chunks(o_ref.shape[0]):
        o_ref[rs, :] = _dot(x_ref[rs, :], w).astype(o_ref.dtype)


def _mm_res_kernel(x_ref, w_ref, r_ref, o_ref, *, scale):
    w = w_ref[...].astype(BF16)
    for rs in _row_chunks(o_ref.shape[0]):
        o_ref[rs, :] = ALPHA * r_ref[rs, :] + scale * _dot(x_ref[rs, :], w)


def _matmul(x, w, *, tm, tn, out_dtype, single_buffer_x=False, residual=None, scale=1.0):
    M, K = x.shape
    N = w.shape[1]
    assert N % tn == 0 and M % tm == 0
    xmode = dict(pipeline_mode=pl.Buffered(1)) if single_buffer_x else {}
    tile = pl.BlockSpec((tm, tn), lambda i, j: (i, j))
    in_specs = [pl.BlockSpec((tm, K), lambda i, j: (i, 0), **xmode),
                pl.BlockSpec((K, tn), lambda i, j: (0, j))]
    if residual is None:
        body, args = _mm_kernel, (x, w)
    else:
        assert out_dtype == F32
        body, args = functools.partial(_mm_res_kernel, scale=scale), (x, w, residual)
        in_specs.append(tile)
    return pl.pallas_call(
        body,
        grid=(M // tm, N // tn),
        in_specs=in_specs,
        out_specs=tile,
        out_shape=jax.ShapeDtypeStruct((M, N), out_dtype),
        compiler_params=_cparams(("parallel", "arbitrary")),
    )(*args)


def _mm_nt_kernel(x_ref, wt_ref, o_ref):
    wt = wt_ref[...].astype(BF16)
    for rs in _row_chunks(o_ref.shape[0]):
        o_ref[rs, :] = _dot_nt(x_ref[rs, :], wt).astype(o_ref.dtype)


def _matmul_nt(x, wt, *, row0, n_rows, tm, tn, out_dtype):
    M, K = x.shape
    assert n_rows % tn == 0 and M % tm == 0 and row0 % SUBLANES == 0
    return pl.pallas_call(
        _mm_nt_kernel,
        grid=(M // tm, n_rows // tn),
        in_specs=[pl.BlockSpec((tm, K), lambda i, j: (i, 0), pipeline_mode=pl.Buffered(1)),
                  pl.BlockSpec((pl.Element(tn), pl.Element(K)), lambda i, j: (pl.multiple_of(row0 + j * tn, SUBLANES), 0))],
        out_specs=pl.BlockSpec((tm, tn), lambda i, j: (i, j)),
        out_shape=jax.ShapeDtypeStruct((M, n_rows), out_dtype),
        compiler_params=_cparams(("parallel", "arbitrary")),
    )(x, wt)


def _two_source_specs(xa, xb, tr):
    D = xa.shape[1]
    na, nb = xa.shape[0] // tr, xb.shape[0] // tr
    return (na, na + nb,
            pl.BlockSpec((tr, D), lambda i: (jnp.minimum(i, na - 1), 0)),
            pl.BlockSpec((tr, D), lambda i: (jnp.maximum(i - na, 0), 0)))


def _stacked_rows(xa_ref, xb_ref, n_first):
    return jnp.where(pl.program_id(0) < n_first, xa_ref[...], xb_ref[...])


def _stack_cast_kernel(xa_ref, xb_ref, o_ref, *, n_first):
    o_ref[...] = _stacked_rows(xa_ref, xb_ref, n_first).astype(o_ref.dtype)


def _stack_cast(xa, xb):
    D = xa.shape[1]
    tr = max(r for r in range(BF16_ROWS, LN_ROWS + 1, BF16_ROWS) if xa.shape[0] % r == 0 and xb.shape[0] % r == 0)
    na, n, spec_a, spec_b = _two_source_specs(xa, xb, tr)
    return pl.pallas_call(
        functools.partial(_stack_cast_kernel, n_first=na), grid=(n,),
        in_specs=[spec_a, spec_b],
        out_specs=pl.BlockSpec((tr, D), lambda i: (i, 0)),
        out_shape=jax.ShapeDtypeStruct((n * tr, D), BF16),
        compiler_params=_cparams(("parallel",)),
    )(xa, xb)


def _ffn_up_kernel(x_ref, w1_ref, w3_ref, o_ref):
    w1 = w1_ref[...].astype(BF16)
    w3 = w3_ref[...].astype(BF16)
    for rs in _row_chunks(o_ref.shape[0]):
        x = x_ref[rs, :]
        o_ref[rs, :] = (_silu(_dot(x, w1)) * _dot(x, w3)).astype(o_ref.dtype)


def _ffn_up(x, w1, w3, *, tm, tn):
    M, K = x.shape
    N = w1.shape[1]
    return pl.pallas_call(
        _ffn_up_kernel,
        grid=(M // tm, N // tn),
        in_specs=[pl.BlockSpec((tm, K), lambda i, j: (i, 0), pipeline_mode=pl.Buffered(1)),
                  pl.BlockSpec((K, tn), lambda i, j: (0, j)),
                  pl.BlockSpec((K, tn), lambda i, j: (0, j))],
        out_specs=pl.BlockSpec((tm, tn), lambda i, j: (i, j)),
        out_shape=jax.ShapeDtypeStruct((M, N), BF16),
        compiler_params=_cparams(("parallel", "arbitrary")),
    )(x, w1, w3)


def _gate_kernel(ys_ref, ya_ref, ws_ref, wa_ref, gs_ref, ga_ref, o_ref):
    ws = ws_ref[...].astype(BF16)
    wa = wa_ref[...].astype(BF16)
    for rs in _row_chunks(o_ref.shape[0]):
        s = _dot(ys_ref[rs, :], ws)
        a = _dot(ya_ref[rs, :], wa)
        o_ref[rs, :] = (_sigmoid(gs_ref[rs, :]) * s + _sigmoid(ga_ref[rs, :]) * a).astype(o_ref.dtype)


def _gated_merge(y_ssd, y_att, w_ssd, w_att, u_gate, gs_col, ga_col, *, tm, tn):
    M, K = y_ssd.shape
    N = w_ssd.shape[1]
    gs_blk, ga_blk = gs_col // tn, ga_col // tn
    once = dict(pipeline_mode=pl.Buffered(1))
    return pl.pallas_call(
        _gate_kernel,
        grid=(M // tm, N // tn),
        in_specs=[pl.BlockSpec((tm, K), lambda i, j: (i, 0), **once),
                  pl.BlockSpec((tm, K), lambda i, j: (i, 0), **once),
                  pl.BlockSpec((K, tn), lambda i, j: (0, j)),
                  pl.BlockSpec((K, tn), lambda i, j: (0, j)),
                  pl.BlockSpec((tm, tn), lambda i, j: (i, gs_blk + j)),
                  pl.BlockSpec((tm, tn), lambda i, j: (i, ga_blk + j))],
        out_specs=pl.BlockSpec((tm, tn), lambda i, j: (i, j)),
        out_shape=jax.ShapeDtypeStruct((M, N), BF16),
        compiler_params=_cparams(("parallel", "arbitrary")),
    )(y_ssd, y_att, w_ssd, w_att, u_gate, u_gate)


def _layer_norm_rows(y, g, b):
    mu = jnp.mean(y, axis=-1, keepdims=True)
    yc = y - mu
    var = jnp.mean(yc * yc, axis=-1, keepdims=True)
    return yc * lax.rsqrt(var + LN_EPS) * g + b


def _ln_two_source_kernel(xa_ref, xb_ref, d_ref, g_ref, b_ref, o_ref, ob_ref, *, scale, n_first):
    x = _stacked_rows(xa_ref, xb_ref, n_first)
    o = _layer_norm_rows(ALPHA * x + scale * d_ref[...], g_ref[...], b_ref[...])
    o_ref[...] = o
    ob_ref[...] = o.astype(BF16)


def _res_layer_norm_stacking(xa, xb, delta, g, b, *, scale):
    D = xa.shape[1]
    tr = max(r for r in range(BF16_ROWS, LN_ROWS + 1, BF16_ROWS) if xa.shape[0] % r == 0 and xb.shape[0] % r == 0)
    na, n, spec_a, spec_b = _two_source_specs(xa, xb, tr)
    row = pl.BlockSpec((tr, D), lambda i: (i, 0))
    vec = pl.BlockSpec((1, D), lambda i: (0, 0))
    return pl.pallas_call(
        functools.partial(_ln_two_source_kernel, scale=scale, n_first=na), grid=(n,),
        in_specs=[spec_a, spec_b, row, vec, vec],
        out_specs=[row, row],
        out_shape=[jax.ShapeDtypeStruct((n * tr, D), F32), jax.ShapeDtypeStruct((n * tr, D), BF16)],
        compiler_params=_cparams(("parallel",)),
    )(xa, xb, delta, g.reshape(1, D), b.reshape(1, D))


def _ln_presummed_kernel(y_ref, g_ref, b_ref, *o_refs):
    o = _layer_norm_rows(y_ref[...], g_ref[...], b_ref[...])
    for o_ref in o_refs:
        o_ref[...] = o.astype(o_ref.dtype)


def _layer_norm(y, g, b, *, n_rows, row0=0, emit_bf16=True):
    D = y.shape[1]
    tr = max(r for r in range(SUBLANES, LN_ROWS + 1, SUBLANES) if n_rows % r == 0 and row0 % r == 0)
    yb = row0 // tr
    vec = pl.BlockSpec((1, D), lambda i: (0, 0))
    dts = (F32, BF16) if emit_bf16 else (F32,)
    return pl.pallas_call(
        _ln_presummed_kernel, grid=(n_rows // tr,),
        in_specs=[pl.BlockSpec((tr, D), lambda i: (yb + i, 0)), vec, vec],
        out_specs=[pl.BlockSpec((tr, D), lambda i: (i, 0)) for _ in dts],
        out_shape=[jax.ShapeDtypeStruct((n_rows, D), dt) for dt in dts],
        compiler_params=_cparams(("parallel",)),
    )(y, g.reshape(1, D), b.reshape(1, D))


def _memattn_kernel(q_ref, k_ref, v_ref, o_ref):
    dh = q_ref.shape[1] // MEM_HEADS
    scale = dh ** -0.5
    for h in range(MEM_HEADS):
        sl = slice(h * dh, (h + 1) * dh)
        q = q_ref[:, sl]
        k = k_ref[:, sl].astype(BF16)
        v = v_ref[:, sl].astype(BF16)
        s = _dot_nt(q, k) * scale
        m = jnp.max(s, axis=-1, keepdims=True)
        p = jnp.exp(s - m)
        p = p / jnp.sum(p, axis=-1, keepdims=True)
        o_ref[:, sl] = _dot(p.astype(BF16), v).astype(o_ref.dtype)


def _memory_attention(q, mem_k, mem_v, *, n_batch, rows_per_batch, row0, tq, stacked=None):
    M, D = q.shape
    mt = mem_k.shape[0] // n_batch
    nq = rows_per_batch // tq
    blk0 = row0 // tq
    qspec = pl.BlockSpec((tq, D), lambda b, j: (blk0 + b * nq + j, 0))
    mspec = pl.BlockSpec((mt, D), lambda b, j: (b, 0))
    call = _stacked_call(
        _memattn_kernel, grid=(n_batch, nq), in_specs=[qspec, mspec, mspec], out_specs=[qspec],
        out_shape=[jax.ShapeDtypeStruct((M, D), BF16)], stacked=stacked, n_stacked=1,
        sem=("parallel", "arbitrary"))
    return call(q, mem_k, mem_v)[0]


def _ssd_kernel(z_ref, xs_ref, bc_ref, us_ref, dtT_ref, cprev_ref, h0_ref,
                convw_ref, convb_ref, dtb_ref, alog_ref, dtbT_ref, alogT_ref, dskip_ref, ng_ref,
                expP_ref, expT_ref,
                y_ref, cnew_ref, h_ref, extx_s, extbc_s, *, T, n_heads):
    P, R, N = SSD_HEAD_DIM, SSD_HEADS_PER_GROUP, SSD_STATE
    G = n_heads // R
    DI = n_heads * P
    GN = G * N
    c = pl.program_id(1)
    W1 = SSD_CONV_W - 1
    HT = SUBLANES
    base = HT - W1

    @pl.when(c == 0)
    def _():
        extx_s[0:base, :] = jnp.zeros((base, DI), F32)
        extbc_s[0:base, :] = jnp.zeros((base, 2 * GN), F32)
        extx_s[base:HT, :] = cprev_ref[0, :, :DI]
        extbc_s[base:HT, :] = cprev_ref[0, :, DI:]
        h_ref[0] = h0_ref[0]

    def conv(hist_s, x, lo, hi):
        xx = jnp.concatenate([hist_s[...], x], axis=0)
        out = convb_ref[:, lo:hi]
        for j in range(SSD_CONV_W):
            tap = xx if j == W1 else pltpu.roll(xx, W1 - j, axis=0)
            out = out + convw_ref[j:j + 1, lo:hi] * tap[HT:HT + T]
        return _silu(out)

    x_in = xs_ref[...]
    bc_in = bc_ref[...]
    xs = conv(extx_s, x_in, 0, DI)
    bc = conv(extbc_s, bc_in, DI, DI + 2 * GN)
    newx = x_in[T - W1:T]
    newbc = bc_in[T - W1:T]
    cnew_ref[0, :, :DI] = newx
    cnew_ref[0, :, DI:] = newbc
    extx_s[base:HT, :] = newx
    extbc_s[base:HT, :] = newbc

    def expand(vals, e_ref):
        pieces = []
        for v in vals:
            for _ in range(3):
                p = v.astype(BF16)
                pieces.append(p)
                v = v - p.astype(F32)
        y = _dot(jnp.concatenate(pieces, axis=0), e_ref[...])
        return [y[(3 * n) * T:(3 * n + 1) * T] + y[(3 * n + 1) * T:(3 * n + 2) * T] + y[(3 * n + 2) * T:(3 * n + 3) * T]
                for n in range(len(vals))]

    dt = _softplus(us_ref[:, US_DT:US_DT + n_heads] + dtb_ref[...])
    d_a = dt * (-jnp.exp(alog_ref[...]))
    row = lax.broadcasted_iota(I32, (T, T), 0)
    col = lax.broadcasted_iota(I32, (T, T), 1)
    tri = (row >= col).astype(F32)
    a_cum = _dot_hi(tri, d_a)
    a_exp, dt_exp = expand([a_cum, dt], expP_ref)
    a_expt = a_exp if T == P else expand([a_cum], expT_ref)[0]
    d_a_t = _softplus(dtT_ref[0] + dtbT_ref[...]) * (-jnp.exp(alogT_ref[...]))
    r2 = lax.broadcasted_iota(I32, (2 * T, 2 * T), 0)
    c2 = lax.broadcasted_iota(I32, (2 * T, 2 * T), 1)
    tri2 = ((r2 // T == c2 // T) & (r2 <= c2)).astype(F32)
    a_cum_t = _dot_hi(d_a_t, tri2)

    l_idx = lax.broadcasted_iota(I32, (T, 2 * T), 0)
    j_idx = lax.broadcasted_iota(I32, (T, 2 * T), 1)
    causal2 = l_idx >= (j_idx % T)
    rr = lax.broadcasted_iota(I32, (2 * T, 2 * P), 0)
    cc = lax.broadcasted_iota(I32, (2 * T, 2 * P), 1)
    blockdiag = (rr // T) == (cc // P)
    chunk_decay = jnp.exp(a_cum_t)

    xdt = xs * dt_exp
    for g in range(G):
        gs = slice(g * R * P, (g + 1) * R * P)
        b_g = bc[:, g * N:(g + 1) * N].astype(BF16)
        c_g = bc[:, GN + g * N:GN + (g + 1) * N].astype(BF16)
        cb2 = _dot_nt(c_g, jnp.concatenate([b_g, b_g], axis=0))
        h_g = h_ref[0, gs, :]
        y_off = _dot_nt(c_g, h_g.astype(BF16)) * jnp.exp(a_exp[:, gs])
        pairs = []
        for pr in range(R // 2):
            i = g * (R // 2) + pr
            seg = a_expt[:, i * 2 * T:(i + 1) * 2 * T] - a_cum_t[i:i + 1, :]
            decay = jnp.exp(jnp.where(causal2, seg, -jnp.inf))
            m_pair = (cb2 * decay).astype(BF16)
            x_pair = xdt[:, i * 2 * P:(i + 1) * 2 * P]
            rhs = jnp.where(blockdiag, jnp.concatenate([x_pair, x_pair], axis=0), 0.0).astype(BF16)
            pairs.append(_dot(m_pair, rhs))
        y_g = jnp.concatenate(pairs, axis=1) + y_off + dskip_ref[:, gs] * xs[:, gs]
        a_g = a_exp[:, gs]
        dte = jnp.exp(a_g[T - 1:T, :] - a_g)
        st = _dot_tn((xdt[:, gs] * dte).astype(BF16), b_g)
        dec = jnp.concatenate(
            [jnp.broadcast_to(chunk_decay[(g * R + r) // 2:(g * R + r) // 2 + 1,
                                          ((g * R + r) % 2) * T + T - 1:((g * R + r) % 2) * T + T], (P, N))
             for r in range(R)], axis=0)
        h_ref[0, gs, :] = dec * h_g + st
        yg = y_g * _silu(z_ref[:, gs])
        ms = jnp.mean(yg * yg, axis=-1, keepdims=True)
        y_ref[:, gs] = (yg * lax.rsqrt(ms + RMS_EPS) * ng_ref[:, gs]).astype(y_ref.dtype)


def _ssd_mixer(u_ssd, u_small, dt_t, conv_prev, h0, prm, cols, *, n_batch, seq, row0, T, stacked=None):
    M = u_ssd.shape[0]
    n_heads = prm['dtb'].shape[1]
    DI = n_heads * SSD_HEAD_DIM
    BCW = conv_prev.shape[2] - DI
    nc = seq // T
    blk0 = row0 // T

    def rows(width, colblk):
        return pl.BlockSpec((T, width), lambda b, c: (blk0 + b * nc + c, colblk))

    def full(a):
        return pl.BlockSpec(a.shape, lambda b, c: (0,) * a.ndim)

    params = [prm['conv_w'], prm['conv_b'], prm['dtb'], prm['alog'], prm['dtbT'], prm['alogT'],
              prm['dskip'], prm['ng'], prm['expP'], prm['expT']]
    in_specs = [rows(DI, cols['z'] // DI), rows(DI, cols['xs'] // DI), rows(BCW, cols['bc'] // BCW),
                rows(US_W, 0),
                pl.BlockSpec((1,) + dt_t.shape[1:], lambda b, c: (b * nc + c, 0, 0)),
                pl.BlockSpec((1,) + conv_prev.shape[1:], lambda b, c: (b, 0, 0)),
                pl.BlockSpec((1,) + h0.shape[1:], lambda b, c: (b, 0, 0))] + [full(a) for a in params]
    out_specs = [rows(DI, 0),
                 pl.BlockSpec((1,) + conv_prev.shape[1:], lambda b, c: (b, 0, 0)),
                 pl.BlockSpec((1,) + h0.shape[1:], lambda b, c: (b, 0, 0))]
    call = _stacked_call(
        functools.partial(_ssd_kernel, T=T, n_heads=n_heads),
        grid=(n_batch, nc), in_specs=in_specs, out_specs=out_specs,
        out_shape=[jax.ShapeDtypeStruct((M, DI), BF16),
                   jax.ShapeDtypeStruct(conv_prev.shape, F32),
                   jax.ShapeDtypeStruct(h0.shape, F32)],
        stacked=stacked, n_stacked=1, sem=("parallel", "arbitrary"),
        scratch_shapes=[pltpu.VMEM((SUBLANES, DI), F32), pltpu.VMEM((SUBLANES, BCW), F32)])
    return call(u_ssd, u_ssd, u_ssd, u_small, dt_t, conv_prev, h0, *params)


def _score_of_key(k):
    return lax.bitcast_convert_type(jnp.where(k >= 0, k, k ^ 0x7FFFFFFF), F32)


def _threshold_score(thr):
    return jnp.where(thr == INT_MIN, -jnp.inf, _score_of_key(thr))


def _head_weights(us_ref):
    return us_ref[:, US_WI:US_WI + IDX_HEADS] * ((IDX_HEADS * IDX_DIM) ** -0.5)


def _indexer_scores(qi_heads, wi, ki_blk):
    Q = qi_heads[0].shape[0]
    acc = None
    for h0 in range(0, IDX_HEADS, IDX_STACK):
        l = _dot_nt(jnp.concatenate(qi_heads[h0:h0 + IDX_STACK], axis=0), ki_blk)
        for n in range(IDX_STACK):
            t = jnp.maximum(l[n * Q:(n + 1) * Q], 0.0) * wi[:, h0 + n:h0 + n + 1]
            acc = t if acc is None else acc + t
    return acc


def _kth_largest(count_ge, shape, k):
    c0 = count_ge(jnp.zeros(shape, I32))
    t0 = jnp.where(c0 >= k, 0, INT_MIN).astype(I32)

    def body(i, t):
        cand = t | (jnp.int32(1) << (30 - i))
        return jnp.where(count_ge(cand) >= k, cand, t)

    return lax.fori_loop(0, 31, body, t0)


def _tie_cutoff(count_ge, count_tie_below, thr, k, n_keys, cut_s):
    cut_s[...] = jnp.full(thr.shape, INT_MAX, I32)
    n_ge = count_ge(thr)

    @pl.when(jnp.max(n_ge) > k)
    def _():
        n_gt = jnp.where(thr == INT_MAX, 0, count_ge(thr + 1))
        need = k - n_gt

        def body(i, v):
            cand = v | (jnp.int32(1) << (n_keys.bit_length() - 1 - i))
            return jnp.where(count_tie_below(cand) < need, cand, v)

        v = lax.fori_loop(0, n_keys.bit_length(), body, jnp.zeros(thr.shape, I32))
        cut_s[...] = jnp.where(n_ge > k, v + 1, INT_MAX)


def _selected(score, idx, thr_f, cut):
    return (score > -jnp.inf) & ((score > thr_f) | ((score == thr_f) & (idx < cut)))


def _limits(pos0, n_rows):
    pos = pos0 + lax.broadcasted_iota(I32, (n_rows, 1), 0)
    return (pos // CHUNK + 1) * CHUNK


def _stack_q_heads(q_ref, g, rows=slice(None)):
    parts = [q_ref[rows, (g * Q_PER_KV + i) * HEAD_DIM:(g * Q_PER_KV + i + 1) * HEAD_DIM] for i in range(Q_PER_KV)]
    return (jnp.concatenate(parts, axis=0) * (HEAD_DIM ** -0.5 * LOG2E)).astype(BF16)


def _dsa_prompt_kernel(q_ref, qi_ref, us_ref, ki_ref, k_ref, vt_ref, o_ref, key_s, bias_s, cut_s, *,
                       n_sel, kv_heads):
    QB, KB = DSA_QB, DSA_KB
    nkb_max = key_s.shape[0]
    j = pl.program_id(1)
    nkb = ((j + 1) * QB + KB - 1) // KB
    wi = _head_weights(us_ref)
    qi_heads = [qi_ref[:, h * IDX_DIM:(h + 1) * IDX_DIM].astype(BF16) for h in range(IDX_HEADS)]
    pos = j * QB + lax.broadcasted_iota(I32, (1, QB), 1)
    lim = (pos // CHUNK + 1) * CHUNK

    def score_blk(kb, carry):
        off = pl.multiple_of(kb * KB, KB)
        sc = _indexer_scores(qi_heads, wi, ki_ref[pl.ds(off, KB), :])
        s_idx = off + lax.broadcasted_iota(I32, (KB, QB), 0)
        key_s[kb] = jnp.where(s_idx < lim, sc.T, -jnp.inf)
        return carry

    lax.fori_loop(0, nkb, score_blk, 0)

    def count_ge(cand):
        cand_f = _score_of_key(cand)

        def body(kb, acc):
            m = (key_s[kb] >= cand_f).astype(I32)
            return acc + jnp.sum(m.reshape(KB // SUBLANES, SUBLANES, QB), axis=0)
        acc = lax.fori_loop(0, nkb, body, jnp.zeros((SUBLANES, QB), I32))
        return jnp.sum(acc, axis=0, keepdims=True)

    thr = _kth_largest(count_ge, (1, QB), n_sel)
    thr_f = _threshold_score(thr)

    def key_index(kb):
        return kb * KB + lax.broadcasted_iota(I32, (KB, QB), 0)

    def count_tie_below(c):
        def body(kb, acc):
            m = ((key_s[kb] == thr_f) & (key_index(kb) < c)).astype(I32)
            return acc + jnp.sum(m.reshape(KB // SUBLANES, SUBLANES, QB), axis=0)
        acc = lax.fori_loop(0, nkb, body, jnp.zeros((SUBLANES, QB), I32))
        return jnp.sum(acc, axis=0, keepdims=True)

    _tie_cutoff(count_ge, count_tie_below, thr, n_sel, nkb_max * KB, cut_s)
    cut = cut_s[...]

    def bias_blk(kb, carry):
        bias_s[kb] = jnp.where(_selected(key_s[kb], key_index(kb), thr_f, cut), 0.0, NEG)
        return carry

    lax.fori_loop(0, nkb, bias_blk, 0)

    cols = Q_PER_KV * QB
    GPL = DSA_GROUPS_PER_LOOP
    for g0 in range(0, kv_heads, GPL):
        groups = range(g0, g0 + GPL)
        qs = [_stack_q_heads(q_ref, g) for g in groups]

        def body(kb, carry):
            off = pl.multiple_of(kb * KB, KB)
            b = bias_s[kb]
            b4 = jnp.concatenate([b] * Q_PER_KV, axis=1)
            heads = [slice(g * HEAD_DIM, (g + 1) * HEAD_DIM) for g in groups]
            vrows = [slice(g * VROWS, (g + 1) * VROWS) for g in groups]

            def logits(n):
                return _dot_nt(k_ref[pl.ds(off, KB), heads[n]], qs[n]) + b4

            pending = [logits(n) for n in range(min(QK_AHEAD, GPL))]
            new = []
            for n in range(GPL):
                t = pending.pop(0)
                if n + QK_AHEAD < GPL:
                    pending.append(logits(n + QK_AHEAD))
                m, acc = carry[n]
                m_new = jnp.maximum(m, jnp.max(t, axis=0, keepdims=True))
                alpha = jnp.exp2(m - m_new)
                p = jnp.exp2(t - m_new)
                acc = alpha * acc + _dot(vt_ref[0, kb, vrows[n], :], p.astype(BF16))
                new.append((m_new, acc))
            return tuple(new)

        init = tuple((jnp.full((1, cols), NEG, F32), jnp.zeros((VROWS, cols), F32)) for _ in groups)
        res = lax.fori_loop(0, nkb, body, init)
        for n, g in enumerate(groups):
            acc = res[n][1]
            out = acc[:HEAD_DIM] / acc[HEAD_DIM:HEAD_DIM + 1]
            for i in range(Q_PER_KV):
                o_ref[:, (g * Q_PER_KV + i) * HEAD_DIM:(g * Q_PER_KV + i + 1) * HEAD_DIM] = (
                    out[:, i * QB:(i + 1) * QB].T.astype(o_ref.dtype))


def _dsa_prompt(u_att, u_small, ki_b, k_b, vt_b, cols, *, n_batch, seq, kv_heads):
    M = u_att.shape[0]
    QB, KB = DSA_QB, DSA_KB
    nq = seq // QB
    nkb_max = seq // KB
    DQ = kv_heads * Q_PER_KV * HEAD_DIM
    DK = kv_heads * HEAD_DIM
    DQI = IDX_HEADS * IDX_DIM
    n_sel = min(TOPK_MAX, seq // 4)
    return pl.pallas_call(
        functools.partial(_dsa_prompt_kernel, n_sel=n_sel, kv_heads=kv_heads),
        grid=(n_batch, nq),
        in_specs=[pl.BlockSpec((QB, DQ), lambda b, j: (b * nq + j, cols['q'] // DQ)),
                  pl.BlockSpec((QB, DQI), lambda b, j: (b * nq + j, cols['qi'] // DQI)),
                  pl.BlockSpec((QB, US_W), lambda b, j: (b * nq + j, 0)),
                  pl.BlockSpec((seq, IDX_DIM), lambda b, j: (b, 0), pipeline_mode=pl.Buffered(1)),
                  pl.BlockSpec((seq, DK), lambda b, j: (b, 0), pipeline_mode=pl.Buffered(1)),
                  pl.BlockSpec((1, nkb_max, kv_heads * VROWS, KB), lambda b, j: (b, 0, 0, 0),
                               pipeline_mode=pl.Buffered(1))],
        out_specs=pl.BlockSpec((QB, DQ), lambda b, j: (b * nq + j, 0)),
        out_shape=jax.ShapeDtypeStruct((M, DQ), BF16),
        scratch_shapes=[pltpu.VMEM((nkb_max, KB, QB), F32), pltpu.VMEM((nkb_max, KB, QB), F32),
                        pltpu.VMEM((1, QB), I32)],
        compiler_params=_cparams(("parallel", "arbitrary")),
    )(u_att, u_att, u_small, ki_b, k_b, vt_b)


def _dsa_sample_kernel(q_ref, qi_ref, us_ref, kn_ref, vn_ref, ck_ref, cv_ref, cki_ref, o_ref, cut_s, *,
                       n_sel, kv_heads, past, n_streams):
    R = q_ref.shape[0]
    Q = R // n_streams
    wi = _head_weights(us_ref)
    ki_new = us_ref[:, US_KI:US_KI + IDX_DIM].astype(BF16)
    sc_p, sc_n = [], []
    for s in range(n_streams):
        rs = slice(s * Q, (s + 1) * Q)
        qi_heads = [qi_ref[rs, h * IDX_DIM:(h + 1) * IDX_DIM].astype(BF16) for h in range(IDX_HEADS)]
        sc_p.append(_indexer_scores(qi_heads, wi[rs], cki_ref[s].astype(BF16)))
        sc_n.append(_indexer_scores(qi_heads, wi[rs], ki_new[rs]))
    sc_p = jnp.concatenate(sc_p, axis=0)
    sc_n = jnp.concatenate(sc_n, axis=0)
    lim = jnp.concatenate([_limits(past, Q)] * n_streams, axis=0)
    idx_p = lax.broadcasted_iota(I32, (R, past), 1)
    idx_n = past + lax.broadcasted_iota(I32, (R, Q), 1)
    key_p = jnp.where(idx_p < lim, sc_p, -jnp.inf)
    key_n = jnp.where(idx_n < lim, sc_n, -jnp.inf)

    def count_ge(cand):
        cand_f = _score_of_key(cand)
        return (jnp.sum((key_p >= cand_f).astype(I32), axis=1, keepdims=True)
                + jnp.sum((key_n >= cand_f).astype(I32), axis=1, keepdims=True))

    thr = _kth_largest(count_ge, (R, 1), n_sel)
    thr_f = _threshold_score(thr)

    def count_tie_below(c):
        return (jnp.sum(((key_p == thr_f) & (idx_p < c)).astype(I32), axis=1, keepdims=True)
                + jnp.sum(((key_n == thr_f) & (idx_n < c)).astype(I32), axis=1, keepdims=True))

    _tie_cutoff(count_ge, count_tie_below, thr, n_sel, past + Q, cut_s)
    cut = cut_s[...]
    bias_p = jnp.where(_selected(key_p, idx_p, thr_f, cut), 0.0, NEG)
    bias_n = jnp.where(_selected(key_n, idx_n, thr_f, cut), 0.0, NEG)

    for s in range(n_streams):
        rs = slice(s * Q, (s + 1) * Q)
        b_p = jnp.concatenate([bias_p[rs]] * Q_PER_KV, axis=0)
        b_n = jnp.concatenate([bias_n[rs]] * Q_PER_KV, axis=0)
        for g in range(kv_heads):
            hs = slice(g * HEAD_DIM, (g + 1) * HEAD_DIM)
            qs = _stack_q_heads(q_ref, g, rs)
            ck = ck_ref[s, pl.ds(g, past, stride=kv_heads), :].astype(BF16)
            cv = cv_ref[s, pl.ds(g, past, stride=kv_heads), :].astype(BF16)
            t_p = _dot_nt(qs, ck) + b_p
            t_n = _dot_nt(qs, kn_ref[rs, hs].astype(BF16)) + b_n
            m = jnp.maximum(jnp.max(t_p, axis=-1, keepdims=True), jnp.max(t_n, axis=-1, keepdims=True))
            p_p = jnp.exp2(t_p - m)
            p_n = jnp.exp2(t_n - m)
            l = jnp.sum(p_p, axis=-1, keepdims=True) + jnp.sum(p_n, axis=-1, keepdims=True)
            acc = _dot(p_p.astype(BF16), cv) + _dot(p_n.astype(BF16), vn_ref[rs, hs].astype(BF16))
            out = acc / l
            for i in range(Q_PER_KV):
                o_ref[rs, (g * Q_PER_KV + i) * HEAD_DIM:(g * Q_PER_KV + i + 1) * HEAD_DIM] = (
                    out[i * Q:(i + 1) * Q].astype(o_ref.dtype))


def _dsa_sample(u_att, u_small, cache_k, cache_v, cache_ki, cols, *, n_batch, seq, row0, kv_heads, stacked):
    past = cache_ki.shape[1]
    DQ = kv_heads * Q_PER_KV * HEAD_DIM
    DK = kv_heads * HEAD_DIM
    DQI = IDX_HEADS * IDX_DIM
    n_sel = min(TOPK_MAX, (past + seq) // 4)
    ns = DSA_SAMPLE_STREAMS
    rows = ns * seq
    assert n_batch % ns == 0 and row0 % rows == 0
    blk0 = row0 // rows
    call = _stacked_call(
        functools.partial(_dsa_sample_kernel, n_sel=n_sel, kv_heads=kv_heads, past=past, n_streams=ns),
        grid=(n_batch // ns,),
        in_specs=[pl.BlockSpec((rows, DQ), lambda b: (blk0 + b, cols['q'] // DQ)),
                  pl.BlockSpec((rows, DQI), lambda b: (blk0 + b, cols['qi'] // DQI)),
                  pl.BlockSpec((rows, US_W), lambda b: (blk0 + b, 0)),
                  pl.BlockSpec((rows, DK), lambda b: (blk0 + b, cols['k'] // DK)),
                  pl.BlockSpec((rows, DK), lambda b: (blk0 + b, cols['v'] // DK)),
                  pl.BlockSpec((ns, past * kv_heads, HEAD_DIM), lambda b: (b, 0, 0)),
                  pl.BlockSpec((ns, past * kv_heads, HEAD_DIM), lambda b: (b, 0, 0)),
                  pl.BlockSpec((ns, past, IDX_DIM), lambda b: (b, 0, 0))],
        out_specs=[pl.BlockSpec((rows, DQ), lambda b: (blk0 + b, 0))],
        out_shape=[jax.ShapeDtypeStruct(stacked[0].shape, BF16)],
        stacked=stacked, n_stacked=1, sem=("parallel",), scratch_shapes=[pltpu.VMEM((rows, 1), I32)])
    return call(u_att, u_att, u_small, u_att, u_att, cache_k, cache_v, cache_ki)[0]


def _pair_layout(v, T):
    return jnp.repeat(v.reshape(-1, 2), T, axis=1)


def _dt_transposed(dt_raw, T):
    n, H = dt_raw.shape
    return dt_raw.reshape(n // T, T, H // 2, 2).transpose(0, 2, 3, 1).reshape(n // T, H // 2, 2 * T)


def _ssd_params(conv_w, conv_b, dt_bias, a_log, d_skip, norm_g, T):
    H = dt_bias.shape[0]
    P = SSD_HEAD_DIM
    eye = jnp.eye(H, dtype=F32)
    return dict(conv_w=conv_w, conv_b=conv_b.reshape(1, -1),
                dtb=dt_bias.reshape(1, H), alog=a_log.reshape(1, H),
                dtbT=_pair_layout(dt_bias, T), alogT=_pair_layout(a_log, T),
                dskip=jnp.repeat(d_skip, P).reshape(1, H * P), ng=norm_g.reshape(1, -1),
                expP=jnp.repeat(eye, P, axis=1).astype(BF16), expT=jnp.repeat(eye, T, axis=1).astype(BF16))


def kernel(x_prompt, x_sample, mem_prompt, cache_k, cache_v, cache_idx_k, cache_mem_k, cache_mem_v,
           state_ssm, state_conv, ln1_g, ln1_b, ffn1_w1, ffn1_w3, ffn1_w2, w_in, conv_w, conv_b,
           dt_bias, a_log, d_skip, ssd_norm_g, w_ssd_br, w_att_br, w_out, ln2_g, ln2_b,
           w_mq, w_mk, w_mv, w_mo, ln3_g, ln3_b, ffn2_w1, ffn2_w3, ffn2_w2, ln4_g, ln4_b):
    assert x_prompt.ndim == 3 and ln1_g.shape[0] == DEPTH == 1
    l = 0
    BP, LP, D = x_prompt.shape
    BS, LS, _ = x_sample.shape
    MP, MS = BP * LP, BS * LS
    M = MP + MS
    H = dt_bias.shape[1]
    assert H == SSD_HEADS
    DI = H * SSD_HEAD_DIM
    BCW = conv_w.shape[2] - DI
    KVH = cache_k.shape[3]
    DQ = KVH * Q_PER_KV * HEAD_DIM
    DK = KVH * HEAD_DIM
    DQI = IDX_HEADS * IDX_DIM
    MT = mem_prompt.shape[1]
    PAST = cache_k.shape[2]
    tm_big = M // 8
    tm_half = M // 16

    w0t = w_in[l].T
    c_dt = 2 * DI + BCW
    c_q = c_dt + H
    c_wi = c_q + DQ + 2 * DK + DQI
    c_ki = c_wi + IDX_HEADS
    c_g = c_ki + IDX_DIM
    assert c_g + 2 * D == w0t.shape[0]
    cols_ssd = dict(z=0, xs=DI, bc=2 * DI)
    cols_att = dict(q=0, k=DQ, v=DQ + DK, qi=DQ + 2 * DK)
    n_small = H + IDX_HEADS + IDX_DIM
    assert n_small <= US_W
    w_small_t = jnp.concatenate([w0t[c_dt:c_q], w0t[c_wi:c_g], jnp.zeros((US_W - n_small, D), F32)], axis=0)

    x_p2, x_s2 = x_prompt.reshape(MP, D), x_sample.reshape(MS, D)
    x0b = _stack_cast(x_p2, x_s2)

    def ffn_delta(xb, w1, w3, w2, residual=None):
        h = _ffn_up(xb, w1[l], w3[l], tm=M // 4, tn=256)
        return _matmul(h, w2[l], tm=tm_big, tn=256, out_dtype=F32, single_buffer_x=True,
                       residual=residual, scale=0.5)

    d1 = ffn_delta(x0b, ffn1_w1, ffn1_w3, ffn1_w2)
    x1, x1b = _res_layer_norm_stacking(x_p2, x_s2, d1, ln1_g[l], ln1_b[l], scale=0.5)

    u_ssd = _matmul_nt(x1b, w0t, row0=0, n_rows=c_dt, tm=M // 4, tn=512, out_dtype=F32)
    u_att = _matmul_nt(x1b, w0t, row0=c_q, n_rows=c_wi - c_q, tm=M // 4, tn=512, out_dtype=F32)
    u_gate = _matmul_nt(x1b, w0t, row0=c_g, n_rows=2 * D, tm=M // 4, tn=512, out_dtype=F32)
    u_small = _matmul_nt(x1b, w_small_t, row0=0, n_rows=US_W, tm=tm_big, tn=US_W, out_dtype=F32)

    dt_raw = u_small[:, US_DT:US_DT + H]
    zeros_conv = jnp.zeros((BP,) + state_conv.shape[2:], F32)
    zeros_h = jnp.zeros((BP, DI, SSD_STATE), F32)
    prm_p = _ssd_params(conv_w[l], conv_b[l], dt_bias[l], a_log[l], d_skip[l], ssd_norm_g[l], CHUNK)
    y_ssd, conv_p, h_p = _ssd_mixer(u_ssd, u_small, _dt_transposed(dt_raw[:MP], CHUNK), zeros_conv, zeros_h,
                                    prm_p, cols_ssd, n_batch=BP, seq=LP, row0=0, T=CHUNK)
    TS = min(CHUNK, LS)
    prm_s = _ssd_params(conv_w[l], conv_b[l], dt_bias[l], a_log[l], d_skip[l], ssd_norm_g[l], TS)
    y_ssd, conv_s, h_s = _ssd_mixer(u_ssd, u_small, _dt_transposed(dt_raw[MP:], TS), state_conv[l],
                                    state_ssm[l].reshape(BS, DI, SSD_STATE),
                                    prm_s, cols_ssd, n_batch=BS, seq=LS, row0=MP, T=TS, stacked=(y_ssd,))

    k_all = u_att[:, cols_att['k']:cols_att['k'] + DK]
    v_all = u_att[:, cols_att['v']:cols_att['v'] + DK]
    ki_all = u_small[:, US_KI:US_KI + IDX_DIM]
    nkb = LP // DSA_KB
    vt5 = v_all[:MP].astype(BF16).reshape(BP, nkb, DSA_KB, KVH, HEAD_DIM).transpose(0, 1, 3, 4, 2)
    ones_row = (lax.broadcasted_iota(I32, (BP, nkb, KVH, VPAD, DSA_KB), 3) == 0).astype(BF16)
    vt_b = jnp.concatenate([vt5, ones_row], axis=3).reshape(BP, nkb, KVH * VROWS, DSA_KB)
    y_att = _dsa_prompt(u_att, u_small, ki_all[:MP].astype(BF16), k_all[:MP].astype(BF16),
                        vt_b, cols_att, n_batch=BP, seq=LP, kv_heads=KVH)
    y_att = _dsa_sample(u_att, u_small, cache_k[l].reshape(BS, PAST * KVH, HEAD_DIM),
                        cache_v[l].reshape(BS, PAST * KVH, HEAD_DIM), cache_idx_k[l], cols_att,
                        n_batch=BS, seq=LS, row0=MP, kv_heads=KVH, stacked=(y_att,))

    merged = _gated_merge(y_ssd, y_att, w_ssd_br[l], w_att_br[l], u_gate, 0, D, tm=tm_big, tn=256)
    s2 = _matmul(merged, w_out[l], tm=tm_big, tn=512, out_dtype=F32, residual=x1)
    x2, x2b = _layer_norm(s2, ln2_g[l], ln2_b[l], n_rows=M)

    memb = mem_prompt.reshape(BP * MT, D).astype(BF16)
    mk_p = _matmul(memb, w_mk[l], tm=BP * MT, tn=512, out_dtype=F32)
    mv_p = _matmul(memb, w_mv[l], tm=BP * MT, tn=512, out_dtype=F32)
    qm = _matmul(x2b, w_mq[l], tm=tm_big, tn=512, out_dtype=BF16)
    o_m = _memory_attention(qm, mk_p, mv_p, n_batch=BP, rows_per_batch=LP, row0=0, tq=min(512, LP))
    o_m = _memory_attention(qm, cache_mem_k[l].reshape(BS * MT, D), cache_mem_v[l].reshape(BS * MT, D),
                            n_batch=BS, rows_per_batch=LS, row0=MP, tq=LS, stacked=(o_m,))
    s3 = _matmul(o_m, w_mo[l], tm=tm_big, tn=512, out_dtype=F32, residual=x2)
    x3, x3b = _layer_norm(s3, ln3_g[l], ln3_b[l], n_rows=M)

    s4 = ffn_delta(x3b, ffn2_w1, ffn2_w3, ffn2_w2, residual=x3)
    y_p, = _layer_norm(s4, ln4_g[l], ln4_b[l], n_rows=MP, emit_bf16=False)
    y_s, = _layer_norm(s4, ln4_g[l], ln4_b[l], n_rows=MS, row0=MP, emit_bf16=False)

    mh = D // MEM_HEADS
    return (y_p.reshape(BP, LP, D), y_s.reshape(BS, LS, D),
            h_p.reshape(1, BP, H, SSD_HEAD_DIM, SSD_STATE), conv_p[None],
            k_all[:MP].reshape(1, BP, LP, KVH, HEAD_DIM), v_all[:MP].reshape(1, BP, LP, KVH, HEAD_DIM),
            ki_all[:MP].reshape(1, BP, LP, IDX_DIM),
            mk_p.reshape(1, BP, MT, MEM_HEADS, mh), mv_p.reshape(1, BP, MT, MEM_HEADS, mh),
            h_s.reshape(1, BS, H, SSD_HEAD_DIM, SSD_STATE), conv_s[None],
            k_all[MP:].reshape(1, BS, LS, KVH, HEAD_DIM), v_all[MP:].reshape(1, BS, LS, KVH, HEAD_DIM),
            ki_all[MP:].reshape(1, BS, LS, IDX_DIM))
```
